```python
import jax, jax.numpy as jnp
from jax import lax
import numpy as np

D_MODEL = 1024
BATCH = 8
SEQ = 2048
DEPTH = 4
DEC_BATCH = 8
DEC_SEQ = 64
PAST_LEN = 4096

CHUNK = 64
N_A = DEPTH // 2
N_B = DEPTH - N_A
EXPAND_A = 2
E_A = EXPAND_A * D_MODEL
POOL_WINDOWS = (2, 4, 8, 16)
N_POOL_GROUPS = len(POOL_WINDOWS)
G_A = E_A // N_POOL_GROUPS
POOL_HIST = max(POOL_WINDOWS) - 1
HEAD_DIM = 64
N_HEADS = D_MODEL // HEAD_DIM
E_B = N_HEADS * HEAD_DIM
N_LEFT_CHUNKS = 8
KV_ROWS = N_LEFT_CHUNKS * CHUNK
BAND = KV_ROWS + CHUNK
MAX_REL = 128
N_REL = 2 * MAX_REL + 1
EPS = 1e-6
NEG_INF = -1e30

kernel_name = "pool_yoco_chunk_band_encoder"


def rms_norm(x, g):
    xf = x.astype(jnp.float32)
    y = xf * lax.rsqrt(jnp.mean(xf * xf, axis=-1, keepdims=True) + EPS)
    return (y * g.astype(jnp.float32)).astype(x.dtype)


def multiscale_pool(u_pad, pos):
    T = pos.shape[0]
    uf = u_pad.astype(jnp.float32)
    B = uf.shape[0]
    cs = jnp.concatenate([jnp.zeros((B, 1, E_A), jnp.float32), jnp.cumsum(uf, axis=1)], axis=1)
    end = cs[:, POOL_HIST + 1:]
    outs = []
    for g, w in enumerate(POOL_WINDOWS):
        lo, hi = g * G_A, (g + 1) * G_A
        start = cs[:, POOL_HIST + 1 - w:POOL_HIST + 1 - w + T, lo:hi]
        cnt = jnp.minimum(pos + 1, w).astype(jnp.float32)[None, :, None]
        outs.append((end[..., lo:hi] - start) / cnt)
    pooled = jnp.concatenate(outs, axis=-1)
    return (pooled - uf[:, POOL_HIST:]).astype(u_pad.dtype)


def pool_mixer_layer(x, u_hist, pos, g_norm, w_in, w_grp, scale, w_out):
    h = rms_norm(x, g_norm)
    uz = h @ w_in
    u, z = uz[..., :E_A], uz[..., E_A:]
    u_pad = jnp.concatenate([u_hist.astype(u.dtype), u], axis=1)
    p = multiscale_pool(u_pad, pos)
    B, T, _ = p.shape
    p = jnp.einsum('btgc,gcd->btgd', p.reshape(B, T, N_POOL_GROUPS, G_A), w_grp).reshape(B, T, E_A) * scale
    y = p * jax.nn.silu(z)
    return x + y @ w_out, u_pad[:, -POOL_HIST:]


def shared_kv(x, g_kv, w_kv, g_k):
    h = rms_norm(x, g_kv)
    kv = h @ w_kv
    B, T, _ = kv.shape
    k = rms_norm(kv[..., :E_B].reshape(B, T, N_HEADS, HEAD_DIM), g_k)
    v = kv[..., E_B:].reshape(B, T, N_HEADS, HEAD_DIM)
    return k, v


def band_mask_rel(qpos, kpos):
    qc = qpos[:, None] // CHUNK
    kc = kpos[None, :] // CHUNK
    valid = (kc <= qc) & (kc >= qc - N_LEFT_CHUNKS) & (kpos[None, :] >= 0)
    rel = jnp.clip(qpos[:, None] - kpos[None, :], -MAX_REL, MAX_REL) + MAX_REL
    return valid, rel


def attend(q, k, v, qpos, kpos, rel_bias):
    valid, rel = band_mask_rel(qpos, kpos)
    s = jnp.einsum('bqhd,bkhd->bhqk', q, k).astype(jnp.float32) * (HEAD_DIM ** -0.5)
    bias = jnp.transpose(rel_bias[rel], (2, 0, 1)).astype(jnp.float32)
    s = jnp.where(valid[None, None], s + bias[None], NEG_INF)
    p = jax.nn.softmax(s, axis=-1).astype(v.dtype)
    return jnp.einsum('bhqk,bkhd->bqhd', p, v)


def chunk_band_attention_prompt(q, k, v, rel_bias):
    B, S = q.shape[0], q.shape[1]
    nc = S // CHUNK
    pad = jnp.zeros((B, KV_ROWS, N_HEADS, HEAD_DIM), k.dtype)
    k_pad = jnp.concatenate([pad, k], axis=1)
    v_pad = jnp.concatenate([pad.astype(v.dtype), v], axis=1)

    def one_chunk(c):
        start = c * CHUNK
        q_c = lax.dynamic_slice_in_dim(q, start, CHUNK, axis=1)
        k_b = lax.dynamic_slice_in_dim(k_pad, start, BAND, axis=1)
        v_b = lax.dynamic_slice_in_dim(v_pad, start, BAND, axis=1)
        qpos = start + jnp.arange(CHUNK, dtype=jnp.int32)
        kpos = start - KV_ROWS + jnp.arange(BAND, dtype=jnp.int32)
        return attend(q_c, k_b, v_b, qpos, kpos, rel_bias)

    out = lax.map(one_chunk, jnp.arange(nc, dtype=jnp.int32))
    return jnp.transpose(out, (1, 0, 2, 3, 4)).reshape(B, S, N_HEADS, HEAD_DIM)


def attn_mixer_layer(x, band_fn, g_norm, w_in, g_q, rel_bias, w_out):
    h = rms_norm(x, g_norm)
    qz = h @ w_in
    B, T, _ = qz.shape
    q = rms_norm(qz[..., :E_B].reshape(B, T, N_HEADS, HEAD_DIM), g_q)
    z = qz[..., E_B:]
    o = band_fn(q, rel_bias).reshape(B, T, E_B)
    return x + (o * jax.nn.silu(z)) @ w_out


def run_trunk(x, pool_hist, cache_k, cache_v, pos0,
              norm_a, w_in_a, w_grp_a, scale_a, w_out_a,
              norm_kv, w_kv, g_k, norm_b, w_in_b, g_q, rel_bias_b, w_out_b):
    T = x.shape[1]
    pos = pos0 + jnp.arange(T, dtype=jnp.int32)
    new_hist = []
    k = v = None
    band_fn = None
    for layer in range(DEPTH):
        if layer < N_A:
            x, hist = pool_mixer_layer(x, pool_hist[layer], pos, norm_a[layer], w_in_a[layer],
                                       w_grp_a[layer], scale_a[layer], w_out_a[layer])
            new_hist.append(hist)
            if layer == N_A - 1:
                k, v = shared_kv(x, norm_kv, w_kv, g_k)
                if cache_k is None:
                    band_fn = lambda qq, rb, k=k, v=v: chunk_band_attention_prompt(qq, k, v, rb)
                else:
                    lc = cache_k.shape[1]
                    k_b = jnp.concatenate([cache_k.astype(k.dtype), k], axis=1)
                    v_b = jnp.concatenate([cache_v.astype(v.dtype), v], axis=1)
                    kpos = pos0 - lc + jnp.arange(lc + T, dtype=jnp.int32)
                    band_fn = lambda qq, rb, k_b=k_b, v_b=v_b, kpos=kpos: attend(qq, k_b, v_b, pos, kpos, rb)
        else:
            j = layer - N_A
            x = attn_mixer_layer(x, band_fn, norm_b[j], w_in_b[j], g_q[j], rel_bias_b[j], w_out_b[j])
    return x, jnp.stack(new_hist, axis=0), k, v


def setup_inputs(seed: int = 0) -> dict:
    key = jax.random.key(seed)
    ks = jax.random.split(key, 20)
    nrm = jax.random.normal
    f32 = jnp.float32
    lc = min(KV_ROWS, PAST_LEN)
    return {
        "x_prompt": nrm(ks[0], (BATCH, SEQ, D_MODEL), f32),
        "x_sample": nrm(ks[1], (DEC_BATCH, DEC_SEQ, D_MODEL), f32),
        "state_pool": nrm(ks[2], (N_A, DEC_BATCH, POOL_HIST, E_A), f32),
        "cache_k": nrm(ks[3], (DEC_BATCH, lc, N_HEADS, HEAD_DIM), f32),
        "cache_v": nrm(ks[4], (DEC_BATCH, lc, N_HEADS, HEAD_DIM), f32),
        "norm_a": 1.0 + 0.05 * nrm(ks[5], (N_A, D_MODEL), f32),
        "w_in_a": nrm(ks[6], (N_A, D_MODEL, 2 * E_A), f32) * D_MODEL ** -0.5,
        "w_grp_a": nrm(ks[7], (N_A, N_POOL_GROUPS, G_A, G_A), f32) * G_A ** -0.5,
        "scale_a": 1.0 + 0.1 * nrm(ks[8], (N_A, E_A), f32),
        "w_out_a": nrm(ks[9], (N_A, E_A, D_MODEL), f32) * E_A ** -0.5,
        "norm_kv": 1.0 + 0.05 * nrm(ks[10], (D_MODEL,), f32),
        "w_kv": nrm(ks[11], (D_MODEL, 2 * E_B), f32) * D_MODEL ** -0.5,
        "g_k": 1.0 + 0.05 * nrm(ks[12], (HEAD_DIM,), f32),
        "norm_b": 1.0 + 0.05 * nrm(ks[13], (N_B, D_MODEL), f32),
        "w_in_b": nrm(ks[14], (N_B, D_MODEL, 2 * E_B), f32) * D_MODEL ** -0.5,
        "g_q": 1.0 + 0.05 * nrm(ks[15], (N_B, HEAD_DIM), f32),
        "rel_bias_b": 0.5 * nrm(ks[16], (N_B, N_REL, N_HEADS), f32),
        "w_out_b": nrm(ks[17], (N_B, E_B, D_MODEL), f32) * E_B ** -0.5,
    }


def reference(x_prompt, x_sample, state_pool, cache_k, cache_v,
              norm_a, w_in_a, w_grp_a, scale_a, w_out_a,
              norm_kv, w_kv, g_k, norm_b, w_in_b, g_q, rel_bias_b, w_out_b):
    weights = (norm_a, w_in_a, w_grp_a, scale_a, w_out_a,
               norm_kv, w_kv, g_k, norm_b, w_in_b, g_q, rel_bias_b, w_out_b)
    zero_hist = jnp.zeros((N_A, x_prompt.shape[0], POOL_HIST, E_A), x_prompt.dtype)
    y_prompt, new_pool_p, k_p, v_p = run_trunk(x_prompt, zero_hist, None, None, 0, *weights)
    y_sample, new_pool_s, new_k_s, new_v_s = run_trunk(x_sample, state_pool, cache_k, cache_v, PAST_LEN, *weights)
    lp = min(KV_ROWS, x_prompt.shape[1])
    new_k_p = k_p[:, -lp:]
    new_v_p = v_p[:, -lp:]
    return (y_prompt, y_sample, new_pool_p, new_pool_s, new_k_p, new_v_p, new_k_s, new_v_s)
```

```python
import functools
import math

import jax
import jax.numpy as jnp
from jax import lax
from jax.experimental import pallas as pl
from jax.experimental.pallas import tpu as pltpu

D_MODEL = 1024
E_A = 2048
POOL_WINDOWS = (2, 4, 8, 16)
N_POOL_GROUPS = len(POOL_WINDOWS)
G_A = E_A // N_POOL_GROUPS
POOL_HIST = max(POOL_WINDOWS) - 1
HEAD_DIM = 64
N_HEADS = 16
E_B = N_HEADS * HEAD_DIM
CHUNK = 64
N_LEFT_CHUNKS = 8
KV_ROWS = N_LEFT_CHUNKS * CHUNK
MAX_REL = 128
N_REL = 2 * MAX_REL + 1
EPS = 1e-6
NEG_INF = -1e30
PAST_LEN = 4096
LOG2E = math.log2(math.e)

SUBLANES = 8
LANES = 128
HEADS_PER_LANE_BLOCK = LANES // HEAD_DIM
N_PAIRS = N_HEADS // HEADS_PER_LANE_BLOCK
HIST_PAD = 2 * SUBLANES
FRONT = SUBLANES
VMEM_LIMIT_BYTES = 56 * 1024 * 1024


def _bdot(a, b):
    return jnp.dot(a, b, preferred_element_type=jnp.float32)


def _rms(x, g):
    ms = jnp.mean(x * x, axis=-1, keepdims=True)
    return x * lax.rsqrt(ms + EPS) * g


def _silu(z):
    return z * (1.0 / (1.0 + jnp.exp(-z)))


def _pool_layer_kernel(*refs, n_seg, seg_len, pos0, has_hist):
    if has_hist:
        (x_ref, hist_ref, g_ref, win_ref, wgrp_ref, scale_ref, wout_ref,
         y_ref, hout_ref, acc_ref, upad_ref, s0_ref, s1_ref, carry_ref) = refs
    else:
        (x_ref, g_ref, win_ref, wgrp_ref, scale_ref, wout_ref,
         y_ref, hout_ref, acc_ref, upad_ref, s0_ref, s1_ref, carry_ref) = refs
        hist_ref = None
    t = pl.program_id(1)
    m = n_seg * seg_len
    rows = HIST_PAD + seg_len
    x = x_ref[...].reshape(m, D_MODEL)
    hb = _rms(x, g_ref[...]).astype(jnp.bfloat16)

    @pl.when(t == 0)
    def _():
        if has_hist:
            carry_ref[...] = hist_ref[...]
        else:
            carry_ref[...] = jnp.zeros_like(carry_ref)

    zero_front = jnp.zeros((n_seg, FRONT, G_A), jnp.float32)
    upad_ref[:, 0:FRONT, :] = zero_front
    s0_ref[:, 0:FRONT, :] = zero_front
    s1_ref[:, 0:FRONT, :] = zero_front

    pos = pos0 + t * seg_len + lax.broadcasted_iota(jnp.int32, (1, seg_len, 1), 1)

    for g, w in enumerate(POOL_WINDOWS):
        lo = g * G_A
        u = _bdot(hb, win_ref[:, lo:lo + G_A])
        z = _bdot(hb, win_ref[:, E_A + lo:E_A + lo + G_A])
        u3 = u.reshape(n_seg, seg_len, G_A)
        upad_ref[:, FRONT:FRONT + HIST_PAD, :] = carry_ref[:, :, lo:lo + G_A]
        upad_ref[:, FRONT + HIST_PAD:, :] = u3
        carry_ref[:, :, lo:lo + G_A] = upad_ref[:, FRONT + seg_len:, :]
        bufs = (upad_ref, s0_ref, s1_ref)
        src = 0
        for k in range(g + 1):
            dst = 1 if src != 1 else 2
            sh = 1 << k
            bufs[dst][:, FRONT:, :] = (bufs[src][:, FRONT:, :]
                                       + bufs[src][:, FRONT - sh:FRONT - sh + rows, :])
            src = dst
        wsum = bufs[src][:, FRONT + HIST_PAD:, :]
        inv_cnt = 1.0 / jnp.minimum(pos + 1, w).astype(jnp.float32)
        pooled = (wsum * inv_cnt - u3).reshape(m, G_A)
        pg = _bdot(pooled.astype(jnp.bfloat16), wgrp_ref[g]) * scale_ref[:, lo:lo + G_A]
        yb = (pg * _silu(z)).astype(jnp.bfloat16)
        contrib = _bdot(yb, wout_ref[lo:lo + G_A, :])
        if g == 0:
            acc_ref[...] = contrib
        else:
            acc_ref[...] += contrib

    y_ref[...] = (x + acc_ref[...]).reshape(n_seg, seg_len, D_MODEL)
    hout_ref[...] = carry_ref[...]


def _pool_layer(x, hist16, g_norm, w_in, w_grp, scale, w_out, *, n_seg, seg_len, pos0):
    s, t_len, _ = x.shape
    has_hist = hist16 is not None
    grid = (s // n_seg, t_len // seg_len)
    const2 = lambda b, t: (0, 0)
    in_specs = [pl.BlockSpec((n_seg, seg_len, D_MODEL), lambda b, t: (b, t, 0))]
    args = [x]
    if has_hist:
        in_specs.append(pl.BlockSpec((n_seg, HIST_PAD, E_A), lambda b, t: (b, 0, 0)))
        args.append(hist16)
    in_specs += [
        pl.BlockSpec((1, D_MODEL), const2),
        pl.BlockSpec((D_MODEL, 2 * E_A), const2),
        pl.BlockSpec((N_POOL_GROUPS, G_A, G_A), lambda b, t: (0, 0, 0)),
        pl.BlockSpec((1, E_A), const2),
        pl.BlockSpec((E_A, D_MODEL), const2),
    ]
    args += [g_norm, w_in, w_grp, scale, w_out]
    m = n_seg * seg_len
    stage = pltpu.VMEM((n_seg, FRONT + HIST_PAD + seg_len, G_A), jnp.float32)
    return pl.pallas_call(
        functools.partial(_pool_layer_kernel, n_seg=n_seg, seg_len=seg_len, pos0=pos0,
                          has_hist=has_hist),
        out_shape=(jax.ShapeDtypeStruct(x.shape, jnp.float32),
                   jax.ShapeDtypeStruct((s, HIST_PAD, E_A), jnp.float32)),
        grid=grid,
        in_specs=in_specs,
        out_specs=(pl.BlockSpec((n_seg, seg_len, D_MODEL), lambda b, t: (b, t, 0)),
                   pl.BlockSpec((n_seg, HIST_PAD, E_A), lambda b, t: (b, 0, 0))),
        scratch_shapes=[pltpu.VMEM((m, D_MODEL), jnp.float32), stage, stage, stage,
                        pltpu.VMEM((n_seg, HIST_PAD, E_A), jnp.float32)],
        compiler_params=pltpu.CompilerParams(
            dimension_semantics=("arbitrary", "arbitrary"),
            vmem_limit_bytes=VMEM_LIMIT_BYTES),
        name="pool_layer",
    )(*args)


def _head_rms(v, gmat_ref, gain):
    ss = _bdot((v * v).astype(jnp.bfloat16), gmat_ref[...])
    return v * lax.rsqrt(ss * (1.0 / HEAD_DIM) + EPS) * gain


def _kv_kernel(*refs, tm, has_cache):
    if has_cache:
        (x_ref, ck_ref, cv_ref, g_ref, wkv_ref, gk_ref, gmat_ref,
         kout_ref, vout_ref, kt_ref, vb_ref, ktc_ref, vbc_ref) = refs
    else:
        (x_ref, g_ref, wkv_ref, gk_ref, gmat_ref,
         kout_ref, vout_ref, kt_ref, vb_ref) = refs
    x = x_ref[0]
    hb = _rms(x, g_ref[...]).astype(jnp.bfloat16)
    k = _bdot(hb, wkv_ref[:, 0:E_B])
    v = _bdot(hb, wkv_ref[:, E_B:2 * E_B])
    kn = _head_rms(k, gmat_ref, gk_ref[...])
    kout_ref[0] = kn
    vout_ref[0] = v
    kt_w = kt_ref.shape[-1]
    for p in range(N_PAIRS):
        blk = kn[:, p * LANES:(p + 1) * LANES]
        if tm < LANES:
            blk = jnp.concatenate([blk, jnp.zeros((LANES - tm, LANES), jnp.float32)], axis=0)
        kt_ref[0, p] = blk.T[:, 0:kt_w].astype(jnp.bfloat16)
        vb_ref[0, p] = v[:, p * LANES:(p + 1) * LANES].astype(jnp.bfloat16)
        if has_cache:
            ktc_ref[0, p] = ck_ref[0, :, p * LANES:(p + 1) * LANES].T.astype(jnp.bfloat16)
            vbc_ref[0, p] = cv_ref[0, :, p * LANES:(p + 1) * LANES].astype(jnp.bfloat16)


def _kv_proj(x, cache_k, cache_v, g_kv, w_kv, gk_t, gmat, *, tm):
    s, t_len, _ = x.shape
    has_cache = cache_k is not None
    n_t = t_len // tm
    keep = min(KV_ROWS, t_len)
    assert keep == tm or n_t == 1
    kt_w = max(tm, LANES)
    const2 = lambda b, t: (0, 0)
    in_specs = [pl.BlockSpec((1, tm, D_MODEL), lambda b, t: (b, t, 0))]
    args = [x]
    if has_cache:
        lc = cache_k.shape[1]
        in_specs += [pl.BlockSpec((1, lc, E_B), lambda b, t: (b, 0, 0))] * 2
        args += [cache_k, cache_v]
    in_specs += [pl.BlockSpec((1, D_MODEL), const2),
                 pl.BlockSpec((D_MODEL, 2 * E_B), const2),
                 pl.BlockSpec((1, E_B), const2),
                 pl.BlockSpec((E_B, E_B), const2)]
    args += [g_kv, w_kv, gk_t, gmat]
    out_shape = [jax.ShapeDtypeStruct((s, keep, E_B), jnp.float32),
                 jax.ShapeDtypeStruct((s, keep, E_B), jnp.float32),
                 jax.ShapeDtypeStruct((s, N_PAIRS, LANES, n_t * kt_w), jnp.bfloat16),
                 jax.ShapeDtypeStruct((s, N_PAIRS, t_len, LANES), jnp.bfloat16)]
    out_specs = [pl.BlockSpec((1, keep, E_B), lambda b, t: (b, 0, 0)),
                 pl.BlockSpec((1, keep, E_B), lambda b, t: (b, 0, 0)),
                 pl.BlockSpec((1, N_PAIRS, LANES, kt_w), lambda b, t: (b, 0, 0, t)),
                 pl.BlockSpec((1, N_PAIRS, tm, LANES), lambda b, t: (b, 0, t, 0))]
    if has_cache:
        out_shape += [jax.ShapeDtypeStruct((s, N_PAIRS, LANES, lc), jnp.bfloat16),
                      jax.ShapeDtypeStruct((s, N_PAIRS, lc, LANES), jnp.bfloat16)]
        out_specs += [pl.BlockSpec((1, N_PAIRS, LANES, lc), lambda b, t: (b, 0, 0, 0)),
                      pl.BlockSpec((1, N_PAIRS, lc, LANES), lambda b, t: (b, 0, 0, 0))]
    return pl.pallas_call(
        functools.partial(_kv_kernel, tm=tm, has_cache=has_cache),
        out_shape=tuple(out_shape),
        grid=(s, n_t),
        in_specs=in_specs,
        out_specs=tuple(out_specs),
        compiler_params=pltpu.CompilerParams(
            dimension_semantics=("arbitrary", "arbitrary"),
            vmem_limit_bytes=VMEM_LIMIT_BYTES),
        name="kv_proj",
    )(*args)


BIAS_Q = 2 * CHUNK
BIAS_K = BIAS_Q + KV_ROWS
BIAS_BASE = 1024


def _bias_kernel(base_ref, out_ref):
    base = base_ref[0] * LOG2E
    tiled = jnp.broadcast_to(base, (BIAS_Q, BIAS_BASE))
    toep = pltpu.roll(tiled, 0, 1, stride=1, stride_axis=0)[:, 0:BIAS_K]
    qc = lax.broadcasted_iota(jnp.int32, (BIAS_Q, BIAS_K), 0) // CHUNK
    kc = lax.broadcasted_iota(jnp.int32, (BIAS_Q, BIAS_K), 1) // CHUNK
    valid = (kc >= qc) & (kc <= qc + N_LEFT_CHUNKS)
    out_ref[0] = jnp.where(valid, toep, NEG_INF)


def _bias_table(rel_bias):
    f = rel_bias.T
    rev = f[:, ::-1]
    far = f[:, N_REL - 1:N_REL]
    n_left = KV_ROWS - MAX_REL
    base = jnp.concatenate(
        [jnp.broadcast_to(far, (N_HEADS, n_left)), rev,
         jnp.broadcast_to(far, (N_HEADS, BIAS_BASE - n_left - N_REL))], axis=1)
    base = base.reshape(N_HEADS, 1, BIAS_BASE)
    return pl.pallas_call(
        _bias_kernel,
        out_shape=jax.ShapeDtypeStruct((N_HEADS, BIAS_Q, BIAS_K), jnp.float32),
        grid=(N_HEADS,),
        in_specs=[pl.BlockSpec((1, 1, BIAS_BASE), lambda h: (h, 0, 0))],
        out_specs=pl.BlockSpec((1, BIAS_Q, BIAS_K), lambda h: (h, 0, 0)),
        compiler_params=pltpu.CompilerParams(dimension_semantics=("arbitrary",)),
        name="bias_table",
    )(base)


def _attn_layer_kernel(x_ref, kta_ref, ktb_ref, va_ref, vb_ref, bias_ref,
                       g_ref, win_ref, gq_ref, gmat_ref, wout_ref,
                       y_ref, q_ref, o_ref, *, tm, qg, wb, mask_first):
    t = pl.program_id(1)
    n_grp = tm // qg
    x = x_ref[0]
    hb = _rms(x, g_ref[...]).astype(jnp.bfloat16)
    q = _bdot(hb, win_ref[:, 0:E_B])
    z = _bdot(hb, win_ref[:, E_B:2 * E_B])
    qn = _head_rms(q, gmat_ref, gq_ref[...] * (HEAD_DIM ** -0.5 * LOG2E))
    for p in range(N_PAIRS):
        q_ref[p] = qn[:, p * LANES:(p + 1) * LANES].astype(jnp.bfloat16)

    lane_lo = lax.broadcasted_iota(jnp.int32, (1, LANES), 1) < HEAD_DIM
    widths = [(KV_ROWS - j * qg, (j + 1) * qg) for j in range(n_grp)]
    neg_rows = []
    for j in range(n_grp):
        if mask_first:
            col = lax.broadcasted_iota(jnp.int32, (1, KV_ROWS + qg), 1)
            dead = jnp.logical_and(t == 0, col < widths[j][0])
            neg_rows.append(jnp.where(dead, NEG_INF, 0.0))
        else:
            neg_rows.append(None)

    def pair_body(p, carry):
        bias = bias_ref[p]
        for j in range(n_grp):
            wa, wbj = widths[j]
            qb = q_ref[p, j * qg:(j + 1) * qg, :]
            zero = jnp.zeros_like(qb)
            lhs = jnp.concatenate([jnp.where(lane_lo, qb, zero),
                                   jnp.where(lane_lo, zero, qb)], axis=0)
            s = jnp.concatenate(
                [_bdot(lhs, kta_ref[0, p, :, KV_ROWS - wa:KV_ROWS]),
                 _bdot(lhs, ktb_ref[0, p, :, 0:wbj])], axis=1)
            s = s + bias
            if neg_rows[j] is not None:
                s = s + neg_rows[j]
            mx = jnp.max(s, axis=1, keepdims=True)
            e = jnp.exp2(s - mx)
            l = jnp.sum(e, axis=1, keepdims=True)
            pb = e.astype(jnp.bfloat16)
            o2 = (_bdot(pb[:, 0:wa], va_ref[0, p, KV_ROWS - wa:KV_ROWS, :])
                  + _bdot(pb[:, wa:wa + wbj], vb_ref[0, p, 0:wbj, :]))
            o2 = o2 * (1.0 / l)
            o_ref[p, j * qg:(j + 1) * qg, :] = jnp.where(lane_lo, o2[0:qg], o2[qg:2 * qg])
        return carry

    lax.fori_loop(0, N_PAIRS, pair_body, 0)
    o = jnp.concatenate([o_ref[p] for p in range(N_PAIRS)], axis=1)
    yb = (o * _silu(z)).astype(jnp.bfloat16)
    y_ref[0] = x + _bdot(yb, wout_ref[...])


def _attn_layer(x, kt_a, kt_b, v_a, v_b, bias, g_norm, w_in, gq_t, gmat, w_out,
                *, tm, qg, a_is_previous_tile):
    s, t_len, _ = x.shape
    n_t = t_len // tm
    wb = kt_b.shape[-1] // n_t
    const2 = lambda b, t: (0, 0)
    if a_is_previous_tile:
        assert tm == KV_ROWS
        prev = lambda b, t: jnp.maximum(t - 1, 0)
        kta_spec = pl.BlockSpec((1, N_PAIRS, LANES, KV_ROWS), lambda b, t: (b, 0, 0, prev(b, t)))
        va_spec = pl.BlockSpec((1, N_PAIRS, KV_ROWS, LANES), lambda b, t: (b, 0, prev(b, t), 0))
    else:
        kta_spec = pl.BlockSpec((1, N_PAIRS, LANES, KV_ROWS), lambda b, t: (b, 0, 0, 0))
        va_spec = pl.BlockSpec((1, N_PAIRS, KV_ROWS, LANES), lambda b, t: (b, 0, 0, 0))
    in_specs = [
        pl.BlockSpec((1, tm, D_MODEL), lambda b, t: (b, t, 0)),
        kta_spec,
        pl.BlockSpec((1, N_PAIRS, LANES, wb), lambda b, t: (b, 0, 0, t)),
        va_spec,
        pl.BlockSpec((1, N_PAIRS, tm, LANES), lambda b, t: (b, 0, t, 0)),
        pl.BlockSpec(bias.shape, lambda b, t: (0, 0, 0)),
        pl.BlockSpec((1, D_MODEL), const2),
        pl.BlockSpec((D_MODEL, 2 * E_B), const2),
        pl.BlockSpec((1, E_B), const2),
        pl.BlockSpec((E_B, E_B), const2),
        pl.BlockSpec((E_B, D_MODEL), const2),
    ]
    return pl.pallas_call(
        functools.partial(_attn_layer_kernel, tm=tm, qg=qg, wb=wb,
                          mask_first=a_is_previous_tile),
        out_shape=jax.ShapeDtypeStruct(x.shape, jnp.float32),
        grid=(s, n_t),
        in_specs=in_specs,
        out_specs=pl.BlockSpec((1, tm, D_MODEL), lambda b, t: (b, t, 0)),
        scratch_shapes=[pltpu.VMEM((N_PAIRS, tm, LANES), jnp.bfloat16),
                        pltpu.VMEM((N_PAIRS, tm, LANES), jnp.float32)],
        compiler_params=pltpu.CompilerParams(
            dimension_semantics=("arbitrary", "arbitrary"),
            vmem_limit_bytes=VMEM_LIMIT_BYTES),
        name="attn_layer",
    )(x, kt_a, kt_b, v_a, v_b, bias, g_norm, w_in, gq_t, gmat, w_out)


def _trunk(x, hist16, cache_k, cache_v, pos0, w, *, pool_seg, pool_len, tile):
    s, t_len, _ = x.shape
    hists = []
    for layer in range(2):
        h_in = None if hist16 is None else hist16[layer]
        x, h_out = _pool_layer(x, h_in, w["norm_a"][layer], w["w_in_a"][layer], w["w_grp_a"][layer],
                               w["scale_a"][layer], w["w_out_a"][layer],
                               n_seg=pool_seg, seg_len=pool_len, pos0=pos0)
        hists.append(h_out[:, HIST_PAD - POOL_HIST:, :])
    if cache_k is None:
        k_new, v_new, kt, vb = _kv_proj(x, None, None, w["norm_kv"], w["w_kv"], w["gk_t"],
                                        w["gmat"], tm=tile)
        srcs = (kt, kt, vb, vb)
        bias = w["bias_prompt"]
        qg = BIAS_Q
    else:
        lc = cache_k.shape[1]
        k_new, v_new, kt, vb, ktc, vbc = _kv_proj(
            x, cache_k.reshape(s, lc, E_B), cache_v.reshape(s, lc, E_B),
            w["norm_kv"], w["w_kv"], w["gk_t"], w["gmat"], tm=tile)
        srcs = (ktc, kt, vbc, vb)
        bias = w["bias_sample"]
        qg = CHUNK
    for j in range(2):
        x = _attn_layer(x, *srcs, bias[j], w["norm_b"][j], w["w_in_b"][j], w["gq_t"][j],
                        w["gmat"], w["w_out_b"][j], tm=tile, qg=qg,
                        a_is_previous_tile=cache_k is None)
    return x, jnp.stack(hists, axis=0), k_new, v_new


def kernel(x_prompt, x_sample, state_pool, cache_k, cache_v, norm_a, w_in_a, w_grp_a, scale_a,
           w_out_a, norm_kv, w_kv, g_k, norm_b, w_in_b, g_q, rel_bias_b, w_out_b):
    bf = jnp.bfloat16
    head_of_lane = jnp.arange(E_B, dtype=jnp.int32) // HEAD_DIM
    gmat = (head_of_lane[:, None] == head_of_lane[None, :]).astype(bf)
    tables = [_bias_table(rel_bias_b[j]) for j in range(rel_bias_b.shape[0])]
    w = dict(
        norm_a=norm_a[:, None, :], w_in_a=w_in_a.astype(bf), w_grp_a=w_grp_a.astype(bf),
        scale_a=scale_a[:, None, :], w_out_a=w_out_a.astype(bf),
        norm_kv=norm_kv[None, :], w_kv=w_kv.astype(bf), gk_t=jnp.tile(g_k, N_HEADS)[None, :],
        norm_b=norm_b[:, None, :], w_in_b=w_in_b.astype(bf),
        gq_t=jnp.tile(g_q, (1, N_HEADS))[:, None, :], w_out_b=w_out_b.astype(bf), gmat=gmat,
        bias_prompt=[tb.reshape(N_PAIRS, HEADS_PER_LANE_BLOCK * BIAS_Q, BIAS_K) for tb in tables],
        bias_sample=[tb[:, 0:CHUNK, 0:CHUNK + KV_ROWS].reshape(
            N_PAIRS, HEADS_PER_LANE_BLOCK * CHUNK, CHUNK + KV_ROWS) for tb in tables],
    )
    bp, sp, _ = x_prompt.shape
    bs, ss, _ = x_sample.shape
    y_p, pool_p, k_p, v_p = _trunk(x_prompt, None, None, None, 0, w,
                                   pool_seg=1, pool_len=KV_ROWS, tile=KV_ROWS)
    hist16 = jnp.pad(state_pool, ((0, 0), (0, 0), (HIST_PAD - POOL_HIST, 0), (0, 0)))
    y_s, pool_s, k_s, v_s = _trunk(x_sample, hist16, cache_k, cache_v, PAST_LEN, w,
                                   pool_seg=bs, pool_len=ss, tile=ss)
    lp = min(KV_ROWS, sp)
    return (y_p, y_s, pool_p, pool_s,
            k_p.reshape(bp, lp, N_HEADS, HEAD_DIM), v_p.reshape(bp, lp, N_HEADS, HEAD_DIM),
            k_s.reshape(bs, ss, N_HEADS, HEAD_DIM), v_s.reshape(bs, ss, N_HEADS, HEAD_DIM))
```

```python
import functools
import math

import jax
import jax.numpy as jnp
from jax import lax
from jax.experimental import pallas as pl
from jax.experimental.pallas import tpu as pltpu

D_MODEL = 1024
E_A = 2048
POOL_WINDOWS = (2, 4, 8, 16)
N_POOL_GROUPS = len(POOL_WINDOWS)
G_A = E_A // N_POOL_GROUPS
POOL_HIST = max(POOL_WINDOWS) - 1
HEAD_DIM = 64
N_HEADS = 16
E_B = N_HEADS * HEAD_DIM
CHUNK = 64
N_LEFT_CHUNKS = 8
KV_ROWS = N_LEFT_CHUNKS * CHUNK
MAX_REL = 128
N_REL = 2 * MAX_REL + 1
EPS = 1e-6
NEG_INF = -1e30
PAST_LEN = 4096
LOG2E = math.log2(math.e)

SUBLANES = 8
LANES = 128
HEADS_PER_LANE_BLOCK = LANES // HEAD_DIM
N_PAIRS = N_HEADS // HEADS_PER_LANE_BLOCK
HIST_PAD = 2 * SUBLANES
FRONT = SUBLANES
VMEM_LIMIT_BYTES = 56 * 1024 * 1024


def _bdot(a, b):
    return jnp.dot(a, b, preferred_element_type=jnp.float32)


def _rms(x, g):
    ms = jnp.mean(x * x, axis=-1, keepdims=True)
    return x * lax.rsqrt(ms + EPS) * g


def _silu(z):
    return z * (1.0 / (1.0 + jnp.exp(-z)))


def _pool_layer_kernel(*refs, n_seg, seg_len, pos0, has_hist):
    if has_hist:
        (x_ref, hist_ref, g_ref, win_ref, wgrp_ref, scale_ref, wout_ref,
         y_ref, hout_ref, acc_ref, upad_ref, s0_ref, s1_ref, carry_ref) = refs
    else:
        (x_ref, g_ref, win_ref, wgrp_ref, scale_ref, wout_ref,
         y_ref, hout_ref, acc_ref, upad_ref, s0_ref, s1_ref, carry_ref) = refs
        hist_ref = None
    t = pl.program_id(1)
    m = n_seg * seg_len
    rows = HIST_PAD + seg_len
    x = x_ref[...].reshape(m, D_MODEL)
    hb = _rms(x, g_ref[...]).astype(jnp.bfloat16)

    @pl.when(t == 0)
    def _():
        if has_hist:
            carry_ref[...] = hist_ref[...]
        else:
            carry_ref[...] = jnp.zeros_like(carry_ref)

    zero_front = jnp.zeros((n_seg, FRONT, G_A), jnp.float32)
    upad_ref[:, 0:FRONT, :] = zero_front
    s0_ref[:, 0:FRONT, :] = zero_front
    s1_ref[:, 0:FRONT, :] = zero_front

    pos = pos0 + t * seg_len + lax.broadcasted_iota(jnp.int32, (1, seg_len, 1), 1)

    for g, w in enumerate(POOL_WINDOWS):
        lo = g * G_A
        u = _bdot(hb, win_ref[:, lo:lo + G_A])
        z = _bdot(hb, win_ref[:, E_A + lo:E_A + lo + G_A])
        u3 = u.reshape(n_seg, seg_len, G_A)
        upad_ref[:, FRONT:FRONT + HIST_PAD, :] = carry_ref[:, :, lo:lo + G_A]
        upad_ref[:, FRONT + HIST_PAD:, :] = u3
        carry_ref[:, :, lo:lo + G_A] = upad_ref[:, FRONT + seg_len:, :]
        bufs = (upad_ref, s0_ref, s1_ref)
        src = 0
        for k in range(g + 1):
            dst = 1 if src != 1 else 2
            sh = 1 << k
            bufs[dst][:, FRONT:, :] = (bufs[src][:, FRONT:, :]
                                       + bufs[src][:, FRONT - sh:FRONT - sh + rows, :])
            src = dst
        wsum = bufs[src][:, FRONT + HIST_PAD:, :]
        inv_cnt = 1.0 / jnp.minimum(pos + 1, w).astype(jnp.float32)
        pooled = (wsum * inv_cnt - u3).reshape(m, G_A)
        pg = _bdot(pooled.astype(jnp.bfloat16), wgrp_ref[g]) * scale_ref[:, lo:lo + G_A]
        yb = (pg * _silu(z)).astype(jnp.bfloat16)
        contrib = _bdot(yb, wout_ref[lo:lo + G_A, :])
        if g == 0:
            acc_ref[...] = contrib
        else:
            acc_ref[...] += contrib

    y_ref[...] = (x + acc_ref[...]).reshape(n_seg, seg_len, D_MODEL)
    hout_ref[...] = carry_ref[...]


def _pool_layer(x, hist16, g_norm, w_in, w_grp, scale, w_out, *, n_seg, seg_len, pos0):
    s, t_len, _ = x.shape
    has_hist = hist16 is not None
    grid = (s // n_seg, t_len // seg_len)
    const2 = lambda b, t: (0, 0)
    in_specs = [pl.BlockSpec((n_seg, seg_len, D_MODEL), lambda b, t: (b, t, 0))]
    args = [x]
    if has_hist:
        in_specs.append(pl.BlockSpec((n_seg, HIST_PAD, E_A), lambda b, t: (b, 0, 0)))
        args.append(hist16)
    in_specs += [
        pl.BlockSpec((1, D_MODEL), const2),
        pl.BlockSpec((D_MODEL, 2 * E_A), const2),
        pl.BlockSpec((N_POOL_GROUPS, G_A, G_A), lambda b, t: (0, 0, 0)),
        pl.BlockSpec((1, E_A), const2),
        pl.BlockSpec((E_A, D_MODEL), const2),
    ]
    args += [g_norm, w_in, w_grp, scale, w_out]
    m = n_seg * seg_len
    stage = pltpu.VMEM((n_seg, FRONT + HIST_PAD + seg_len, G_A), jnp.float32)
    return pl.pallas_call(
        functools.partial(_pool_layer_kernel, n_seg=n_seg, seg_len=seg_len, pos0=pos0,
                          has_hist=has_hist),
        out_shape=(jax.ShapeDtypeStruct(x.shape, jnp.float32),
                   jax.ShapeDtypeStruct((s, HIST_PAD, E_A), jnp.float32)),
        grid=grid,
        in_specs=in_specs,
        out_specs=(pl.BlockSpec((n_seg, seg_len, D_MODEL), lambda b, t: (b, t, 0)),
                   pl.BlockSpec((n_seg, HIST_PAD, E_A), lambda b, t: (b, 0, 0))),
        scratch_shapes=[pltpu.VMEM((m, D_MODEL), jnp.float32), stage, stage, stage,
                        pltpu.VMEM((n_seg, HIST_PAD, E_A), jnp.float32)],
        compiler_params=pltpu.CompilerParams(
            dimension_semantics=("arbitrary", "arbitrary"),
            vmem_limit_bytes=VMEM_LIMIT_BYTES),
        name="pool_layer",
    )(*args)


def _head_rms(v, gmat_ref, gain):
    ss = _bdot((v * v).astype(jnp.bfloat16), gmat_ref[...])
    return v * lax.rsqrt(ss * (1.0 / HEAD_DIM) + EPS) * gain


def _kv_kernel(*refs, tm, has_cache):
    if has_cache:
        (x_ref, ck_ref, cv_ref, g_ref, wkv_ref, gk_ref, gmat_ref,
         kout_ref, vout_ref, kt_ref, vb_ref, ktc_ref, vbc_ref) = refs
    else:
        (x_ref, g_ref, wkv_ref, gk_ref, gmat_ref,
         kout_ref, vout_ref, kt_ref, vb_ref) = refs
    x = x_ref[0]
    hb = _rms(x, g_ref[...]).astype(jnp.bfloat16)
    k = _bdot(hb, wkv_ref[:, 0:E_B])
    v = _bdot(hb, wkv_ref[:, E_B:2 * E_B])
    kn = _head_rms(k, gmat_ref, gk_ref[...])
    kout_ref[0] = kn
    vout_ref[0] = v
    kt_w = kt_ref.shape[-1]
    for p in range(N_PAIRS):
        blk = kn[:, p * LANES:(p + 1) * LANES]
        if tm < LANES:
            blk = jnp.concatenate([blk, jnp.zeros((LANES - tm, LANES), jnp.float32)], axis=0)
        kt_ref[0, p] = blk.T[:, 0:kt_w].astype(jnp.bfloat16)
        vb_ref[0, p] = v[:, p * LANES:(p + 1) * LANES].astype(jnp.bfloat16)
        if has_cache:
            ktc_ref[0, p] = ck_ref[0, :, p * LANES:(p + 1) * LANES].T.astype(jnp.bfloat16)
            vbc_ref[0, p] = cv_ref[0, :, p * LANES:(p + 1) * LANES].astype(jnp.bfloat16)


def _kv_proj(x, cache_k, cache_v, g_kv, w_kv, gk_t, gmat, *, tm):
    s, t_len, _ = x.shape
    has_cache = cache_k is not None
    n_t = t_len // tm
    keep = min(KV_ROWS, t_len)
    assert keep == tm or n_t == 1
    kt_w = max(tm, LANES)
    const2 = lambda b, t: (0, 0)
    in_specs = [pl.BlockSpec((1, tm, D_MODEL), lambda b, t: (b, t, 0))]
    args = [x]
    if has_cache:
        lc = cache_k.shape[1]
        in_specs += [pl.BlockSpec((1, lc, E_B), lambda b, t: (b, 0, 0))] * 2
        args += [cache_k, cache_v]
    in_specs += [pl.BlockSpec((1, D_MODEL), const2),
                 pl.BlockSpec((D_MODEL, 2 * E_B), const2),
                 pl.BlockSpec((1, E_B), const2),
                 pl.BlockSpec((E_B, E_B), const2)]
    args += [g_kv, w_kv, gk_t, gmat]
    out_shape = [jax.ShapeDtypeStruct((s, keep, E_B), jnp.float32),
                 jax.ShapeDtypeStruct((s, keep, E_B), jnp.float32),
                 jax.ShapeDtypeStruct((s, N_PAIRS, LANES, n_t * kt_w), jnp.bfloat16),
                 jax.ShapeDtypeStruct((s, N_PAIRS, t_len, LANES), jnp.bfloat16)]
    out_specs = [pl.BlockSpec((1, keep, E_B), lambda b, t: (b, 0, 0)),
                 pl.BlockSpec((1, keep, E_B), lambda b, t: (b, 0, 0)),
                 pl.BlockSpec((1, N_PAIRS, LANES, kt_w), lambda b, t: (b, 0, 0, t)),
                 pl.BlockSpec((1, N_PAIRS, tm, LANES), lambda b, t: (b, 0, t, 0))]
    if has_cache:
        out_shape += [jax.ShapeDtypeStruct((s, N_PAIRS, LANES, lc), jnp.bfloat16),
                      jax.ShapeDtypeStruct((s, N_PAIRS, lc, LANES), jnp.bfloat16)]
        out_specs += [pl.BlockSpec((1, N_PAIRS, LANES, lc), lambda b, t: (b, 0, 0, 0)),
                      pl.BlockSpec((1, N_PAIRS, lc, LANES), lambda b, t: (b, 0, 0, 0))]
    return pl.pallas_call(
        functools.partial(_kv_kernel, tm=tm, has_cache=has_cache),
        out_shape=tuple(out_shape),
        grid=(s, n_t),
        in_specs=in_specs,
        out_specs=tuple(out_specs),
        compiler_params=pltpu.CompilerParams(
            dimension_semantics=("arbitrary", "arbitrary"),
            vmem_limit_bytes=VMEM_LIMIT_BYTES),
        name="kv_proj",
    )(*args)


BIAS_Q = 2 * CHUNK
BIAS_K = BIAS_Q + KV_ROWS
BIAS_BASE = 1024


def _bias_kernel(base_ref, out_ref):
    base = base_ref[0] * LOG2E
    tiled = jnp.broadcast_to(base, (BIAS_Q, BIAS_BASE))
    toep = pltpu.roll(tiled, 0, 1, stride=1, stride_axis=0)[:, 0:BIAS_K]
    qc = lax.broadcasted_iota(jnp.int32, (BIAS_Q, BIAS_K), 0) // CHUNK
    kc = lax.broadcasted_iota(jnp.int32, (BIAS_Q, BIAS_K), 1) // CHUNK
    valid = (kc >= qc) & (kc <= qc + N_LEFT_CHUNKS)
    out_ref[0] = jnp.where(valid, toep, NEG_INF)


def _bias_table(rel_bias):
    f = rel_bias.T
    rev = f[:, ::-1]
    far = f[:, N_REL - 1:N_REL]
    n_left = KV_ROWS - MAX_REL
    base = jnp.concatenate(
        [jnp.broadcast_to(far, (N_HEADS, n_left)), rev,
         jnp.broadcast_to(far, (N_HEADS, BIAS_BASE - n_left - N_REL))], axis=1)
    base = base.reshape(N_HEADS, 1, BIAS_BASE)
    return pl.pallas_call(
        _bias_kernel,
        out_shape=jax.ShapeDtypeStruct((N_HEADS, BIAS_Q, BIAS_K), jnp.float32),
        grid=(N_HEADS,),
        in_specs=[pl.BlockSpec((1, 1, BIAS_BASE), lambda h: (h, 0, 0))],
        out_specs=pl.BlockSpec((1, BIAS_Q, BIAS_K), lambda h: (h, 0, 0)),
        compiler_params=pltpu.CompilerParams(dimension_semantics=("arbitrary",)),
        name="bias_table",
    )(base)


def _attn_layer_kernel(x_ref, kta_ref, ktb_ref, va_ref, vb_ref, bias_ref,
                       g_ref, win_ref, gq_ref, gmat_ref, wout_ref,
                       y_ref, q_ref, o_ref, *, tm, qg, wb, mask_first):
    t = pl.program_id(1)
    n_grp = tm // qg
    x = x_ref[0]
    hb = _rms(x, g_ref[...]).astype(jnp.bfloat16)
    q = _bdot(hb, win_ref[:, 0:E_B])
    z = _bdot(hb, win_ref[:, E_B:2 * E_B])
    qn = _head_rms(q, gmat_ref, gq_ref[...] * (HEAD_DIM ** -0.5 * LOG2E))
    for p in range(N_PAIRS):
        q_ref[p] = qn[:, p * LANES:(p + 1) * LANES].astype(jnp.bfloat16)

    lane_lo = lax.broadcasted_iota(jnp.int32, (1, LANES), 1) < HEAD_DIM
    widths = [(KV_ROWS - j * qg, (j + 1) * qg) for j in range(n_grp)]
    neg_rows = []
    for j in range(n_grp):
        if mask_first:
            col = lax.broadcasted_iota(jnp.int32, (1, KV_ROWS + qg), 1)
            dead = jnp.logical_and(t == 0, col < widths[j][0])
            neg_rows.append(jnp.where(dead, NEG_INF, 0.0))
        else:
            neg_rows.append(None)

    def scores(p, j):
        wa, wbj = widths[j]
        qb = q_ref[p, j * qg:(j + 1) * qg, :]
        zero = jnp.zeros_like(qb)
        lhs = jnp.concatenate([jnp.where(lane_lo, qb, zero),
                               jnp.where(lane_lo, zero, qb)], axis=0)
        s = jnp.concatenate(
            [_bdot(lhs, kta_ref[0, p, :, KV_ROWS - wa:KV_ROWS]),
             _bdot(lhs, ktb_ref[0, p, :, 0:wbj])], axis=1)
        s = s + bias_ref[p]
        if neg_rows[j] is not None:
            s = s + neg_rows[j]
        return s

    def attend(p, j, s):
        wa, wbj = widths[j]
        mx = jnp.max(s, axis=1, keepdims=True)
        e = jnp.exp2(s - mx)
        l = jnp.sum(e, axis=1, keepdims=True)
        pb = e.astype(jnp.bfloat16)
        o2 = (_bdot(pb[:, 0:wa], va_ref[0, p, KV_ROWS - wa:KV_ROWS, :])
              + _bdot(pb[:, wa:wa + wbj], vb_ref[0, p, 0:wbj, :]))
        o2 = o2 * (1.0 / l)
        o_ref[p, j * qg:(j + 1) * qg, :] = jnp.where(lane_lo, o2[0:qg], o2[qg:2 * qg])

    items = [(p, j) for p in range(N_PAIRS) for j in range(n_grp)]
    s = scores(*items[0])
    for i, (p, j) in enumerate(items):
        s_next = scores(*items[i + 1]) if i + 1 < len(items) else None
        attend(p, j, s)
        s = s_next
    o = jnp.concatenate([o_ref[p] for p in range(N_PAIRS)], axis=1)
    yb = (o * _silu(z)).astype(jnp.bfloat16)
    y_ref[0] = x + _bdot(yb, wout_ref[...])


def _attn_layer(x, kt_a, kt_b, v_a, v_b, bias, g_norm, w_in, gq_t, gmat, w_out,
                *, tm, qg, a_is_previous_tile):
    s, t_len, _ = x.shape
    n_t = t_len // tm
    wb = kt_b.shape[-1] // n_t
    const2 = lambda b, t: (0, 0)
    if a_is_previous_tile:
        assert tm == KV_ROWS
        prev = lambda b, t: jnp.maximum(t - 1, 0)
        kta_spec = pl.BlockSpec((1, N_PAIRS, LANES, KV_ROWS), lambda b, t: (b, 0, 0, prev(b, t)))
        va_spec = pl.BlockSpec((1, N_PAIRS, KV_ROWS, LANES), lambda b, t: (b, 0, prev(b, t), 0))
    else:
        kta_spec = pl.BlockSpec((1, N_PAIRS, LANES, KV_ROWS), lambda b, t: (b, 0, 0, 0))
        va_spec = pl.BlockSpec((1, N_PAIRS, KV_ROWS, LANES), lambda b, t: (b, 0, 0, 0))
    in_specs = [
        pl.BlockSpec((1, tm, D_MODEL), lambda b, t: (b, t, 0)),
        kta_spec,
        pl.BlockSpec((1, N_PAIRS, LANES, wb), lambda b, t: (b, 0, 0, t)),
        va_spec,
        pl.BlockSpec((1, N_PAIRS, tm, LANES), lambda b, t: (b, 0, t, 0)),
        pl.BlockSpec(bias.shape, lambda b, t: (0, 0, 0)),
        pl.BlockSpec((1, D_MODEL), const2),
        pl.BlockSpec((D_MODEL, 2 * E_B), const2),
        pl.BlockSpec((1, E_B), const2),
        pl.BlockSpec((E_B, E_B), const2),
        pl.BlockSpec((E_B, D_MODEL), const2),
    ]
    return pl.pallas_call(
        functools.partial(_attn_layer_kernel, tm=tm, qg=qg, wb=wb,
                          mask_first=a_is_previous_tile),
        out_shape=jax.ShapeDtypeStruct(x.shape, jnp.float32),
        grid=(s, n_t),
        in_specs=in_specs,
        out_specs=pl.BlockSpec((1, tm, D_MODEL), lambda b, t: (b, t, 0)),
        scratch_shapes=[pltpu.VMEM((N_PAIRS, tm, LANES), jnp.bfloat16),
                        pltpu.VMEM((N_PAIRS, tm, LANES), jnp.float32)],
        compiler_params=pltpu.CompilerParams(
            dimension_semantics=("arbitrary", "arbitrary"),
            vmem_limit_bytes=VMEM_LIMIT_BYTES),
        name="attn_layer",
    )(x, kt_a, kt_b, v_a, v_b, bias, g_norm, w_in, gq_t, gmat, w_out)


def _trunk(x, hist16, cache_k, cache_v, pos0, w, *, pool_seg, pool_len, tile):
    s, t_len, _ = x.shape
    hists = []
    for layer in range(2):
        h_in = None if hist16 is None else hist16[layer]
        x, h_out = _pool_layer(x, h_in, w["norm_a"][layer], w["w_in_a"][layer], w["w_grp_a"][layer],
                               w["scale_a"][layer], w["w_out_a"][layer],
                               n_seg=pool_seg, seg_len=pool_len, pos0=pos0)
        hists.append(h_out[:, HIST_PAD - POOL_HIST:, :])
    if cache_k is None:
        k_new, v_new, kt, vb = _kv_proj(x, None, None, w["norm_kv"], w["w_kv"], w["gk_t"],
                                        w["gmat"], tm=tile)
        srcs = (kt, kt, vb, vb)
        bias = w["bias_prompt"]
        qg = BIAS_Q
    else:
        lc = cache_k.shape[1]
        k_new, v_new, kt, vb, ktc, vbc = _kv_proj(
            x, cache_k.reshape(s, lc, E_B), cache_v.reshape(s, lc, E_B),
            w["norm_kv"], w["w_kv"], w["gk_t"], w["gmat"], tm=tile)
        srcs = (ktc, kt, vbc, vb)
        bias = w["bias_sample"]
        qg = CHUNK
    for j in range(2):
        x = _attn_layer(x, *srcs, bias[j], w["norm_b"][j], w["w_in_b"][j], w["gq_t"][j],
                        w["gmat"], w["w_out_b"][j], tm=tile, qg=qg,
                        a_is_previous_tile=cache_k is None)
    return x, jnp.stack(hists, axis=0), k_new, v_new


def kernel(x_prompt, x_sample, state_pool, cache_k, cache_v, norm_a, w_in_a, w_grp_a, scale_a,
           w_out_a, norm_kv, w_kv, g_k, norm_b, w_in_b, g_q, rel_bias_b, w_out_b):
    bf = jnp.bfloat16
    head_of_lane = jnp.arange(E_B, dtype=jnp.int32) // HEAD_DIM
    gmat = (head_of_lane[:, None] == head_of_lane[None, :]).astype(bf)
    tables = [_bias_table(rel_bias_b[j]) for j in range(rel_bias_b.shape[0])]
    w = dict(
        norm_a=norm_a[:, None, :], w_in_a=w_in_a.astype(bf), w_grp_a=w_grp_a.astype(bf),
        scale_a=scale_a[:, None, :], w_out_a=w_out_a.astype(bf),
        norm_kv=norm_kv[None, :], w_kv=w_kv.astype(bf), gk_t=jnp.tile(g_k, N_HEADS)[None, :],
        norm_b=norm_b[:, None, :], w_in_b=w_in_b.astype(bf),
        gq_t=jnp.tile(g_q, (1, N_HEADS))[:, None, :], w_out_b=w_out_b.astype(bf), gmat=gmat,
        bias_prompt=[tb.reshape(N_PAIRS, HEADS_PER_LANE_BLOCK * BIAS_Q, BIAS_K) for tb in tables],
        bias_sample=[tb[:, 0:CHUNK, 0:CHUNK + KV_ROWS].reshape(
            N_PAIRS, HEADS_PER_LANE_BLOCK * CHUNK, CHUNK + KV_ROWS) for tb in tables],
    )
    bp, sp, _ = x_prompt.shape
    bs, ss, _ = x_sample.shape
    y_p, pool_p, k_p, v_p = _trunk(x_prompt, None, None, None, 0, w,
                                   pool_seg=1, pool_len=KV_ROWS, tile=KV_ROWS)
    hist16 = jnp.pad(state_pool, ((0, 0), (0, 0), (HIST_PAD - POOL_HIST, 0), (0, 0)))
    y_s, pool_s, k_s, v_s = _trunk(x_sample, hist16, cache_k, cache_v, PAST_LEN, w,
                                   pool_seg=bs, pool_len=ss, tile=ss)
    lp = min(KV_ROWS, sp)
    return (y_p, y_s, pool_p, pool_s,
            k_p.reshape(bp, lp, N_HEADS, HEAD_DIM), v_p.reshape(bp, lp, N_HEADS, HEAD_DIM),
            k_s.reshape(bs, ss, N_HEADS, HEAD_DIM), v_s.reshape(bs, ss, N_HEADS, HEAD_DIM))
```

```python
import functools
import math

import jax
import jax.numpy as jnp
from jax import lax
from jax.experimental import pallas as pl
from jax.experimental.pallas import tpu as pltpu

D_MODEL = 1024
E_A = 2048
POOL_WINDOWS = (2, 4, 8, 16)
N_POOL_GROUPS = len(POOL_WINDOWS)
G_A = E_A // N_POOL_GROUPS
POOL_HIST = max(POOL_WINDOWS) - 1
HEAD_DIM = 64
N_HEADS = 16
E_B = N_HEADS * HEAD_DIM
CHUNK = 64
N_LEFT_CHUNKS = 8
KV_ROWS = N_LEFT_CHUNKS * CHUNK
MAX_REL = 128
N_REL = 2 * MAX_REL + 1
EPS = 1e-6
NEG_INF = -1e30
PAST_LEN = 4096
LOG2E = math.log2(math.e)

SUBLANES = 8
LANES = 128
MXU_DIM = 256
HEADS_PER_LANE_BLOCK = LANES // HEAD_DIM
N_PAIRS = N_HEADS // HEADS_PER_LANE_BLOCK
HIST_PAD = 2 * SUBLANES
FRONT = SUBLANES
QK_AHEAD = 2
PIECE_COLS = 2 * MXU_DIM
VMEM_LIMIT_BYTES = 56 * 1024 * 1024


def _bdot(a, b):
    return jnp.dot(a, b, preferred_element_type=jnp.float32)


def _rms(x, g):
    ms = jnp.mean(x * x, axis=-1, keepdims=True)
    return x * lax.rsqrt(ms + EPS) * g


def _silu(z):
    return z * (1.0 / (1.0 + jnp.exp(-z)))


def _pool_layer_kernel(*refs, n_seg, seg_len, pos0, has_hist):
    if has_hist:
        (x_ref, hist_ref, g_ref, win_ref, wgrp_ref, scale_ref, wout_ref,
         y_ref, hout_ref, acc_ref, upad_ref, s0_ref, s1_ref, carry_ref) = refs
    else:
        (x_ref, g_ref, win_ref, wgrp_ref, scale_ref, wout_ref,
         y_ref, hout_ref, acc_ref, upad_ref, s0_ref, s1_ref, carry_ref) = refs
        hist_ref = None
    t = pl.program_id(1)
    m = n_seg * seg_len
    rows = HIST_PAD + seg_len
    x = x_ref[...].reshape(m, D_MODEL)
    hb = _rms(x, g_ref[...]).astype(jnp.bfloat16)

    @pl.when(t == 0)
    def _():
        if has_hist:
            carry_ref[...] = hist_ref[...]
        else:
            carry_ref[...] = jnp.zeros_like(carry_ref)

    zero_front = jnp.zeros((n_seg, FRONT, G_A), jnp.float32)
    upad_ref[:, 0:FRONT, :] = zero_front
    s0_ref[:, 0:FRONT, :] = zero_front
    s1_ref[:, 0:FRONT, :] = zero_front

    pos = pos0 + t * seg_len + lax.broadcasted_iota(jnp.int32, (1, seg_len, 1), 1)

    for g, w in enumerate(POOL_WINDOWS):
        lo = g * G_A
        u = _bdot(hb, win_ref[:, lo:lo + G_A])
        z = _bdot(hb, win_ref[:, E_A + lo:E_A + lo + G_A])
        u3 = u.reshape(n_seg, seg_len, G_A)
        upad_ref[:, FRONT:FRONT + HIST_PAD, :] = carry_ref[:, :, lo:lo + G_A]
        upad_ref[:, FRONT + HIST_PAD:, :] = u3
        carry_ref[:, :, lo:lo + G_A] = upad_ref[:, FRONT + seg_len:, :]
        bufs = (upad_ref, s0_ref, s1_ref)
        src = 0
        for k in range(g + 1):
            dst = 1 if src != 1 else 2
            sh = 1 << k
            bufs[dst][:, FRONT:, :] = (bufs[src][:, FRONT:, :]
                                       + bufs[src][:, FRONT - sh:FRONT - sh + rows, :])
            src = dst
        wsum = bufs[src][:, FRONT + HIST_PAD:, :]
        inv_cnt = 1.0 / jnp.minimum(pos + 1, w).astype(jnp.float32)
        pooled = (wsum * inv_cnt - u3).reshape(m, G_A)
        pg = _bdot(pooled.astype(jnp.bfloat16), wgrp_ref[g]) * scale_ref[:, lo:lo + G_A]
        yb = (pg * _silu(z)).astype(jnp.bfloat16)
        contrib = _bdot(yb, wout_ref[lo:lo + G_A, :])
        if g == 0:
            acc_ref[...] = contrib
        else:
            acc_ref[...] += contrib

    y_ref[...] = (x + acc_ref[...]).reshape(n_seg, seg_len, D_MODEL)
    hout_ref[...] = carry_ref[...]


def _pool_layer(x, hist16, g_norm, w_in, w_grp, scale, w_out, *, n_seg, seg_len, pos0):
    s, t_len, _ = x.shape
    has_hist = hist16 is not None
    grid = (s // n_seg, t_len // seg_len)
    const2 = lambda b, t: (0, 0)
    in_specs = [pl.BlockSpec((n_seg, seg_len, D_MODEL), lambda b, t: (b, t, 0))]
    args = [x]
    if has_hist:
        in_specs.append(pl.BlockSpec((n_seg, HIST_PAD, E_A), lambda b, t: (b, 0, 0)))
        args.append(hist16)
    in_specs += [
        pl.BlockSpec((1, D_MODEL), const2),
        pl.BlockSpec((D_MODEL, 2 * E_A), const2),
        pl.BlockSpec((N_POOL_GROUPS, G_A, G_A), lambda b, t: (0, 0, 0)),
        pl.BlockSpec((1, E_A), const2),
        pl.BlockSpec((E_A, D_MODEL), const2),
    ]
    args += [g_norm, w_in, w_grp, scale, w_out]
    m = n_seg * seg_len
    stage = pltpu.VMEM((n_seg, FRONT + HIST_PAD + seg_len, G_A), jnp.float32)
    return pl.pallas_call(
        functools.partial(_pool_layer_kernel, n_seg=n_seg, seg_len=seg_len, pos0=pos0,
                          has_hist=has_hist),
        out_shape=(jax.ShapeDtypeStruct(x.shape, jnp.float32),
                   jax.ShapeDtypeStruct((s, HIST_PAD, E_A), jnp.float32)),
        grid=grid,
        in_specs=in_specs,
        out_specs=(pl.BlockSpec((n_seg, seg_len, D_MODEL), lambda b, t: (b, t, 0)),
                   pl.BlockSpec((n_seg, HIST_PAD, E_A), lambda b, t: (b, 0, 0))),
        scratch_shapes=[pltpu.VMEM((m, D_MODEL), jnp.float32), stage, stage, stage,
                        pltpu.VMEM((n_seg, HIST_PAD, E_A), jnp.float32)],
        compiler_params=pltpu.CompilerParams(
            dimension_semantics=("arbitrary", "arbitrary"),
            vmem_limit_bytes=VMEM_LIMIT_BYTES),
        name="pool_layer",
    )(*args)


def _head_rms(v, gmat_ref, gain):
    sq = (v * v).astype(jnp.bfloat16)
    ss = jnp.concatenate([_bdot(sq[:, c:c + MXU_DIM], gmat_ref[...])
                          for c in range(0, v.shape[-1], MXU_DIM)], axis=1)
    return v * lax.rsqrt(ss * (1.0 / HEAD_DIM) + EPS) * gain


def _kv_kernel(*refs, tm, has_cache):
    if has_cache:
        (x_ref, ck_ref, cv_ref, g_ref, wkv_ref, gk_ref, gmat_ref,
         kout_ref, vout_ref, kt_ref, vb_ref, ktc_ref, vbc_ref) = refs
    else:
        (x_ref, g_ref, wkv_ref, gk_ref, gmat_ref,
         kout_ref, vout_ref, kt_ref, vb_ref) = refs
    x = x_ref[0]
    hb = _rms(x, g_ref[...]).astype(jnp.bfloat16)
    k = _bdot(hb, wkv_ref[:, 0:E_B])
    v = _bdot(hb, wkv_ref[:, E_B:2 * E_B])
    kn = _head_rms(k, gmat_ref, gk_ref[...])
    kout_ref[0] = kn
    vout_ref[0] = v
    kt_w = kt_ref.shape[-1]
    for p in range(N_PAIRS):
        blk = kn[:, p * LANES:(p + 1) * LANES]
        if tm < LANES:
            blk = jnp.concatenate([blk, jnp.zeros((LANES - tm, LANES), jnp.float32)], axis=0)
        kt_ref[0, p] = blk.T[:, 0:kt_w].astype(jnp.bfloat16)
        vb_ref[0, p] = v[:, p * LANES:(p + 1) * LANES].astype(jnp.bfloat16)
        if has_cache:
            ktc_ref[0, p] = ck_ref[0, :, p * LANES:(p + 1) * LANES].T.astype(jnp.bfloat16)
            vbc_ref[0, p] = cv_ref[0, :, p * LANES:(p + 1) * LANES].astype(jnp.bfloat16)


def _kv_proj(x, cache_k, cache_v, g_kv, w_kv, gk_t, gmat, *, tm):
    s, t_len, _ = x.shape
    has_cache = cache_k is not None
    n_t = t_len // tm
    keep = min(KV_ROWS, t_len)
    assert keep == tm or n_t == 1
    kt_w = max(tm, LANES)
    const2 = lambda b, t: (0, 0)
    in_specs = [pl.BlockSpec((1, tm, D_MODEL), lambda b, t: (b, t, 0))]
    args = [x]
    if has_cache:
        lc = cache_k.shape[1]
        in_specs += [pl.BlockSpec((1, lc, E_B), lambda b, t: (b, 0, 0))] * 2
        args += [cache_k, cache_v]
    in_specs += [pl.BlockSpec((1, D_MODEL), const2),
                 pl.BlockSpec((D_MODEL, 2 * E_B), const2),
                 pl.BlockSpec((1, E_B), const2),
                 pl.BlockSpec((MXU_DIM, MXU_DIM), const2)]
    args += [g_kv, w_kv, gk_t, gmat]
    out_shape = [jax.ShapeDtypeStruct((s, keep, E_B), jnp.float32),
                 jax.ShapeDtypeStruct((s, keep, E_B), jnp.float32),
                 jax.ShapeDtypeStruct((s, N_PAIRS, LANES, n_t * kt_w), jnp.bfloat16),
                 jax.ShapeDtypeStruct((s, N_PAIRS, t_len, LANES), jnp.bfloat16)]
    out_specs = [pl.BlockSpec((1, keep, E_B), lambda b, t: (b, 0, 0)),
                 pl.BlockSpec((1, keep, E_B), lambda b, t: (b, 0, 0)),
                 pl.BlockSpec((1, N_PAIRS, LANES, kt_w), lambda b, t: (b, 0, 0, t)),
                 pl.BlockSpec((1, N_PAIRS, tm, LANES), lambda b, t: (b, 0, t, 0))]
    if has_cache:
        out_shape += [jax.ShapeDtypeStruct((s, N_PAIRS, LANES, lc), jnp.bfloat16),
                      jax.ShapeDtypeStruct((s, N_PAIRS, lc, LANES), jnp.bfloat16)]
        out_specs += [pl.BlockSpec((1, N_PAIRS, LANES, lc), lambda b, t: (b, 0, 0, 0)),
                      pl.BlockSpec((1, N_PAIRS, lc, LANES), lambda b, t: (b, 0, 0, 0))]
    return pl.pallas_call(
        functools.partial(_kv_kernel, tm=tm, has_cache=has_cache),
        out_shape=tuple(out_shape),
        grid=(s, n_t),
        in_specs=in_specs,
        out_specs=tuple(out_specs),
        compiler_params=pltpu.CompilerParams(
            dimension_semantics=("arbitrary", "arbitrary"),
            vmem_limit_bytes=VMEM_LIMIT_BYTES),
        name="kv_proj",
    )(*args)


BIAS_Q = 2 * CHUNK
BIAS_K = BIAS_Q + KV_ROWS
BIAS_BASE = 1024


BIAS_HEADS_PER_STEP = 4


def _bias_kernel(base_ref, wide_ref, narrow_ref):
    qc = lax.broadcasted_iota(jnp.int32, (BIAS_Q, BIAS_K), 0) // CHUNK
    kc = lax.broadcasted_iota(jnp.int32, (BIAS_Q, BIAS_K), 1) // CHUNK
    valid = (kc >= qc) & (kc <= qc + N_LEFT_CHUNKS)
    for h in range(BIAS_HEADS_PER_STEP):
        base = base_ref[h] * LOG2E
        tiled = jnp.broadcast_to(base, (BIAS_Q, BIAS_BASE))
        toep = pltpu.roll(tiled, 0, 1, stride=1, stride_axis=0)[:, 0:BIAS_K]
        table = jnp.where(valid, toep, NEG_INF)
        wide_ref[h] = table
        narrow_ref[h] = table[0:CHUNK, 0:CHUNK + KV_ROWS]


def _bias_tables(rel_bias):
    n = rel_bias.shape[0] * N_HEADS
    f = jnp.transpose(rel_bias, (0, 2, 1)).reshape(n, N_REL)
    rev = f[:, ::-1]
    far = f[:, N_REL - 1:N_REL]
    n_left = KV_ROWS - MAX_REL
    base = jnp.concatenate(
        [jnp.broadcast_to(far, (n, n_left)), rev,
         jnp.broadcast_to(far, (n, BIAS_BASE - n_left - N_REL))], axis=1)
    base = base.reshape(n, 1, BIAS_BASE)
    hs = BIAS_HEADS_PER_STEP
    return pl.pallas_call(
        _bias_kernel,
        out_shape=(jax.ShapeDtypeStruct((n, BIAS_Q, BIAS_K), jnp.float32),
                   jax.ShapeDtypeStruct((n, CHUNK, CHUNK + KV_ROWS), jnp.float32)),
        grid=(n // hs,),
        in_specs=[pl.BlockSpec((hs, 1, BIAS_BASE), lambda h: (h, 0, 0))],
        out_specs=(pl.BlockSpec((hs, BIAS_Q, BIAS_K), lambda h: (h, 0, 0)),
                   pl.BlockSpec((hs, CHUNK, CHUNK + KV_ROWS), lambda h: (h, 0, 0))),
        compiler_params=pltpu.CompilerParams(dimension_semantics=("arbitrary",)),
        name="bias_table",
    )(base)


def _attn_layer_kernel(x_ref, kta_ref, ktb_ref, va_ref, vb_ref, bias_ref,
                       g_ref, win_ref, gq_ref, gmat_ref, wout_ref,
                       y_ref, q_ref, o_ref, z_ref, *, tm, qg, wb, mask_first):
    t = pl.program_id(1)
    n_grp = tm // qg
    cb = PIECE_COLS
    n_cb = E_B // cb
    pairs_per_cb = cb // LANES
    x = x_ref[0]
    hb = _rms(x, g_ref[...]).astype(jnp.bfloat16)
    q_gain = gq_ref[...] * (HEAD_DIM ** -0.5 * LOG2E)

    def rows_of(j):
        return slice(j * qg, (j + 1) * qg)

    def q_piece(c):
        cols = slice(c * cb, (c + 1) * cb)
        qn = _head_rms(_bdot(hb, win_ref[:, cols]), gmat_ref, q_gain[:, cols])
        for h in range(pairs_per_cb):
            q_ref[c * pairs_per_cb + h] = qn[:, h * LANES:(h + 1) * LANES].astype(jnp.bfloat16)

    def z_piece(c):
        cols = slice(c * cb, (c + 1) * cb)
        z_ref[:, cols] = _bdot(hb, win_ref[:, E_B + c * cb:E_B + (c + 1) * cb])

    def out_piece(c):
        cols = slice(c * cb, (c + 1) * cb)
        o = jnp.concatenate([o_ref[c * pairs_per_cb + h] for h in range(pairs_per_cb)],
                            axis=1)
        yb = (o * _silu(z_ref[:, cols])).astype(jnp.bfloat16)
        contrib = _bdot(yb, wout_ref[cols, :])
        if c == 0:
            y_ref[0] = x + contrib
        else:
            y_ref[0] += contrib

    lane_lo = lax.broadcasted_iota(jnp.int32, (1, LANES), 1) < HEAD_DIM
    widths = [(KV_ROWS - j * qg, (j + 1) * qg) for j in range(n_grp)]
    neg_rows = []
    for j in range(n_grp):
        if mask_first:
            col = lax.broadcasted_iota(jnp.int32, (1, KV_ROWS + qg), 1)
            dead = jnp.logical_and(t == 0, col < widths[j][0])
            neg_rows.append(jnp.where(dead, NEG_INF, 0.0))
        else:
            neg_rows.append(None)

    def scores(p, j):
        wa, wbj = widths[j]
        qb = q_ref[p, j * qg:(j + 1) * qg, :]
        zero = jnp.zeros_like(qb)
        lhs = jnp.concatenate([jnp.where(lane_lo, qb, zero),
                               jnp.where(lane_lo, zero, qb)], axis=0)
        s = jnp.concatenate(
            [_bdot(lhs, kta_ref[0, p, :, KV_ROWS - wa:KV_ROWS]),
             _bdot(lhs, ktb_ref[0, p, :, 0:wbj])], axis=1)
        s = s + bias_ref[p]
        if neg_rows[j] is not None:
            s = s + neg_rows[j]
        return s

    def attend(p, j, s):
        wa, wbj = widths[j]
        mx = jnp.max(s, axis=1, keepdims=True)
        e = jnp.exp2(s - mx)
        l = jnp.sum(e, axis=1, keepdims=True)
        pb = e.astype(jnp.bfloat16)
        o2 = (_bdot(pb[:, 0:wa], va_ref[0, p, KV_ROWS - wa:KV_ROWS, :])
              + _bdot(pb[:, wa:wa + wbj], vb_ref[0, p, 0:wbj, :]))
        o2 = o2 * (1.0 / l)
        o_ref[p, rows_of(j), :] = jnp.where(lane_lo, o2[0:qg], o2[qg:2 * qg])

    items = [(c * pairs_per_cb + h, j) for c in range(n_cb) for j in range(n_grp)
             for h in range(pairs_per_cb)]
    per_cb = len(items) // n_cb
    pieces = []
    for c in range(n_cb):
        if c > 0:
            pieces.append((per_cb * c - QK_AHEAD - 1, 0, len(pieces), q_piece, c))
        pieces.append((per_cb * c + per_cb // 2, 0, len(pieces), z_piece, c))
        if c + 1 < n_cb:
            pieces.append((per_cb * (c + 1) + per_cb // 2, per_cb * (c + 1), len(pieces),
                           out_piece, c))
    n_pieces = len(pieces)
    q_piece(0)
    pending = {}
    for i in range(min(QK_AHEAD, len(items))):
        pending[i] = scores(*items[i])
    issued = 0
    for i in range(len(items)):
        quota = ((i + 1) * n_pieces) // len(items) - issued
        ready = sorted(pc for pc in pieces if pc[1] <= i)
        take = [pc for pc in ready if pc[0] <= i]
        take += [pc for pc in ready if pc[0] > i][:max(0, quota - len(take))]
        for pc in take:
            pieces.remove(pc)
            pc[3](pc[4])
            issued += 1
        if i + QK_AHEAD < len(items):
            pending[i + QK_AHEAD] = scores(*items[i + QK_AHEAD])
        attend(*items[i], pending.pop(i))
    for pc in sorted(pieces):
        pc[3](pc[4])
    out_piece(n_cb - 1)


def _attn_layer(x, kt_a, kt_b, v_a, v_b, bias, g_norm, w_in, gq_t, gmat, w_out,
                *, layer, tm, qg, a_is_previous_tile):
    s, t_len, _ = x.shape
    n_t = t_len // tm
    wb = kt_b.shape[-1] // n_t
    const2 = lambda b, t: (0, 0)
    if a_is_previous_tile:
        assert tm == KV_ROWS
        prev = lambda b, t: jnp.maximum(t - 1, 0)
        kta_spec = pl.BlockSpec((1, N_PAIRS, LANES, KV_ROWS), lambda b, t: (b, 0, 0, prev(b, t)))
        va_spec = pl.BlockSpec((1, N_PAIRS, KV_ROWS, LANES), lambda b, t: (b, 0, prev(b, t), 0))
    else:
        kta_spec = pl.BlockSpec((1, N_PAIRS, LANES, KV_ROWS), lambda b, t: (b, 0, 0, 0))
        va_spec = pl.BlockSpec((1, N_PAIRS, KV_ROWS, LANES), lambda b, t: (b, 0, 0, 0))
    in_specs = [
        pl.BlockSpec((1, tm, D_MODEL), lambda b, t: (b, t, 0)),
        kta_spec,
        pl.BlockSpec((1, N_PAIRS, LANES, wb), lambda b, t: (b, 0, 0, t)),
        va_spec,
        pl.BlockSpec((1, N_PAIRS, tm, LANES), lambda b, t: (b, 0, t, 0)),
        pl.BlockSpec((None,) + bias.shape[1:], lambda b, t: (layer, 0, 0, 0)),
        pl.BlockSpec((1, D_MODEL), const2),
        pl.BlockSpec((D_MODEL, 2 * E_B), const2),
        pl.BlockSpec((1, E_B), const2),
        pl.BlockSpec((MXU_DIM, MXU_DIM), const2),
        pl.BlockSpec((E_B, D_MODEL), const2),
    ]
    return pl.pallas_call(
        functools.partial(_attn_layer_kernel, tm=tm, qg=qg, wb=wb,
                          mask_first=a_is_previous_tile),
        out_shape=jax.ShapeDtypeStruct(x.shape, jnp.float32),
        grid=(s, n_t),
        in_specs=in_specs,
        out_specs=pl.BlockSpec((1, tm, D_MODEL), lambda b, t: (b, t, 0)),
        scratch_shapes=[pltpu.VMEM((N_PAIRS, tm, LANES), jnp.bfloat16),
                        pltpu.VMEM((N_PAIRS, tm, LANES), jnp.float32),
                        pltpu.VMEM((tm, E_B), jnp.float32)],
        compiler_params=pltpu.CompilerParams(
            dimension_semantics=("arbitrary", "arbitrary"),
            vmem_limit_bytes=VMEM_LIMIT_BYTES),
        name="attn_layer",
    )(x, kt_a, kt_b, v_a, v_b, bias, g_norm, w_in, gq_t, gmat, w_out)


def _trunk(x, hist16, cache_k, cache_v, pos0, w, *, pool_seg, pool_len, tile):
    s, t_len, _ = x.shape
    hists = []
    for layer in range(2):
        h_in = None if hist16 is None else hist16[layer]
        x, h_out = _pool_layer(x, h_in, w["norm_a"][layer], w["w_in_a"][layer], w["w_grp_a"][layer],
                               w["scale_a"][layer], w["w_out_a"][layer],
                               n_seg=pool_seg, seg_len=pool_len, pos0=pos0)
        hists.append(h_out[:, HIST_PAD - POOL_HIST:, :])
    if cache_k is None:
        k_new, v_new, kt, vb = _kv_proj(x, None, None, w["norm_kv"], w["w_kv"], w["gk_t"],
                                        w["gmat"], tm=tile)
        srcs = (kt, kt, vb, vb)
        bias = w["bias_prompt"]
        qg = BIAS_Q
    else:
        lc = cache_k.shape[1]
        k_new, v_new, kt, vb, ktc, vbc = _kv_proj(
            x, cache_k.reshape(s, lc, E_B), cache_v.reshape(s, lc, E_B),
            w["norm_kv"], w["w_kv"], w["gk_t"], w["gmat"], tm=tile)
        srcs = (ktc, kt, vbc, vb)
        bias = w["bias_sample"]
        qg = CHUNK
    for j in range(2):
        x = _attn_layer(x, *srcs, bias, w["norm_b"][j], w["w_in_b"][j], w["gq_t"][j],
                        w["gmat"], w["w_out_b"][j], layer=j, tm=tile, qg=qg,
                        a_is_previous_tile=cache_k is None)
    return x, jnp.stack(hists, axis=0), k_new, v_new


def kernel(x_prompt, x_sample, state_pool, cache_k, cache_v, norm_a, w_in_a, w_grp_a, scale_a,
           w_out_a, norm_kv, w_kv, g_k, norm_b, w_in_b, g_q, rel_bias_b, w_out_b):
    bf = jnp.bfloat16
    head_of_lane = jnp.arange(MXU_DIM, dtype=jnp.int32) // HEAD_DIM
    gmat = (head_of_lane[:, None] == head_of_lane[None, :]).astype(bf)
    n_b = rel_bias_b.shape[0]
    wide, narrow = _bias_tables(rel_bias_b)
    pair_rows = HEADS_PER_LANE_BLOCK
    w = dict(
        norm_a=norm_a[:, None, :], w_in_a=w_in_a.astype(bf), w_grp_a=w_grp_a.astype(bf),
        scale_a=scale_a[:, None, :], w_out_a=w_out_a.astype(bf),
        norm_kv=norm_kv[None, :], w_kv=w_kv.astype(bf), gk_t=jnp.tile(g_k, N_HEADS)[None, :],
        norm_b=norm_b[:, None, :], w_in_b=w_in_b.astype(bf),
        gq_t=jnp.tile(g_q, (1, N_HEADS))[:, None, :], w_out_b=w_out_b.astype(bf), gmat=gmat,
        bias_prompt=wide.reshape(n_b, N_PAIRS, pair_rows * BIAS_Q, BIAS_K),
        bias_sample=narrow.reshape(n_b, N_PAIRS, pair_rows * CHUNK, CHUNK + KV_ROWS),
    )
    bp, sp, _ = x_prompt.shape
    bs, ss, _ = x_sample.shape
    y_p, pool_p, k_p, v_p = _trunk(x_prompt, None, None, None, 0, w,
                                   pool_seg=1, pool_len=KV_ROWS, tile=KV_ROWS)
    hist16 = jnp.pad(state_pool, ((0, 0), (0, 0), (HIST_PAD - POOL_HIST, 0), (0, 0)))
    y_s, pool_s, k_s, v_s = _trunk(x_sample, hist16, cache_k, cache_v, PAST_LEN, w,
                                   pool_seg=bs, pool_len=ss, tile=ss)
    lp = min(KV_ROWS, sp)
    return (y_p, y_s, pool_p, pool_s,
            k_p.reshape(bp, lp, N_HEADS, HEAD_DIM), v_p.reshape(bp, lp, N_HEADS, HEAD_DIM),
            k_s.reshape(bs, ss, N_HEADS, HEAD_DIM), v_s.reshape(bs, ss, N_HEADS, HEAD_DIM))
```

```python
import functools
import math

import jax
import jax.numpy as jnp
from jax import lax
from jax.experimental import pallas as pl
from jax.experimental.pallas import tpu as pltpu

D_MODEL = 1024
E_A = 2048
POOL_WINDOWS = (2, 4, 8, 16)
N_POOL_GROUPS = len(POOL_WINDOWS)
G_A = E_A // N_POOL_GROUPS
POOL_HIST = max(POOL_WINDOWS) - 1
HEAD_DIM = 64
N_HEADS = 16
E_B = N_HEADS * HEAD_DIM
CHUNK = 64
N_LEFT_CHUNKS = 8
KV_ROWS = N_LEFT_CHUNKS * CHUNK
MAX_REL = 128
N_REL = 2 * MAX_REL + 1
EPS = 1e-6
NEG_INF = -1e30
PAST_LEN = 4096
LOG2E = math.log2(math.e)

SUBLANES = 8
LANES = 128
MXU_DIM = 256
HEADS_PER_LANE_BLOCK = LANES // HEAD_DIM
N_PAIRS = N_HEADS // HEADS_PER_LANE_BLOCK
HIST_PAD = 2 * SUBLANES
FRONT = SUBLANES
QK_AHEAD = 2
PIECE_COLS = 2 * MXU_DIM
POOL_TILE = 1024
VMEM_LIMIT_BYTES = 56 * 1024 * 1024


def _bdot(a, b):
    return jnp.dot(a, b, preferred_element_type=jnp.float32)


def _rms(x, g):
    ms = jnp.mean(x * x, axis=-1, keepdims=True)
    return x * lax.rsqrt(ms + EPS) * g


def _silu(z):
    return z * (1.0 / (1.0 + jnp.exp(-z)))


def _pool_layer_kernel(*refs, n_seg, seg_len, pos0, has_hist):
    if has_hist:
        (x_ref, hist_ref, g_ref, win_ref, wgrp_ref, scale_ref, wout_ref,
         y_ref, hout_ref, acc_ref, upad_ref, s0_ref, s1_ref, carry_ref) = refs
    else:
        (x_ref, g_ref, win_ref, wgrp_ref, scale_ref, wout_ref,
         y_ref, hout_ref, acc_ref, upad_ref, s0_ref, s1_ref, carry_ref) = refs
        hist_ref = None
    t = pl.program_id(1)
    m = n_seg * seg_len
    rows = HIST_PAD + seg_len
    x = x_ref[...].reshape(m, D_MODEL)
    hb = _rms(x, g_ref[...]).astype(jnp.bfloat16)

    @pl.when(t == 0)
    def _():
        if has_hist:
            carry_ref[...] = hist_ref[...]
        else:
            carry_ref[...] = jnp.zeros_like(carry_ref)

    zero_front = jnp.zeros((n_seg, FRONT, G_A), jnp.float32)
    upad_ref[:, 0:FRONT, :] = zero_front
    s0_ref[:, 0:FRONT, :] = zero_front
    s1_ref[:, 0:FRONT, :] = zero_front

    pos = pos0 + t * seg_len + lax.broadcasted_iota(jnp.int32, (1, seg_len, 1), 1)

    for g, w in enumerate(POOL_WINDOWS):
        lo = g * G_A
        u = _bdot(hb, win_ref[:, lo:lo + G_A])
        z = _bdot(hb, win_ref[:, E_A + lo:E_A + lo + G_A])
        u3 = u.reshape(n_seg, seg_len, G_A)
        upad_ref[:, FRONT:FRONT + HIST_PAD, :] = carry_ref[:, :, lo:lo + G_A]
        upad_ref[:, FRONT + HIST_PAD:, :] = u3
        carry_ref[:, :, lo:lo + G_A] = upad_ref[:, FRONT + seg_len:, :]
        bufs = (upad_ref, s0_ref, s1_ref)
        src = 0
        for k in range(g + 1):
            dst = 1 if src != 1 else 2
            sh = 1 << k
            bufs[dst][:, FRONT:, :] = (bufs[src][:, FRONT:, :]
                                       + bufs[src][:, FRONT - sh:FRONT - sh + rows, :])
            src = dst
        wsum = bufs[src][:, FRONT + HIST_PAD:, :]
        inv_cnt = 1.0 / jnp.minimum(pos + 1, w).astype(jnp.float32)
        pooled = (wsum * inv_cnt - u3).reshape(m, G_A)
        pg = _bdot(pooled.astype(jnp.bfloat16), wgrp_ref[g]) * scale_ref[:, lo:lo + G_A]
        yb = (pg * _silu(z)).astype(jnp.bfloat16)
        contrib = _bdot(yb, wout_ref[lo:lo + G_A, :])
        if g == 0:
            acc_ref[...] = contrib
        else:
            acc_ref[...] += contrib

    y_ref[...] = (x + acc_ref[...]).reshape(n_seg, seg_len, D_MODEL)
    hout_ref[...] = carry_ref[...]


def _layer_spec(arr, layer):
    tail = (0,) * (arr.ndim - 1)
    return pl.BlockSpec((None,) + arr.shape[1:], lambda b, t: (layer,) + tail,
                        pipeline_mode=pl.Buffered(1))


def _pool_layer(x, hist16, layer, g_norm, w_in, w_grp, scale, w_out, *, n_seg, seg_len, pos0):
    s, t_len, _ = x.shape
    has_hist = hist16 is not None
    grid = (s // n_seg, t_len // seg_len)
    in_specs = [pl.BlockSpec((n_seg, seg_len, D_MODEL), lambda b, t: (b, t, 0))]
    args = [x]
    if has_hist:
        in_specs.append(pl.BlockSpec((None, n_seg, HIST_PAD, E_A), lambda b, t: (layer, b, 0, 0)))
        args.append(hist16)
    args += [g_norm, w_in, w_grp, scale, w_out]
    in_specs += [_layer_spec(a, layer) for a in args[-5:]]
    m = n_seg * seg_len
    stage = pltpu.VMEM((n_seg, FRONT + HIST_PAD + seg_len, G_A), jnp.float32)
    return pl.pallas_call(
        functools.partial(_pool_layer_kernel, n_seg=n_seg, seg_len=seg_len, pos0=pos0,
                          has_hist=has_hist),
        out_shape=(jax.ShapeDtypeStruct(x.shape, jnp.float32),
                   jax.ShapeDtypeStruct((s, HIST_PAD, E_A), jnp.float32)),
        grid=grid,
        in_specs=in_specs,
        out_specs=(pl.BlockSpec((n_seg, seg_len, D_MODEL), lambda b, t: (b, t, 0)),
                   pl.BlockSpec((n_seg, HIST_PAD, E_A), lambda b, t: (b, 0, 0))),
        scratch_shapes=[pltpu.VMEM((m, D_MODEL), jnp.float32), stage, stage, stage,
                        pltpu.VMEM((n_seg, HIST_PAD, E_A), jnp.float32)],
        compiler_params=pltpu.CompilerParams(
            dimension_semantics=("arbitrary", "arbitrary"),
            vmem_limit_bytes=VMEM_LIMIT_BYTES),
        name="pool_layer",
    )(*args)


def _head_rms(v, gmat_ref, gain):
    sq = (v * v).astype(jnp.bfloat16)
    ss = jnp.concatenate([_bdot(sq[:, c:c + MXU_DIM], gmat_ref[...])
                          for c in range(0, v.shape[-1], MXU_DIM)], axis=1)
    return v * lax.rsqrt(ss * (1.0 / HEAD_DIM) + EPS) * gain


def _kv_kernel(*refs, tm, has_cache):
    if has_cache:
        (x_ref, ck_ref, cv_ref, g_ref, wkv_ref, gk_ref, gmat_ref,
         kout_ref, vout_ref, kt_ref, vb_ref, ktc_ref, vbc_ref) = refs
    else:
        (x_ref, g_ref, wkv_ref, gk_ref, gmat_ref,
         kout_ref, vout_ref, kt_ref, vb_ref) = refs
    x = x_ref[0]
    hb = _rms(x, g_ref[...]).astype(jnp.bfloat16)
    k = _bdot(hb, wkv_ref[:, 0:E_B])
    v = _bdot(hb, wkv_ref[:, E_B:2 * E_B])
    kn = _head_rms(k, gmat_ref, gk_ref[...])
    kout_ref[0] = kn
    vout_ref[0] = v
    kt_w = kt_ref.shape[-1]
    if has_cache:
        lane = lax.broadcasted_iota(jnp.int32, (LANES, HEAD_DIM), 0)
        chan = lax.broadcasted_iota(jnp.int32, (LANES, HEAD_DIM), 1)
        lane_t = lax.broadcasted_iota(jnp.int32, (HEAD_DIM, LANES), 1)
        chan_t = lax.broadcasted_iota(jnp.int32, (HEAD_DIM, LANES), 0)
        place = [(lane == chan + h * HEAD_DIM).astype(jnp.bfloat16)
                 for h in range(HEADS_PER_LANE_BLOCK)]
        place_t = [(lane_t == chan_t + h * HEAD_DIM).astype(jnp.bfloat16)
                   for h in range(HEADS_PER_LANE_BLOCK)]
        contract_last = (((1,), (1,)), ((), ()))
    for p in range(N_PAIRS):
        blk = kn[:, p * LANES:(p + 1) * LANES]
        if tm < LANES:
            blk = jnp.concatenate([blk, jnp.zeros((LANES - tm, LANES), jnp.float32)], axis=0)
        kt_ref[0, p] = blk.T[:, 0:kt_w].astype(jnp.bfloat16)
        vb_ref[0, p] = v[:, p * LANES:(p + 1) * LANES].astype(jnp.bfloat16)
        if has_cache:
            kt_pair = None
            v_pair = None
            for h in range(HEADS_PER_LANE_BLOCK):
                head = p * HEADS_PER_LANE_BLOCK + h
                ck = ck_ref[0, head].astype(jnp.bfloat16)
                cv = cv_ref[0, head].astype(jnp.bfloat16)
                kt_h = lax.dot_general(place[h], ck, contract_last,
                                       preferred_element_type=jnp.float32)
                v_h = _bdot(cv, place_t[h])
                kt_pair = kt_h if kt_pair is None else kt_pair + kt_h
                v_pair = v_h if v_pair is None else v_pair + v_h
            ktc_ref[0, p] = kt_pair.astype(jnp.bfloat16)
            vbc_ref[0, p] = v_pair.astype(jnp.bfloat16)


def _kv_proj(x, cache_k, cache_v, g_kv, w_kv, gk_t, gmat, *, tm):
    s, t_len, _ = x.shape
    has_cache = cache_k is not None
    n_t = t_len // tm
    keep = min(KV_ROWS, t_len)
    assert keep == tm or n_t == 1
    kt_w = max(tm, LANES)
    const2 = lambda b, t: (0, 0)
    in_specs = [pl.BlockSpec((1, tm, D_MODEL), lambda b, t: (b, t, 0))]
    args = [x]
    if has_cache:
        lc = cache_k.shape[2]
        in_specs += [pl.BlockSpec((1, N_HEADS, lc, HEAD_DIM), lambda b, t: (b, 0, 0, 0))] * 2
        args += [cache_k, cache_v]
    in_specs += [pl.BlockSpec((1, D_MODEL), const2),
                 pl.BlockSpec((D_MODEL, 2 * E_B), const2),
                 pl.BlockSpec((1, E_B), const2),
                 pl.BlockSpec((MXU_DIM, MXU_DIM), const2)]
    args += [g_kv, w_kv, gk_t, gmat]
    out_shape = [jax.ShapeDtypeStruct((s, keep, E_B), jnp.float32),
                 jax.ShapeDtypeStruct((s, keep, E_B), jnp.float32),
                 jax.ShapeDtypeStruct((s, N_PAIRS, LANES, n_t * kt_w), jnp.bfloat16),
                 jax.ShapeDtypeStruct((s, N_PAIRS, t_len, LANES), jnp.bfloat16)]
    out_specs = [pl.BlockSpec((1, keep, E_B), lambda b, t: (b, 0, 0)),
                 pl.BlockSpec((1, keep, E_B), lambda b, t: (b, 0, 0)),
                 pl.BlockSpec((1, N_PAIRS, LANES, kt_w), lambda b, t: (b, 0, 0, t)),
                 pl.BlockSpec((1, N_PAIRS, tm, LANES), lambda b, t: (b, 0, t, 0))]
    if has_cache:
        out_shape += [jax.ShapeDtypeStruct((s, N_PAIRS, LANES, lc), jnp.bfloat16),
                      jax.ShapeDtypeStruct((s, N_PAIRS, lc, LANES), jnp.bfloat16)]
        out_specs += [pl.BlockSpec((1, N_PAIRS, LANES, lc), lambda b, t: (b, 0, 0, 0)),
                      pl.BlockSpec((1, N_PAIRS, lc, LANES), lambda b, t: (b, 0, 0, 0))]
    return pl.pallas_call(
        functools.partial(_kv_kernel, tm=tm, has_cache=has_cache),
        out_shape=tuple(out_shape),
        grid=(s, n_t),
        in_specs=in_specs,
        out_specs=tuple(out_specs),
        compiler_params=pltpu.CompilerParams(
            dimension_semantics=("arbitrary", "arbitrary"),
            vmem_limit_bytes=VMEM_LIMIT_BYTES),
        name="kv_proj",
    )(*args)


BIAS_Q = 2 * CHUNK
BIAS_K = BIAS_Q + KV_ROWS
BIAS_BASE = 1024


BIAS_HEADS_PER_STEP = 4


def _bias_kernel(base_ref, wide_ref, narrow_ref):
    qc = lax.broadcasted_iota(jnp.int32, (BIAS_Q, BIAS_K), 0) // CHUNK
    kc = lax.broadcasted_iota(jnp.int32, (BIAS_Q, BIAS_K), 1) // CHUNK
    valid = (kc >= qc) & (kc <= qc + N_LEFT_CHUNKS)
    for h in range(BIAS_HEADS_PER_STEP):
        base = base_ref[h] * LOG2E
        tiled = jnp.broadcast_to(base, (BIAS_Q, BIAS_BASE))
        toep = pltpu.roll(tiled, 0, 1, stride=1, stride_axis=0)[:, 0:BIAS_K]
        table = jnp.where(valid, toep, NEG_INF)
        wide_ref[h] = table
        narrow_ref[h] = table[0:CHUNK, 0:CHUNK + KV_ROWS]


def _bias_tables(rel_bias):
    n = rel_bias.shape[0] * N_HEADS
    f = jnp.transpose(rel_bias, (0, 2, 1)).reshape(n, N_REL)
    rev = f[:, ::-1]
    far = f[:, N_REL - 1:N_REL]
    n_left = KV_ROWS - MAX_REL
    base = jnp.concatenate(
        [jnp.broadcast_to(far, (n, n_left)), rev,
         jnp.broadcast_to(far, (n, BIAS_BASE - n_left - N_REL))], axis=1)
    base = base.reshape(n, 1, BIAS_BASE)
    hs = BIAS_HEADS_PER_STEP
    return pl.pallas_call(
        _bias_kernel,
        out_shape=(jax.ShapeDtypeStruct((n, BIAS_Q, BIAS_K), jnp.float32),
                   jax.ShapeDtypeStruct((n, CHUNK, CHUNK + KV_ROWS), jnp.float32)),
        grid=(n // hs,),
        in_specs=[pl.BlockSpec((hs, 1, BIAS_BASE), lambda h: (h, 0, 0))],
        out_specs=(pl.BlockSpec((hs, BIAS_Q, BIAS_K), lambda h: (h, 0, 0)),
                   pl.BlockSpec((hs, CHUNK, CHUNK + KV_ROWS), lambda h: (h, 0, 0))),
        compiler_params=pltpu.CompilerParams(dimension_semantics=("arbitrary",)),
        name="bias_table",
    )(base)


def _attn_layer_kernel(x_ref, kta_ref, ktb_ref, va_ref, vb_ref, bias_ref,
                       g_ref, win_ref, gq_ref, gmat_ref, wout_ref,
                       y_ref, q_ref, o_ref, z_ref, *, tm, qg, wb, mask_first):
    t = pl.program_id(1)
    n_grp = tm // qg
    cb = PIECE_COLS
    n_cb = E_B // cb
    pairs_per_cb = cb // LANES
    x = x_ref[0]
    hb = _rms(x, g_ref[...]).astype(jnp.bfloat16)
    q_gain = gq_ref[...] * (HEAD_DIM ** -0.5 * LOG2E)

    def rows_of(j):
        return slice(j * qg, (j + 1) * qg)

    def q_piece(c):
        cols = slice(c * cb, (c + 1) * cb)
        qn = _head_rms(_bdot(hb, win_ref[:, cols]), gmat_ref, q_gain[:, cols])
        for h in range(pairs_per_cb):
            q_ref[c * pairs_per_cb + h] = qn[:, h * LANES:(h + 1) * LANES].astype(jnp.bfloat16)

    def z_piece(c):
        cols = slice(c * cb, (c + 1) * cb)
        z_ref[:, cols] = _bdot(hb, win_ref[:, E_B + c * cb:E_B + (c + 1) * cb])

    def out_piece(c):
        cols = slice(c * cb, (c + 1) * cb)
        o = jnp.concatenate([o_ref[c * pairs_per_cb + h] for h in range(pairs_per_cb)],
                            axis=1)
        yb = (o * _silu(z_ref[:, cols])).astype(jnp.bfloat16)
        contrib = _bdot(yb, wout_ref[cols, :])
        if c == 0:
            y_ref[0] = x + contrib
        else:
            y_ref[0] += contrib

    lane_lo = lax.broadcasted_iota(jnp.int32, (1, LANES), 1) < HEAD_DIM
    widths = [(KV_ROWS - j * qg, (j + 1) * qg) for j in range(n_grp)]
    neg_rows = []
    for j in range(n_grp):
        if mask_first:
            col = lax.broadcasted_iota(jnp.int32, (1, KV_ROWS + qg), 1)
            dead = jnp.logical_and(t == 0, col < widths[j][0])
            neg_rows.append(jnp.where(dead, NEG_INF, 0.0))
        else:
            neg_rows.append(None)

    def scores(p, j):
        wa, wbj = widths[j]
        qb = q_ref[p, j * qg:(j + 1) * qg, :]
        zero = jnp.zeros_like(qb)
        lhs = jnp.concatenate([jnp.where(lane_lo, qb, zero),
                               jnp.where(lane_lo, zero, qb)], axis=0)
        s = jnp.concatenate(
            [_bdot(lhs, kta_ref[0, p, :, KV_ROWS - wa:KV_ROWS]),
             _bdot(lhs, ktb_ref[0, p, :, 0:wbj])], axis=1)
        s = s + bias_ref[p]
        if neg_rows[j] is not None:
            s = s + neg_rows[j]
        return s

    def attend(p, j, s):
        wa, wbj = widths[j]
        mx = jnp.max(s, axis=1, keepdims=True)
        e = jnp.exp2(s - mx)
        l = jnp.sum(e, axis=1, keepdims=True)
        pb = e.astype(jnp.bfloat16)
        o2 = (_bdot(pb[:, 0:wa], va_ref[0, p, KV_ROWS - wa:KV_ROWS, :])
              + _bdot(pb[:, wa:wa + wbj], vb_ref[0, p, 0:wbj, :]))
        o2 = o2 * (1.0 / l)
        o_ref[p, rows_of(j), :] = jnp.where(lane_lo, o2[0:qg], o2[qg:2 * qg])

    items = [(c * pairs_per_cb + h, j) for c in range(n_cb) for j in range(n_grp)
             for h in range(pairs_per_cb)]
    per_cb = len(items) // n_cb
    pieces = []
    for c in range(n_cb):
        if c > 0:
            pieces.append((per_cb * c - QK_AHEAD - 1, 0, len(pieces), q_piece, c))
        pieces.append((per_cb * c + per_cb // 2, 0, len(pieces), z_piece, c))
        if c + 1 < n_cb:
            pieces.append((per_cb * (c + 1) + per_cb // 2, per_cb * (c + 1), len(pieces),
                           out_piece, c))
    n_pieces = len(pieces)
    q_piece(0)
    pending = {}
    for i in range(min(QK_AHEAD, len(items))):
        pending[i] = scores(*items[i])
    issued = 0
    for i in range(len(items)):
        quota = ((i + 1) * n_pieces) // len(items) - issued
        ready = sorted(pc for pc in pieces if pc[1] <= i)
        take = [pc for pc in ready if pc[0] <= i]
        take += [pc for pc in ready if pc[0] > i][:max(0, quota - len(take))]
        for pc in take:
            pieces.remove(pc)
            pc[3](pc[4])
            issued += 1
        if i + QK_AHEAD < len(items):
            pending[i + QK_AHEAD] = scores(*items[i + QK_AHEAD])
        attend(*items[i], pending.pop(i))
    for pc in sorted(pieces):
        pc[3](pc[4])
    out_piece(n_cb - 1)


def _attn_layer(x, kt_a, kt_b, v_a, v_b, bias, g_norm, w_in, gq_t, gmat, w_out,
                *, layer, tm, qg, a_is_previous_tile):
    s, t_len, _ = x.shape
    n_t = t_len // tm
    wb = kt_b.shape[-1] // n_t
    const2 = lambda b, t: (0, 0)
    if a_is_previous_tile:
        assert tm == KV_ROWS
        prev = lambda b, t: jnp.maximum(t - 1, 0)
        kta_spec = pl.BlockSpec((1, N_PAIRS, LANES, KV_ROWS), lambda b, t: (b, 0, 0, prev(b, t)))
        va_spec = pl.BlockSpec((1, N_PAIRS, KV_ROWS, LANES), lambda b, t: (b, 0, prev(b, t), 0))
    else:
        kta_spec = pl.BlockSpec((1, N_PAIRS, LANES, KV_ROWS), lambda b, t: (b, 0, 0, 0))
        va_spec = pl.BlockSpec((1, N_PAIRS, KV_ROWS, LANES), lambda b, t: (b, 0, 0, 0))
    in_specs = [
        pl.BlockSpec((1, tm, D_MODEL), lambda b, t: (b, t, 0)),
        kta_spec,
        pl.BlockSpec((1, N_PAIRS, LANES, wb), lambda b, t: (b, 0, 0, t)),
        va_spec,
        pl.BlockSpec((1, N_PAIRS, tm, LANES), lambda b, t: (b, 0, t, 0)),
        _layer_spec(bias, layer),
        _layer_spec(g_norm, layer),
        _layer_spec(w_in, layer),
        _layer_spec(gq_t, layer),
        pl.BlockSpec((MXU_DIM, MXU_DIM), const2),
        _layer_spec(w_out, layer),
    ]
    return pl.pallas_call(
        functools.partial(_attn_layer_kernel, tm=tm, qg=qg, wb=wb,
                          mask_first=a_is_previous_tile),
        out_shape=jax.ShapeDtypeStruct(x.shape, jnp.float32),
        grid=(s, n_t),
        in_specs=in_specs,
        out_specs=pl.BlockSpec((1, tm, D_MODEL), lambda b, t: (b, t, 0)),
        scratch_shapes=[pltpu.VMEM((N_PAIRS, tm, LANES), jnp.bfloat16),
                        pltpu.VMEM((N_PAIRS, tm, LANES), jnp.float32),
                        pltpu.VMEM((tm, E_B), jnp.float32)],
        compiler_params=pltpu.CompilerParams(
            dimension_semantics=("arbitrary", "arbitrary"),
            vmem_limit_bytes=VMEM_LIMIT_BYTES),
        name="attn_layer",
    )(x, kt_a, kt_b, v_a, v_b, bias, g_norm, w_in, gq_t, gmat, w_out)


def _trunk(x, hist16, cache_k, cache_v, pos0, w, *, pool_seg, pool_len, tile):
    s, t_len, _ = x.shape
    hists = []
    for layer in range(w["w_in_a"].shape[0]):
        x, h_out = _pool_layer(x, hist16, layer, w["norm_a"], w["w_in_a"], w["w_grp_a"],
                               w["scale_a"], w["w_out_a"],
                               n_seg=pool_seg, seg_len=pool_len, pos0=pos0)
        hists.append(h_out[:, HIST_PAD - POOL_HIST:, :])
    if cache_k is None:
        k_new, v_new, kt, vb = _kv_proj(x, None, None, w["norm_kv"], w["w_kv"], w["gk_t"],
                                        w["gmat"], tm=tile)
        srcs = (kt, kt, vb, vb)
        bias = w["bias_prompt"]
        qg = BIAS_Q
    else:
        k_new, v_new, kt, vb, ktc, vbc = _kv_proj(
            x, jnp.transpose(cache_k, (0, 2, 1, 3)), jnp.transpose(cache_v, (0, 2, 1, 3)),
            w["norm_kv"], w["w_kv"], w["gk_t"], w["gmat"], tm=tile)
        srcs = (ktc, kt, vbc, vb)
        bias = w["bias_sample"]
        qg = CHUNK
    for j in range(w["w_in_b"].shape[0]):
        x = _attn_layer(x, *srcs, bias, w["norm_b"], w["w_in_b"], w["gq_t"],
                        w["gmat"], w["w_out_b"], layer=j, tm=tile, qg=qg,
                        a_is_previous_tile=cache_k is None)
    return x, jnp.stack(hists, axis=0), k_new, v_new


def kernel(x_prompt, x_sample, state_pool, cache_k, cache_v, norm_a, w_in_a, w_grp_a, scale_a,
           w_out_a, norm_kv, w_kv, g_k, norm_b, w_in_b, g_q, rel_bias_b, w_out_b):
    bf = jnp.bfloat16
    head_of_lane = jnp.arange(MXU_DIM, dtype=jnp.int32) // HEAD_DIM
    gmat = (head_of_lane[:, None] == head_of_lane[None, :]).astype(bf)
    n_b = rel_bias_b.shape[0]
    wide, narrow = _bias_tables(rel_bias_b)
    pair_rows = HEADS_PER_LANE_BLOCK
    w = dict(
        norm_a=norm_a[:, None, :], w_in_a=w_in_a.astype(bf), w_grp_a=w_grp_a.astype(bf),
        scale_a=scale_a[:, None, :], w_out_a=w_out_a.astype(bf),
        norm_kv=norm_kv[None, :], w_kv=w_kv.astype(bf), gk_t=jnp.tile(g_k, N_HEADS)[None, :],
        norm_b=norm_b[:, None, :], w_in_b=w_in_b.astype(bf),
        gq_t=jnp.tile(g_q, (1, N_HEADS))[:, None, :], w_out_b=w_out_b.astype(bf), gmat=gmat,
        bias_prompt=wide.reshape(n_b, N_PAIRS, pair_rows * BIAS_Q, BIAS_K),
        bias_sample=narrow.reshape(n_b, N_PAIRS, pair_rows * CHUNK, CHUNK + KV_ROWS),
    )
    bp, sp, _ = x_prompt.shape
    bs, ss, _ = x_sample.shape
    y_p, pool_p, k_p, v_p = _trunk(x_prompt, None, None, None, 0, w,
                                   pool_seg=1, pool_len=min(POOL_TILE, sp), tile=KV_ROWS)
    hist16 = jnp.pad(state_pool, ((0, 0), (0, 0), (HIST_PAD - POOL_HIST, 0), (0, 0)))
    y_s, pool_s, k_s, v_s = _trunk(x_sample, hist16, cache_k, cache_v, PAST_LEN, w,
                                   pool_seg=bs, pool_len=ss, tile=ss)
    lp = min(KV_ROWS, sp)
    return (y_p, y_s, pool_p, pool_s,
            k_p.reshape(bp, lp, N_HEADS, HEAD_DIM), v_p.reshape(bp, lp, N_HEADS, HEAD_DIM),
            k_s.reshape(bs, ss, N_HEADS, HEAD_DIM), v_s.reshape(bs, ss, N_HEADS, HEAD_DIM))
```

```python
import functools
import math

import jax
import jax.numpy as jnp
from jax import lax
from jax.experimental import pallas as pl
from jax.experimental.pallas import tpu as pltpu

D_MODEL = 1024
E_A = 2048
POOL_WINDOWS = (2, 4, 8, 16)
N_POOL_GROUPS = len(POOL_WINDOWS)
G_A = E_A // N_POOL_GROUPS
POOL_HIST = max(POOL_WINDOWS) - 1
HEAD_DIM = 64
N_HEADS = 16
E_B = N_HEADS * HEAD_DIM
CHUNK = 64
N_LEFT_CHUNKS = 8
KV_ROWS = N_LEFT_CHUNKS * CHUNK
MAX_REL = 128
N_REL = 2 * MAX_REL + 1
EPS = 1e-6
NEG_INF = -1e30
PAST_LEN = 4096
LOG2E = math.log2(math.e)

SUBLANES = 8
LANES = 128
MXU_DIM = 256
HEADS_PER_LANE_BLOCK = LANES // HEAD_DIM
N_PAIRS = N_HEADS // HEADS_PER_LANE_BLOCK
HIST_PAD = 2 * SUBLANES
FRONT = SUBLANES
QK_AHEAD = 2
PIECE_COLS = 2 * MXU_DIM
POOL_TILE = 1024
VMEM_LIMIT_BYTES = 56 * 1024 * 1024


def _bdot(a, b):
    return jnp.dot(a, b, preferred_element_type=jnp.float32)


def _rms(x, g):
    ms = jnp.mean(x * x, axis=-1, keepdims=True)
    return x * lax.rsqrt(ms + EPS) * g


def _silu(z):
    return z * (1.0 / (1.0 + jnp.exp(-z)))


def _pool_layer_kernel(*refs, n_seg, seg_len, pos0, has_hist):
    if has_hist:
        (x_ref, hist_ref, g_ref, win_ref, wgrp_ref, scale_ref, wout_ref,
         y_ref, hout_ref, acc_ref, upad_ref, s0_ref, s1_ref, carry_ref) = refs
    else:
        (x_ref, g_ref, win_ref, wgrp_ref, scale_ref, wout_ref,
         y_ref, hout_ref, acc_ref, upad_ref, s0_ref, s1_ref, carry_ref) = refs
        hist_ref = None
    t = pl.program_id(1)
    m = n_seg * seg_len
    rows = HIST_PAD + seg_len
    x = x_ref[...].reshape(m, D_MODEL)
    hb = _rms(x, g_ref[...]).astype(jnp.bfloat16)

    @pl.when(t == 0)
    def _():
        if has_hist:
            carry_ref[...] = hist_ref[...]
        else:
            carry_ref[...] = jnp.zeros_like(carry_ref)

    zero_front = jnp.zeros((n_seg, FRONT, G_A), jnp.float32)
    upad_ref[:, 0:FRONT, :] = zero_front
    s0_ref[:, 0:FRONT, :] = zero_front
    s1_ref[:, 0:FRONT, :] = zero_front

    pos = pos0 + t * seg_len + lax.broadcasted_iota(jnp.int32, (1, seg_len, 1), 1)

    for g, w in enumerate(POOL_WINDOWS):
        lo = g * G_A
        u = _bdot(hb, win_ref[:, lo:lo + G_A])
        z = _bdot(hb, win_ref[:, E_A + lo:E_A + lo + G_A])
        u3 = u.reshape(n_seg, seg_len, G_A)
        upad_ref[:, FRONT:FRONT + HIST_PAD, :] = carry_ref[:, :, lo:lo + G_A]
        upad_ref[:, FRONT + HIST_PAD:, :] = u3
        carry_ref[:, :, lo:lo + G_A] = upad_ref[:, FRONT + seg_len:, :]
        bufs = (upad_ref, s0_ref, s1_ref)
        src = 0
        for k in range(g + 1):
            dst = 1 if src != 1 else 2
            sh = 1 << k
            bufs[dst][:, FRONT:, :] = (bufs[src][:, FRONT:, :]
                                       + bufs[src][:, FRONT - sh:FRONT - sh + rows, :])
            src = dst
        wsum = bufs[src][:, FRONT + HIST_PAD:, :]
        inv_cnt = 1.0 / jnp.minimum(pos + 1, w).astype(jnp.float32)
        pooled = (wsum * inv_cnt - u3).reshape(m, G_A)
        pg = _bdot(pooled.astype(jnp.bfloat16), wgrp_ref[g]) * scale_ref[:, lo:lo + G_A]
        yb = (pg * _silu(z)).astype(jnp.bfloat16)
        contrib = _bdot(yb, wout_ref[lo:lo + G_A, :])
        if g == 0:
            acc_ref[...] = contrib
        else:
            acc_ref[...] += contrib

    y_ref[...] = (x + acc_ref[...]).reshape(n_seg, seg_len, D_MODEL)
    hout_ref[...] = carry_ref[...]


def _layer_spec(arr, layer):
    tail = (0,) * (arr.ndim - 1)
    return pl.BlockSpec((None,) + arr.shape[1:], lambda b, t: (layer,) + tail,
                        pipeline_mode=pl.Buffered(1))


def _pool_layer(x, hist16, layer, g_norm, w_in, w_grp, scale, w_out, *, n_seg, seg_len, pos0):
    s, t_len, _ = x.shape
    has_hist = hist16 is not None
    grid = (s // n_seg, t_len // seg_len)
    in_specs = [pl.BlockSpec((n_seg, seg_len, D_MODEL), lambda b, t: (b, t, 0))]
    args = [x]
    if has_hist:
        in_specs.append(pl.BlockSpec((None, n_seg, HIST_PAD, E_A), lambda b, t: (layer, b, 0, 0)))
        args.append(hist16)
    args += [g_norm, w_in, w_grp, scale, w_out]
    in_specs += [_layer_spec(a, layer) for a in args[-5:]]
    m = n_seg * seg_len
    stage = pltpu.VMEM((n_seg, FRONT + HIST_PAD + seg_len, G_A), jnp.float32)
    return pl.pallas_call(
        functools.partial(_pool_layer_kernel, n_seg=n_seg, seg_len=seg_len, pos0=pos0,
                          has_hist=has_hist),
        out_shape=(jax.ShapeDtypeStruct(x.shape, jnp.float32),
                   jax.ShapeDtypeStruct((s, HIST_PAD, E_A), jnp.float32)),
        grid=grid,
        in_specs=in_specs,
        out_specs=(pl.BlockSpec((n_seg, seg_len, D_MODEL), lambda b, t: (b, t, 0)),
                   pl.BlockSpec((n_seg, HIST_PAD, E_A), lambda b, t: (b, 0, 0))),
        scratch_shapes=[pltpu.VMEM((m, D_MODEL), jnp.float32), stage, stage, stage,
                        pltpu.VMEM((n_seg, HIST_PAD, E_A), jnp.float32)],
        compiler_params=pltpu.CompilerParams(
            dimension_semantics=("arbitrary", "arbitrary"),
            vmem_limit_bytes=VMEM_LIMIT_BYTES),
        name="pool_layer",
    )(*args)


def _head_rms(v, gmat_ref, gain):
    sq = (v * v).astype(jnp.bfloat16)
    ss = jnp.concatenate([_bdot(sq[:, c:c + MXU_DIM], gmat_ref[...])
                          for c in range(0, v.shape[-1], MXU_DIM)], axis=1)
    return v * lax.rsqrt(ss * (1.0 / HEAD_DIM) + EPS) * gain


def _kv_kernel(x_ref, g_ref, wkv_ref, gk_ref, gmat_ref, kout_ref, vout_ref, kt_ref, vb_ref, *, tm):
    x = x_ref[0]
    hb = _rms(x, g_ref[...]).astype(jnp.bfloat16)
    k = _bdot(hb, wkv_ref[:, 0:E_B])
    kn = _head_rms(k, gmat_ref, gk_ref[...])
    v = _bdot(hb, wkv_ref[:, E_B:2 * E_B])
    kout_ref[0] = kn
    vout_ref[0] = v
    kt_w = kt_ref.shape[-1]
    for p in range(N_PAIRS):
        blk = kn[:, p * LANES:(p + 1) * LANES]
        if tm < LANES:
            blk = jnp.concatenate([blk, jnp.zeros((LANES - tm, LANES), jnp.float32)], axis=0)
        kt_ref[0, p] = blk.T[:, 0:kt_w].astype(jnp.bfloat16)
        vb_ref[0, p] = v[:, p * LANES:(p + 1) * LANES].astype(jnp.bfloat16)


def _kv_proj(x, g_kv, w_kv, gk_t, gmat, *, tm):
    s, t_len, _ = x.shape
    n_t = t_len // tm
    keep = min(KV_ROWS, t_len)
    assert keep == tm or n_t == 1
    kt_w = max(tm, LANES)
    const2 = lambda b, t: (0, 0)
    in_specs = [pl.BlockSpec((1, tm, D_MODEL), lambda b, t: (b, t, 0)),
                pl.BlockSpec((1, D_MODEL), const2),
                pl.BlockSpec((D_MODEL, 2 * E_B), const2),
                pl.BlockSpec((1, E_B), const2),
                pl.BlockSpec((MXU_DIM, MXU_DIM), const2)]
    args = [x, g_kv, w_kv, gk_t, gmat]
    out_shape = [jax.ShapeDtypeStruct((s, keep, E_B), jnp.float32),
                 jax.ShapeDtypeStruct((s, keep, E_B), jnp.float32),
                 jax.ShapeDtypeStruct((s, N_PAIRS, LANES, n_t * kt_w), jnp.bfloat16),
                 jax.ShapeDtypeStruct((s, N_PAIRS, t_len, LANES), jnp.bfloat16)]
    out_specs = [pl.BlockSpec((1, keep, E_B), lambda b, t: (b, 0, 0)),
                 pl.BlockSpec((1, keep, E_B), lambda b, t: (b, 0, 0)),
                 pl.BlockSpec((1, N_PAIRS, LANES, kt_w), lambda b, t: (b, 0, 0, t)),
                 pl.BlockSpec((1, N_PAIRS, tm, LANES), lambda b, t: (b, 0, t, 0))]
    return pl.pallas_call(
        functools.partial(_kv_kernel, tm=tm),
        out_shape=tuple(out_shape),
        grid=(s, n_t),
        in_specs=in_specs,
        out_specs=tuple(out_specs),
        compiler_params=pltpu.CompilerParams(
            dimension_semantics=("arbitrary", "arbitrary"),
            vmem_limit_bytes=VMEM_LIMIT_BYTES),
        name="kv_proj",
    )(*args)


BIAS_Q = 2 * CHUNK
BIAS_K = BIAS_Q + KV_ROWS
BIAS_BASE = 1024


BIAS_HEADS_PER_STEP = 4


def _bias_kernel(base_ref, wide_ref, narrow_ref):
    qc = lax.broadcasted_iota(jnp.int32, (BIAS_Q, BIAS_K), 0) // CHUNK
    kc = lax.broadcasted_iota(jnp.int32, (BIAS_Q, BIAS_K), 1) // CHUNK
    valid = (kc >= qc) & (kc <= qc + N_LEFT_CHUNKS)
    for h in range(BIAS_HEADS_PER_STEP):
        base = base_ref[h] * LOG2E
        tiled = jnp.broadcast_to(base, (BIAS_Q, BIAS_BASE))
        toep = pltpu.roll(tiled, 0, 1, stride=1, stride_axis=0)[:, 0:BIAS_K]
        table = jnp.where(valid, toep, NEG_INF)
        wide_ref[h] = table
        narrow_ref[h] = table[0:CHUNK, 0:CHUNK + KV_ROWS]


def _bias_tables(rel_bias):
    n = rel_bias.shape[0] * N_HEADS
    f = jnp.transpose(rel_bias, (0, 2, 1)).reshape(n, N_REL)
    rev = f[:, ::-1]
    far = f[:, N_REL - 1:N_REL]
    n_left = KV_ROWS - MAX_REL
    base = jnp.concatenate(
        [jnp.broadcast_to(far, (n, n_left)), rev,
         jnp.broadcast_to(far, (n, BIAS_BASE - n_left - N_REL))], axis=1)
    base = base.reshape(n, 1, BIAS_BASE)
    hs = BIAS_HEADS_PER_STEP
    return pl.pallas_call(
        _bias_kernel,
        out_shape=(jax.ShapeDtypeStruct((n, BIAS_Q, BIAS_K), jnp.float32),
                   jax.ShapeDtypeStruct((n, CHUNK, CHUNK + KV_ROWS), jnp.float32)),
        grid=(n // hs,),
        in_specs=[pl.BlockSpec((hs, 1, BIAS_BASE), lambda h: (h, 0, 0))],
        out_specs=(pl.BlockSpec((hs, BIAS_Q, BIAS_K), lambda h: (h, 0, 0)),
                   pl.BlockSpec((hs, CHUNK, CHUNK + KV_ROWS), lambda h: (h, 0, 0))),
        compiler_params=pltpu.CompilerParams(dimension_semantics=("arbitrary",)),
        name="bias_table",
    )(base)


def _attn_layer_kernel(x_ref, kta_ref, ktb_ref, va_ref, vb_ref, bias_ref,
                       g_ref, win_ref, gq_ref, gmat_ref, wout_ref,
                       y_ref, q_ref, o_ref, z_ref, *, tm, qg, wb, mask_first, a_v_transposed):
    t = pl.program_id(1)
    n_grp = tm // qg
    cb = PIECE_COLS
    n_cb = E_B // cb
    pairs_per_cb = cb // LANES
    x = x_ref[0]
    hb = _rms(x, g_ref[...]).astype(jnp.bfloat16)
    q_gain = gq_ref[...] * (HEAD_DIM ** -0.5 * LOG2E)

    def rows_of(j):
        return slice(j * qg, (j + 1) * qg)

    def q_piece(c):
        cols = slice(c * cb, (c + 1) * cb)
        qn = _head_rms(_bdot(hb, win_ref[:, cols]), gmat_ref, q_gain[:, cols])
        for h in range(pairs_per_cb):
            q_ref[c * pairs_per_cb + h] = qn[:, h * LANES:(h + 1) * LANES].astype(jnp.bfloat16)

    def z_piece(c):
        cols = slice(c * cb, (c + 1) * cb)
        z_ref[:, cols] = _bdot(hb, win_ref[:, E_B + c * cb:E_B + (c + 1) * cb])

    def out_piece(c):
        cols = slice(c * cb, (c + 1) * cb)
        o = jnp.concatenate([o_ref[c * pairs_per_cb + h] for h in range(pairs_per_cb)],
                            axis=1)
        yb = (o * _silu(z_ref[:, cols])).astype(jnp.bfloat16)
        contrib = _bdot(yb, wout_ref[cols, :])
        if c == 0:
            y_ref[0] = x + contrib
        else:
            y_ref[0] += contrib

    lane_lo = lax.broadcasted_iota(jnp.int32, (1, LANES), 1) < HEAD_DIM
    widths = [(KV_ROWS - j * qg, (j + 1) * qg) for j in range(n_grp)]
    a_mask = jnp.where(t == 0, NEG_INF, 0.0) if mask_first else None
    contract_last = (((1,), (1,)), ((), ()))

    def scores(p, j):
        wa, wbj = widths[j]
        qb = q_ref[p, j * qg:(j + 1) * qg, :]
        zero = jnp.zeros_like(qb)
        lhs = jnp.concatenate([jnp.where(lane_lo, qb, zero),
                               jnp.where(lane_lo, zero, qb)], axis=0)
        sa = _bdot(lhs, kta_ref[0, p, :, KV_ROWS - wa:KV_ROWS].astype(jnp.bfloat16))
        if a_mask is not None:
            sa = sa + a_mask
        sb = _bdot(lhs, ktb_ref[0, p, :, 0:wbj])
        s = jnp.concatenate([sa, sb], axis=1) + bias_ref[p]
        return s, jnp.max(s, axis=1, keepdims=True)

    def attend(p, j, s_mx):
        wa, wbj = widths[j]
        s, mx = s_mx
        e = jnp.exp2(s - mx)
        l = jnp.sum(e, axis=1, keepdims=True)
        pb = e.astype(jnp.bfloat16)
        if a_v_transposed:
            va = va_ref[0, p, :, KV_ROWS - wa:KV_ROWS].astype(jnp.bfloat16)
            oa = lax.dot_general(pb[:, 0:wa], va, contract_last,
                                 preferred_element_type=jnp.float32)
        else:
            oa = _bdot(pb[:, 0:wa], va_ref[0, p, KV_ROWS - wa:KV_ROWS, :])
        o2 = oa + _bdot(pb[:, wa:wa + wbj], vb_ref[0, p, 0:wbj, :])
        o2 = o2 * (1.0 / l)
        o_ref[p, rows_of(j), :] = jnp.where(lane_lo, o2[0:qg], o2[qg:2 * qg])

    items = [(c * pairs_per_cb + h, j) for c in range(n_cb) for j in range(n_grp)
             for h in range(pairs_per_cb)]
    per_cb = len(items) // n_cb
    pieces = []
    for c in range(n_cb):
        if c > 0:
            pieces.append((per_cb * c - QK_AHEAD - 1, 0, len(pieces), q_piece, c))
        pieces.append((per_cb * c + per_cb // 2, 0, len(pieces), z_piece, c))
        if c + 1 < n_cb:
            pieces.append((per_cb * (c + 1) + per_cb // 2, per_cb * (c + 1), len(pieces),
                           out_piece, c))
    n_pieces = len(pieces)
    q_piece(0)
    pending = {}
    for i in range(min(QK_AHEAD, len(items))):
        pending[i] = scores(*items[i])
    issued = 0
    for i in range(len(items)):
        quota = ((i + 1) * n_pieces) // len(items) - issued
        ready = sorted(pc for pc in pieces if pc[1] <= i)
        take = [pc for pc in ready if pc[0] <= i]
        take += [pc for pc in ready if pc[0] > i][:max(0, quota - len(take))]
        for pc in take:
            pieces.remove(pc)
            pc[3](pc[4])
            issued += 1
        if i + QK_AHEAD < len(items):
            pending[i + QK_AHEAD] = scores(*items[i + QK_AHEAD])
        attend(*items[i], pending.pop(i))
    for pc in sorted(pieces):
        pc[3](pc[4])
    out_piece(n_cb - 1)


def _attn_layer(x, kt_a, kt_b, v_a, v_b, bias, g_norm, w_in, gq_t, gmat, w_out,
                *, layer, tm, qg, a_is_previous_tile):
    s, t_len, _ = x.shape
    n_t = t_len // tm
    wb = kt_b.shape[-1] // n_t
    const2 = lambda b, t: (0, 0)
    if a_is_previous_tile:
        assert tm == KV_ROWS
        prev = lambda b, t: jnp.maximum(t - 1, 0)
        kta_spec = pl.BlockSpec((1, N_PAIRS, LANES, KV_ROWS), lambda b, t: (b, 0, 0, prev(b, t)))
        va_spec = pl.BlockSpec((1, N_PAIRS, KV_ROWS, LANES), lambda b, t: (b, 0, prev(b, t), 0))
    else:
        kta_spec = pl.BlockSpec((1, N_PAIRS, LANES, KV_ROWS), lambda b, t: (b, 0, 0, 0))
        va_spec = pl.BlockSpec((1, N_PAIRS, LANES, KV_ROWS), lambda b, t: (b, 0, 0, 0))
    in_specs = [
        pl.BlockSpec((1, tm, D_MODEL), lambda b, t: (b, t, 0)),
        kta_spec,
        pl.BlockSpec((1, N_PAIRS, LANES, wb), lambda b, t: (b, 0, 0, t)),
        va_spec,
        pl.BlockSpec((1, N_PAIRS, tm, LANES), lambda b, t: (b, 0, t, 0)),
        _layer_spec(bias, layer),
        _layer_spec(g_norm, layer),
        _layer_spec(w_in, layer),
        _layer_spec(gq_t, layer),
        pl.BlockSpec((MXU_DIM, MXU_DIM), const2),
        _layer_spec(w_out, layer),
    ]
    return pl.pallas_call(
        functools.partial(_attn_layer_kernel, tm=tm, qg=qg, wb=wb,
                          mask_first=a_is_previous_tile,
                          a_v_transposed=not a_is_previous_tile),
        out_shape=jax.ShapeDtypeStruct(x.shape, jnp.float32),
        grid=(s, n_t),
        in_specs=in_specs,
        out_specs=pl.BlockSpec((1, tm, D_MODEL), lambda b, t: (b, t, 0)),
        scratch_shapes=[pltpu.VMEM((N_PAIRS, tm, LANES), jnp.bfloat16),
                        pltpu.VMEM((N_PAIRS, tm, LANES), jnp.float32),
                        pltpu.VMEM((tm, E_B), jnp.float32)],
        compiler_params=pltpu.CompilerParams(
            dimension_semantics=("arbitrary", "arbitrary"),
            vmem_limit_bytes=VMEM_LIMIT_BYTES),
        name="attn_layer",
    )(x, kt_a, kt_b, v_a, v_b, bias, g_norm, w_in, gq_t, gmat, w_out)


def _trunk(x, hist16, cache_k, cache_v, pos0, w, *, pool_seg, pool_len, tile):
    s, t_len, _ = x.shape
    hists = []
    for layer in range(w["w_in_a"].shape[0]):
        x, h_out = _pool_layer(x, hist16, layer, w["norm_a"], w["w_in_a"], w["w_grp_a"],
                               w["scale_a"], w["w_out_a"],
                               n_seg=pool_seg, seg_len=pool_len, pos0=pos0)
        hists.append(h_out[:, HIST_PAD - POOL_HIST:, :])
    k_new, v_new, kt, vb = _kv_proj(x, w["norm_kv"], w["w_kv"], w["gk_t"], w["gmat"], tm=tile)
    if cache_k is None:
        srcs = (kt, kt, vb, vb)
        bias = w["bias_prompt"]
        qg = BIAS_Q
    else:
        lc = cache_k.shape[1]
        as_pairs = lambda c: jnp.transpose(c, (0, 2, 3, 1)).reshape(s, N_PAIRS, LANES, lc)
        srcs = (as_pairs(cache_k), kt, as_pairs(cache_v), vb)
        bias = w["bias_sample"]
        qg = CHUNK
    for j in range(w["w_in_b"].shape[0]):
        x = _attn_layer(x, *srcs, bias, w["norm_b"], w["w_in_b"], w["gq_t"],
                        w["gmat"], w["w_out_b"], layer=j, tm=tile, qg=qg,
                        a_is_previous_tile=cache_k is None)
    return x, jnp.stack(hists, axis=0), k_new, v_new


def kernel(x_prompt, x_sample, state_pool, cache_k, cache_v, norm_a, w_in_a, w_grp_a, scale_a,
           w_out_a, norm_kv, w_kv, g_k, norm_b, w_in_b, g_q, rel_bias_b, w_out_b):
    bf = jnp.bfloat16
    head_of_lane = jnp.arange(MXU_DIM, dtype=jnp.int32) // HEAD_DIM
    gmat = (head_of_lane[:, None] == head_of_lane[None, :]).astype(bf)
    n_b = rel_bias_b.shape[0]
    wide, narrow = _bias_tables(rel_bias_b)
    pair_rows = HEADS_PER_LANE_BLOCK
    w = dict(
        norm_a=norm_a[:, None, :], w_in_a=w_in_a.astype(bf), w_grp_a=w_grp_a.astype(bf),
        scale_a=scale_a[:, None, :], w_out_a=w_out_a.astype(bf),
        norm_kv=norm_kv[None, :], w_kv=w_kv.astype(bf), gk_t=jnp.tile(g_k, N_HEADS)[None, :],
        norm_b=norm_b[:, None, :], w_in_b=w_in_b.astype(bf),
        gq_t=jnp.tile(g_q, (1, N_HEADS))[:, None, :], w_out_b=w_out_b.astype(bf), gmat=gmat,
        bias_prompt=wide.reshape(n_b, N_PAIRS, pair_rows * BIAS_Q, BIAS_K),
        bias_sample=narrow.reshape(n_b, N_PAIRS, pair_rows * CHUNK, CHUNK + KV_ROWS),
    )
    bp, sp, _ = x_prompt.shape
    bs, ss, _ = x_sample.shape
    y_p, pool_p, k_p, v_p = _trunk(x_prompt, None, None, None, 0, w,
                                   pool_seg=1, pool_len=min(POOL_TILE, sp), tile=KV_ROWS)
    hist16 = jnp.pad(state_pool, ((0, 0), (0, 0), (HIST_PAD - POOL_HIST, 0), (0, 0)))
    y_s, pool_s, k_s, v_s = _trunk(x_sample, hist16, cache_k, cache_v, PAST_LEN, w,
                                   pool_seg=bs, pool_len=ss, tile=ss)
    lp = min(KV_ROWS, sp)
    return (y_p, y_s, pool_p, pool_s,
            k_p.reshape(bp, lp, N_HEADS, HEAD_DIM), v_p.reshape(bp, lp, N_HEADS, HEAD_DIM),
            k_s.reshape(bs, ss, N_HEADS, HEAD_DIM), v_s.reshape(bs, ss, N_HEADS, HEAD_DIM))
```

```python
import functools
import math

import jax
import jax.numpy as jnp
from jax import lax
from jax.experimental import pallas as pl
from jax.experimental.pallas import tpu as pltpu

D_MODEL = 1024
E_A = 2048
POOL_WINDOWS = (2, 4, 8, 16)
N_POOL_GROUPS = len(POOL_WINDOWS)
G_A = E_A // N_POOL_GROUPS
POOL_HIST = max(POOL_WINDOWS) - 1
HEAD_DIM = 64
N_HEADS = 16
E_B = N_HEADS * HEAD_DIM
CHUNK = 64
N_LEFT_CHUNKS = 8
KV_ROWS = N_LEFT_CHUNKS * CHUNK
MAX_REL = 128
N_REL = 2 * MAX_REL + 1
EPS = 1e-6
NEG_INF = -1e30
PAST_LEN = 4096
LOG2E = math.log2(math.e)

SUBLANES = 8
LANES = 128
MXU_DIM = 256
HEADS_PER_LANE_BLOCK = LANES // HEAD_DIM
N_PAIRS = N_HEADS // HEADS_PER_LANE_BLOCK
HIST_PAD = 2 * SUBLANES
FRONT = SUBLANES
QK_AHEAD = 2
PIECE_COLS = 2 * MXU_DIM
POOL_TILE = 1024
VMEM_LIMIT_BYTES = 56 * 1024 * 1024


def _bdot(a, b):
    return jnp.dot(a, b, preferred_element_type=jnp.float32)


def _rms(x, g):
    ms = jnp.mean(x * x, axis=-1, keepdims=True)
    return x * lax.rsqrt(ms + EPS) * g


def _silu(z):
    return z * (1.0 / (1.0 + jnp.exp(-z)))


def _pool_layer_kernel(*refs, n_seg, seg_len, pos0, has_hist):
    if has_hist:
        (x_ref, hist_ref, g_ref, win_ref, wgrp_ref, scale_ref, wout_ref,
         y_ref, hout_ref, acc_ref, upad_ref, s0_ref, s1_ref, carry_ref) = refs
    else:
        (x_ref, g_ref, win_ref, wgrp_ref, scale_ref, wout_ref,
         y_ref, hout_ref, acc_ref, upad_ref, s0_ref, s1_ref, carry_ref) = refs
        hist_ref = None
    t = pl.program_id(1)
    m = n_seg * seg_len
    rows = HIST_PAD + seg_len
    x = x_ref[...].reshape(m, D_MODEL)
    hb = _rms(x, g_ref[...]).astype(jnp.bfloat16)

    @pl.when(t == 0)
    def _():
        if has_hist:
            carry_ref[...] = hist_ref[...]
        else:
            carry_ref[...] = jnp.zeros_like(carry_ref)

    zero_front = jnp.zeros((n_seg, FRONT, G_A), jnp.float32)
    upad_ref[:, 0:FRONT, :] = zero_front
    s0_ref[:, 0:FRONT, :] = zero_front
    s1_ref[:, 0:FRONT, :] = zero_front

    pos = pos0 + t * seg_len + lax.broadcasted_iota(jnp.int32, (1, seg_len, 1), 1)

    for g, w in enumerate(POOL_WINDOWS):
        lo = g * G_A
        u = _bdot(hb, win_ref[:, lo:lo + G_A])
        z = _bdot(hb, win_ref[:, E_A + lo:E_A + lo + G_A])
        u3 = u.reshape(n_seg, seg_len, G_A)
        upad_ref[:, FRONT:FRONT + HIST_PAD, :] = carry_ref[:, :, lo:lo + G_A]
        upad_ref[:, FRONT + HIST_PAD:, :] = u3
        carry_ref[:, :, lo:lo + G_A] = upad_ref[:, FRONT + seg_len:, :]
        bufs = (upad_ref, s0_ref, s1_ref)
        src = 0
        for k in range(g + 1):
            dst = 1 if src != 1 else 2
            sh = 1 << k
            bufs[dst][:, FRONT:, :] = (bufs[src][:, FRONT:, :]
                                       + bufs[src][:, FRONT - sh:FRONT - sh + rows, :])
            src = dst
        wsum = bufs[src][:, FRONT + HIST_PAD:, :]
        inv_cnt = 1.0 / jnp.minimum(pos + 1, w).astype(jnp.float32)
        pooled = (wsum * inv_cnt - u3).reshape(m, G_A)
        pg = _bdot(pooled.astype(jnp.bfloat16), wgrp_ref[g]) * scale_ref[:, lo:lo + G_A]
        yb = (pg * _silu(z)).astype(jnp.bfloat16)
        contrib = _bdot(yb, wout_ref[lo:lo + G_A, :])
        if g == 0:
            acc_ref[...] = contrib
        else:
            acc_ref[...] += contrib

    y_ref[...] = (x + acc_ref[...]).reshape(n_seg, seg_len, D_MODEL)
    hout_ref[...] = carry_ref[...]


def _layer_spec(arr, layer):
    tail = (0,) * (arr.ndim - 1)
    return pl.BlockSpec((None,) + arr.shape[1:], lambda b, t: (layer,) + tail,
                        pipeline_mode=pl.Buffered(1))


def _pool_layer(x, hist16, layer, g_norm, w_in, w_grp, scale, w_out, *, n_seg, seg_len, pos0):
    s, t_len, _ = x.shape
    has_hist = hist16 is not None
    grid = (s // n_seg, t_len // seg_len)
    in_specs = [pl.BlockSpec((n_seg, seg_len, D_MODEL), lambda b, t: (b, t, 0))]
    args = [x]
    if has_hist:
        in_specs.append(pl.BlockSpec((None, n_seg, HIST_PAD, E_A), lambda b, t: (layer, b, 0, 0)))
        args.append(hist16)
    args += [g_norm, w_in, w_grp, scale, w_out]
    in_specs += [_layer_spec(a, layer) for a in args[-5:]]
    m = n_seg * seg_len
    stage = pltpu.VMEM((n_seg, FRONT + HIST_PAD + seg_len, G_A), jnp.float32)
    return pl.pallas_call(
        functools.partial(_pool_layer_kernel, n_seg=n_seg, seg_len=seg_len, pos0=pos0,
                          has_hist=has_hist),
        out_shape=(jax.ShapeDtypeStruct(x.shape, jnp.float32),
                   jax.ShapeDtypeStruct((s, HIST_PAD, E_A), jnp.float32)),
        grid=grid,
        in_specs=in_specs,
        out_specs=(pl.BlockSpec((n_seg, seg_len, D_MODEL), lambda b, t: (b, t, 0)),
                   pl.BlockSpec((n_seg, HIST_PAD, E_A), lambda b, t: (b, 0, 0))),
        scratch_shapes=[pltpu.VMEM((m, D_MODEL), jnp.float32), stage, stage, stage,
                        pltpu.VMEM((n_seg, HIST_PAD, E_A), jnp.float32)],
        compiler_params=pltpu.CompilerParams(
            dimension_semantics=("arbitrary", "arbitrary"),
            vmem_limit_bytes=VMEM_LIMIT_BYTES),
        name="pool_layer",
    )(*args)


def _head_rms(v, gmat_ref, gain):
    sq = (v * v).astype(jnp.bfloat16)
    ss = jnp.concatenate([_bdot(sq[:, c:c + MXU_DIM], gmat_ref[...])
                          for c in range(0, v.shape[-1], MXU_DIM)], axis=1)
    return v * lax.rsqrt(ss * (1.0 / HEAD_DIM) + EPS) * gain


def _kv_kernel(x_ref, g_ref, wkv_ref, gk_ref, gmat_ref, kout_ref, vout_ref, kt_ref, vb_ref, *, tm):
    x = x_ref[0]
    hb = _rms(x, g_ref[...]).astype(jnp.bfloat16)
    k = _bdot(hb, wkv_ref[:, 0:E_B])
    kn = _head_rms(k, gmat_ref, gk_ref[...])
    v = _bdot(hb, wkv_ref[:, E_B:2 * E_B])
    kout_ref[0] = kn
    vout_ref[0] = v
    kt_w = kt_ref.shape[-1]
    for p in range(N_PAIRS):
        blk = kn[:, p * LANES:(p + 1) * LANES]
        if tm < LANES:
            blk = jnp.concatenate([blk, jnp.zeros((LANES - tm, LANES), jnp.float32)], axis=0)
        kt_ref[0, p] = blk.T[:, 0:kt_w].astype(jnp.bfloat16)
        vb_ref[0, p] = v[:, p * LANES:(p + 1) * LANES].astype(jnp.bfloat16)


def _kv_proj(x, g_kv, w_kv, gk_t, gmat, *, tm):
    s, t_len, _ = x.shape
    n_t = t_len // tm
    keep = min(KV_ROWS, t_len)
    assert keep == tm or n_t == 1
    kt_w = max(tm, LANES)
    const2 = lambda b, t: (0, 0)
    in_specs = [pl.BlockSpec((1, tm, D_MODEL), lambda b, t: (b, t, 0)),
                pl.BlockSpec((1, D_MODEL), const2),
                pl.BlockSpec((D_MODEL, 2 * E_B), const2),
                pl.BlockSpec((1, E_B), const2),
                pl.BlockSpec((MXU_DIM, MXU_DIM), const2)]
    args = [x, g_kv, w_kv, gk_t, gmat]
    out_shape = [jax.ShapeDtypeStruct((s, keep, E_B), jnp.float32),
                 jax.ShapeDtypeStruct((s, keep, E_B), jnp.float32),
                 jax.ShapeDtypeStruct((s, N_PAIRS, LANES, n_t * kt_w), jnp.bfloat16),
                 jax.ShapeDtypeStruct((s, N_PAIRS, t_len, LANES), jnp.bfloat16)]
    out_specs = [pl.BlockSpec((1, keep, E_B), lambda b, t: (b, 0, 0)),
                 pl.BlockSpec((1, keep, E_B), lambda b, t: (b, 0, 0)),
                 pl.BlockSpec((1, N_PAIRS, LANES, kt_w), lambda b, t: (b, 0, 0, t)),
                 pl.BlockSpec((1, N_PAIRS, tm, LANES), lambda b, t: (b, 0, t, 0))]
    return pl.pallas_call(
        functools.partial(_kv_kernel, tm=tm),
        out_shape=tuple(out_shape),
        grid=(s, n_t),
        in_specs=in_specs,
        out_specs=tuple(out_specs),
        compiler_params=pltpu.CompilerParams(
            dimension_semantics=("arbitrary", "arbitrary"),
            vmem_limit_bytes=VMEM_LIMIT_BYTES),
        name="kv_proj",
    )(*args)


BIAS_Q = 2 * CHUNK
BIAS_K = BIAS_Q + KV_ROWS
BIAS_BASE = 1024


BIAS_HEADS_PER_STEP = 4


def _bias_kernel(base_ref, wide_ref, narrow_ref):
    qc = lax.broadcasted_iota(jnp.int32, (BIAS_Q, BIAS_K), 0) // CHUNK
    kc = lax.broadcasted_iota(jnp.int32, (BIAS_Q, BIAS_K), 1) // CHUNK
    valid = (kc >= qc) & (kc <= qc + N_LEFT_CHUNKS)
    for h in range(BIAS_HEADS_PER_STEP):
        base = base_ref[h] * LOG2E
        tiled = jnp.broadcast_to(base, (BIAS_Q, BIAS_BASE))
        toep = pltpu.roll(tiled, 0, 1, stride=1, stride_axis=0)[:, 0:BIAS_K]
        table = jnp.where(valid, toep, NEG_INF)
        wide_ref[h] = table
        narrow_ref[h] = table[0:CHUNK, 0:CHUNK + KV_ROWS]


def _bias_tables(rel_bias):
    n = rel_bias.shape[0] * N_HEADS
    f = jnp.transpose(rel_bias, (0, 2, 1)).reshape(n, N_REL)
    rev = f[:, ::-1]
    far = f[:, N_REL - 1:N_REL]
    n_left = KV_ROWS - MAX_REL
    base = jnp.concatenate(
        [jnp.broadcast_to(far, (n, n_left)), rev,
         jnp.broadcast_to(far, (n, BIAS_BASE - n_left - N_REL))], axis=1)
    base = base.reshape(n, 1, BIAS_BASE)
    hs = BIAS_HEADS_PER_STEP
    return pl.pallas_call(
        _bias_kernel,
        out_shape=(jax.ShapeDtypeStruct((n, BIAS_Q, BIAS_K), jnp.float32),
                   jax.ShapeDtypeStruct((n, CHUNK, CHUNK + KV_ROWS), jnp.float32)),
        grid=(n // hs,),
        in_specs=[pl.BlockSpec((hs, 1, BIAS_BASE), lambda h: (h, 0, 0))],
        out_specs=(pl.BlockSpec((hs, BIAS_Q, BIAS_K), lambda h: (h, 0, 0)),
                   pl.BlockSpec((hs, CHUNK, CHUNK + KV_ROWS), lambda h: (h, 0, 0))),
        compiler_params=pltpu.CompilerParams(dimension_semantics=("arbitrary",)),
        name="bias_table",
    )(base)


def _attn_layer_kernel(*refs, tm, qg, a_mode):
    tile = functools.partial(_attn_tile, *refs, tm=tm, qg=qg, a_v_transposed=a_mode == "cache")
    if a_mode == "previous":
        first = pl.program_id(1) == 0
        pl.when(first)(functools.partial(tile, has_a=False))
        pl.when(jnp.logical_not(first))(functools.partial(tile, has_a=True))
    else:
        tile(has_a=True)


def _attn_tile(x_ref, kta_ref, ktb_ref, va_ref, vb_ref, bias_ref,
               g_ref, win_ref, gq_ref, gmat_ref, wout_ref,
               y_ref, q_ref, o_ref, z_ref, *, tm, qg, has_a, a_v_transposed):
    n_grp = tm // qg
    cb = PIECE_COLS
    n_cb = E_B // cb
    pairs_per_cb = cb // LANES
    x = x_ref[0]
    hb = _rms(x, g_ref[...]).astype(jnp.bfloat16)
    q_gain = gq_ref[...] * (HEAD_DIM ** -0.5 * LOG2E)

    def rows_of(j):
        return slice(j * qg, (j + 1) * qg)

    def q_piece(c):
        cols = slice(c * cb, (c + 1) * cb)
        qn = _head_rms(_bdot(hb, win_ref[:, cols]), gmat_ref, q_gain[:, cols])
        for h in range(pairs_per_cb):
            q_ref[c * pairs_per_cb + h] = qn[:, h * LANES:(h + 1) * LANES].astype(jnp.bfloat16)

    def z_piece(c):
        cols = slice(c * cb, (c + 1) * cb)
        z_ref[:, cols] = _bdot(hb, win_ref[:, E_B + c * cb:E_B + (c + 1) * cb])

    def out_piece(c):
        cols = slice(c * cb, (c + 1) * cb)
        o = jnp.concatenate([o_ref[c * pairs_per_cb + h] for h in range(pairs_per_cb)],
                            axis=1)
        yb = (o * _silu(z_ref[:, cols])).astype(jnp.bfloat16)
        contrib = _bdot(yb, wout_ref[cols, :])
        if c == 0:
            y_ref[0] = x + contrib
        else:
            y_ref[0] += contrib

    lane_lo = lax.broadcasted_iota(jnp.int32, (1, LANES), 1) < HEAD_DIM
    widths = [(KV_ROWS - j * qg, (j + 1) * qg) for j in range(n_grp)]
    contract_last = (((1,), (1,)), ((), ()))

    def scores(p, j):
        wa, wbj = widths[j]
        qb = q_ref[p, j * qg:(j + 1) * qg, :]
        zero = jnp.zeros_like(qb)
        lhs = jnp.concatenate([jnp.where(lane_lo, qb, zero),
                               jnp.where(lane_lo, zero, qb)], axis=0)
        s = _bdot(lhs, ktb_ref[0, p, :, 0:wbj])
        if has_a:
            sa = _bdot(lhs, kta_ref[0, p, :, KV_ROWS - wa:KV_ROWS].astype(jnp.bfloat16))
            return jnp.concatenate([sa, s], axis=1) + bias_ref[p]
        return s + bias_ref[p, :, wa:wa + wbj]

    def attend(p, j, s):
        wa, wbj = widths[j]
        mx = jnp.max(s, axis=1, keepdims=True)
        e = jnp.exp2(s - mx)
        l = jnp.sum(e, axis=1, keepdims=True)
        pb = e.astype(jnp.bfloat16)
        if not has_a:
            o2 = _bdot(pb, vb_ref[0, p, 0:wbj, :])
        else:
            if a_v_transposed:
                va = va_ref[0, p, :, KV_ROWS - wa:KV_ROWS].astype(jnp.bfloat16)
                oa = lax.dot_general(pb[:, 0:wa], va, contract_last,
                                     preferred_element_type=jnp.float32)
            else:
                oa = _bdot(pb[:, 0:wa], va_ref[0, p, KV_ROWS - wa:KV_ROWS, :])
            o2 = oa + _bdot(pb[:, wa:wa + wbj], vb_ref[0, p, 0:wbj, :])
        o2 = o2 * (1.0 / l)
        o_ref[p, rows_of(j), :] = jnp.where(lane_lo, o2[0:qg], o2[qg:2 * qg])

    items = [(c * pairs_per_cb + h, j) for c in range(n_cb) for j in range(n_grp)
             for h in range(pairs_per_cb)]
    per_cb = len(items) // n_cb
    pieces = []
    for c in range(n_cb):
        if c > 0:
            pieces.append((per_cb * c - QK_AHEAD - 1, 0, len(pieces), q_piece, c))
        pieces.append((per_cb * c + per_cb // 2, 0, len(pieces), z_piece, c))
        if c + 1 < n_cb:
            pieces.append((per_cb * (c + 1) + per_cb // 2, per_cb * (c + 1), len(pieces),
                           out_piece, c))
    n_pieces = len(pieces)
    q_piece(0)
    pending = {}
    for i in range(min(QK_AHEAD, len(items))):
        pending[i] = scores(*items[i])
    issued = 0
    for i in range(len(items)):
        quota = ((i + 1) * n_pieces) // len(items) - issued
        ready = sorted(pc for pc in pieces if pc[1] <= i)
        take = [pc for pc in ready if pc[0] <= i]
        take += [pc for pc in ready if pc[0] > i][:max(0, quota - len(take))]
        for pc in take:
            pieces.remove(pc)
            pc[3](pc[4])
            issued += 1
        if i + QK_AHEAD < len(items):
            pending[i + QK_AHEAD] = scores(*items[i + QK_AHEAD])
        attend(*items[i], pending.pop(i))
    for pc in sorted(pieces):
        pc[3](pc[4])
    out_piece(n_cb - 1)


def _attn_layer(x, kt_a, kt_b, v_a, v_b, bias, g_norm, w_in, gq_t, gmat, w_out,
                *, layer, tm, qg, a_is_previous_tile):
    s, t_len, _ = x.shape
    n_t = t_len // tm
    const2 = lambda b, t: (0, 0)
    if a_is_previous_tile:
        assert tm == KV_ROWS
        prev = lambda t: jnp.maximum(t - 1, 0)
        kta_spec = pl.BlockSpec((1, N_PAIRS, LANES, KV_ROWS), lambda b, t: (b, 0, 0, prev(t)))
        va_spec = pl.BlockSpec((1, N_PAIRS, KV_ROWS, LANES), lambda b, t: (b, 0, prev(t), 0))
    else:
        kta_spec = pl.BlockSpec((1, N_PAIRS, LANES, KV_ROWS), lambda b, t: (b, 0, 0, 0))
        va_spec = pl.BlockSpec((1, N_PAIRS, LANES, KV_ROWS), lambda b, t: (b, 0, 0, 0))
    in_specs = [
        pl.BlockSpec((1, tm, D_MODEL), lambda b, t: (b, t, 0)),
        kta_spec,
        pl.BlockSpec((1, N_PAIRS, LANES, kt_b.shape[-1] // n_t), lambda b, t: (b, 0, 0, t)),
        va_spec,
        pl.BlockSpec((1, N_PAIRS, tm, LANES), lambda b, t: (b, 0, t, 0)),
        _layer_spec(bias, layer),
        _layer_spec(g_norm, layer),
        _layer_spec(w_in, layer),
        _layer_spec(gq_t, layer),
        pl.BlockSpec((MXU_DIM, MXU_DIM), const2),
        _layer_spec(w_out, layer),
    ]
    return pl.pallas_call(
        functools.partial(_attn_layer_kernel, tm=tm, qg=qg,
                          a_mode="previous" if a_is_previous_tile else "cache"),
        out_shape=jax.ShapeDtypeStruct(x.shape, jnp.float32),
        grid=(s, n_t),
        in_specs=in_specs,
        out_specs=pl.BlockSpec((1, tm, D_MODEL), lambda b, t: (b, t, 0)),
        scratch_shapes=[pltpu.VMEM((N_PAIRS, tm, LANES), jnp.bfloat16),
                        pltpu.VMEM((N_PAIRS, tm, LANES), jnp.float32),
                        pltpu.VMEM((tm, E_B), jnp.float32)],
        compiler_params=pltpu.CompilerParams(
            dimension_semantics=("arbitrary", "arbitrary"),
            vmem_limit_bytes=VMEM_LIMIT_BYTES),
        name="attn_layer",
    )(x, kt_a, kt_b, v_a, v_b, bias, g_norm, w_in, gq_t, gmat, w_out)


def _trunk(x, hist16, cache_k, cache_v, pos0, w, *, pool_seg, pool_len, tile):
    s, t_len, _ = x.shape
    hists = []
    for layer in range(w["w_in_a"].shape[0]):
        x, h_out = _pool_layer(x, hist16, layer, w["norm_a"], w["w_in_a"], w["w_grp_a"],
                               w["scale_a"], w["w_out_a"],
                               n_seg=pool_seg, seg_len=pool_len, pos0=pos0)
        hists.append(h_out[:, HIST_PAD - POOL_HIST:, :])
    k_new, v_new, kt, vb = _kv_proj(x, w["norm_kv"], w["w_kv"], w["gk_t"], w["gmat"], tm=tile)
    if cache_k is None:
        srcs = (kt, kt, vb, vb)
        bias = w["bias_prompt"]
        qg = BIAS_Q
    else:
        lc = cache_k.shape[1]
        as_pairs = lambda c: jnp.transpose(c, (0, 2, 3, 1)).reshape(s, N_PAIRS, LANES, lc)
        srcs = (as_pairs(cache_k), kt, as_pairs(cache_v), vb)
        bias = w["bias_sample"]
        qg = CHUNK
    for j in range(w["w_in_b"].shape[0]):
        x = _attn_layer(x, *srcs, bias, w["norm_b"], w["w_in_b"], w["gq_t"],
                        w["gmat"], w["w_out_b"], layer=j, tm=tile, qg=qg,
                        a_is_previous_tile=cache_k is None)
    return x, jnp.stack(hists, axis=0), k_new, v_new


def kernel(x_prompt, x_sample, state_pool, cache_k, cache_v, norm_a, w_in_a, w_grp_a, scale_a,
           w_out_a, norm_kv, w_kv, g_k, norm_b, w_in_b, g_q, rel_bias_b, w_out_b):
    bf = jnp.bfloat16
    head_of_lane = jnp.arange(MXU_DIM, dtype=jnp.int32) // HEAD_DIM
    gmat = (head_of_lane[:, None] == head_of_lane[None, :]).astype(bf)
    n_b = rel_bias_b.shape[0]
    wide, narrow = _bias_tables(rel_bias_b)
    pair_rows = HEADS_PER_LANE_BLOCK
    w = dict(
        norm_a=norm_a[:, None, :], w_in_a=w_in_a.astype(bf), w_grp_a=w_grp_a.astype(bf),
        scale_a=scale_a[:, None, :], w_out_a=w_out_a.astype(bf),
        norm_kv=norm_kv[None, :], w_kv=w_kv.astype(bf), gk_t=jnp.tile(g_k, N_HEADS)[None, :],
        norm_b=norm_b[:, None, :], w_in_b=w_in_b.astype(bf),
        gq_t=jnp.tile(g_q, (1, N_HEADS))[:, None, :], w_out_b=w_out_b.astype(bf), gmat=gmat,
        bias_prompt=wide.reshape(n_b, N_PAIRS, pair_rows * BIAS_Q, BIAS_K),
        bias_sample=narrow.reshape(n_b, N_PAIRS, pair_rows * CHUNK, CHUNK + KV_ROWS),
    )
    bp, sp, _ = x_prompt.shape
    bs, ss, _ = x_sample.shape
    y_p, pool_p, k_p, v_p = _trunk(x_prompt, None, None, None, 0, w,
                                   pool_seg=1, pool_len=min(POOL_TILE, sp), tile=KV_ROWS)
    hist16 = jnp.pad(state_pool, ((0, 0), (0, 0), (HIST_PAD - POOL_HIST, 0), (0, 0)))
    y_s, pool_s, k_s, v_s = _trunk(x_sample, hist16, cache_k, cache_v, PAST_LEN, w,
                                   pool_seg=bs, pool_len=ss, tile=ss)
    lp = min(KV_ROWS, sp)
    return (y_p, y_s, pool_p, pool_s,
            k_p.reshape(bp, lp, N_HEADS, HEAD_DIM), v_p.reshape(bp, lp, N_HEADS, HEAD_DIM),
            k_s.reshape(bs, ss, N_HEADS, HEAD_DIM), v_s.reshape(bs, ss, N_HEADS, HEAD_DIM))
```

```python
import functools
import math

import jax
import jax.numpy as jnp
from jax import lax
from jax.experimental import pallas as pl
from jax.experimental.pallas import tpu as pltpu

D_MODEL = 1024
E_A = 2048
POOL_WINDOWS = (2, 4, 8, 16)
N_POOL_GROUPS = len(POOL_WINDOWS)
G_A = E_A // N_POOL_GROUPS
POOL_HIST = max(POOL_WINDOWS) - 1
HEAD_DIM = 64
N_HEADS = 16
E_B = N_HEADS * HEAD_DIM
CHUNK = 64
N_LEFT_CHUNKS = 8
KV_ROWS = N_LEFT_CHUNKS * CHUNK
MAX_REL = 128
N_REL = 2 * MAX_REL + 1
EPS = 1e-6
NEG_INF = -1e30
PAST_LEN = 4096
LOG2E = math.log2(math.e)

SUBLANES = 8
LANES = 128
MXU_DIM = 256
HEADS_PER_LANE_BLOCK = LANES // HEAD_DIM
N_PAIRS = N_HEADS // HEADS_PER_LANE_BLOCK
HIST_PAD = 2 * SUBLANES
FRONT = SUBLANES
QK_AHEAD = 2
PIECE_COLS = 2 * MXU_DIM
POOL_TILE = 1024
VMEM_LIMIT_BYTES = 56 * 1024 * 1024


def _bdot(a, b):
    return jnp.dot(a, b, preferred_element_type=jnp.float32)


def _rms(x, g):
    ms = jnp.mean(x * x, axis=-1, keepdims=True)
    return x * lax.rsqrt(ms + EPS) * g


def _silu(z):
    return z * (1.0 / (1.0 + jnp.exp(-z)))


def _pool_layer_kernel(*refs, n_seg, seg_len, pos0, has_hist):
    if has_hist:
        (x_ref, hist_ref, g_ref, win_ref, wgrp_ref, scale_ref, wout_ref,
         y_ref, hout_ref, acc_ref, upad_ref, s0_ref, s1_ref, carry_ref) = refs
    else:
        (x_ref, g_ref, win_ref, wgrp_ref, scale_ref, wout_ref,
         y_ref, hout_ref, acc_ref, upad_ref, s0_ref, s1_ref, carry_ref) = refs
        hist_ref = None
    t = pl.program_id(1)
    m = n_seg * seg_len
    rows = HIST_PAD + seg_len
    x = x_ref[...].reshape(m, D_MODEL)
    hb = _rms(x, g_ref[...]).astype(jnp.bfloat16)

    @pl.when(t == 0)
    def _():
        if has_hist:
            carry_ref[...] = hist_ref[...]
        else:
            carry_ref[...] = jnp.zeros_like(carry_ref)

    zero_front = jnp.zeros((n_seg, FRONT, G_A), jnp.float32)
    upad_ref[:, 0:FRONT, :] = zero_front
    s0_ref[:, 0:FRONT, :] = zero_front
    s1_ref[:, 0:FRONT, :] = zero_front

    pos = pos0 + t * seg_len + lax.broadcasted_iota(jnp.int32, (1, seg_len, 1), 1)

    for g, w in enumerate(POOL_WINDOWS):
        lo = g * G_A
        u = _bdot(hb, win_ref[:, lo:lo + G_A])
        z = _bdot(hb, win_ref[:, E_A + lo:E_A + lo + G_A])
        u3 = u.reshape(n_seg, seg_len, G_A)
        upad_ref[:, FRONT:FRONT + HIST_PAD, :] = carry_ref[:, :, lo:lo + G_A]
        upad_ref[:, FRONT + HIST_PAD:, :] = u3
        carry_ref[:, :, lo:lo + G_A] = upad_ref[:, FRONT + seg_len:, :]
        bufs = (upad_ref, s0_ref, s1_ref)
        src = 0
        for k in range(g + 1):
            dst = 1 if src != 1 else 2
            sh = 1 << k
            bufs[dst][:, FRONT:, :] = (bufs[src][:, FRONT:, :]
                                       + bufs[src][:, FRONT - sh:FRONT - sh + rows, :])
            src = dst
        wsum = bufs[src][:, FRONT + HIST_PAD:, :]
        inv_cnt = 1.0 / jnp.minimum(pos + 1, w).astype(jnp.float32)
        pooled = (wsum * inv_cnt - u3).reshape(m, G_A)
        pg = _bdot(pooled.astype(jnp.bfloat16), wgrp_ref[g]) * scale_ref[:, lo:lo + G_A]
        yb = (pg * _silu(z)).astype(jnp.bfloat16)
        contrib = _bdot(yb, wout_ref[lo:lo + G_A, :])
        if g == 0:
            acc_ref[...] = contrib
        else:
            acc_ref[...] += contrib

    y_ref[...] = (x + acc_ref[...]).reshape(n_seg, seg_len, D_MODEL)
    hout_ref[...] = carry_ref[...]


def _layer_spec(arr, layer):
    tail = (0,) * (arr.ndim - 1)
    return pl.BlockSpec((None,) + arr.shape[1:], lambda *_: (layer,) + tail,
                        pipeline_mode=pl.Buffered(1))


def _pool_layer(x, hist16, layer, g_norm, w_in, w_grp, scale, w_out, *, n_seg, seg_len, pos0):
    s, t_len, _ = x.shape
    has_hist = hist16 is not None
    grid = (s // n_seg, t_len // seg_len)
    in_specs = [pl.BlockSpec((n_seg, seg_len, D_MODEL), lambda b, t: (b, t, 0))]
    args = [x]
    if has_hist:
        in_specs.append(pl.BlockSpec((None, n_seg, HIST_PAD, E_A), lambda b, t: (layer, b, 0, 0)))
        args.append(hist16)
    args += [g_norm, w_in, w_grp, scale, w_out]
    in_specs += [_layer_spec(a, layer) for a in args[-5:]]
    m = n_seg * seg_len
    stage = pltpu.VMEM((n_seg, FRONT + HIST_PAD + seg_len, G_A), jnp.float32)
    return pl.pallas_call(
        functools.partial(_pool_layer_kernel, n_seg=n_seg, seg_len=seg_len, pos0=pos0,
                          has_hist=has_hist),
        out_shape=(jax.ShapeDtypeStruct(x.shape, jnp.float32),
                   jax.ShapeDtypeStruct((s, HIST_PAD, E_A), jnp.float32)),
        grid=grid,
        in_specs=in_specs,
        out_specs=(pl.BlockSpec((n_seg, seg_len, D_MODEL), lambda b, t: (b, t, 0)),
                   pl.BlockSpec((n_seg, HIST_PAD, E_A), lambda b, t: (b, 0, 0))),
        scratch_shapes=[pltpu.VMEM((m, D_MODEL), jnp.float32), stage, stage, stage,
                        pltpu.VMEM((n_seg, HIST_PAD, E_A), jnp.float32)],
        compiler_params=pltpu.CompilerParams(
            dimension_semantics=("arbitrary", "arbitrary"),
            vmem_limit_bytes=VMEM_LIMIT_BYTES),
        name="pool_layer",
    )(*args)


def _head_rms(v, gmat_ref, gain):
    sq = (v * v).astype(jnp.bfloat16)
    ss = jnp.concatenate([_bdot(sq[:, c:c + MXU_DIM], gmat_ref[...])
                          for c in range(0, v.shape[-1], MXU_DIM)], axis=1)
    return v * lax.rsqrt(ss * (1.0 / HEAD_DIM) + EPS) * gain


def _kv_kernel(x_ref, g_ref, wkv_ref, gk_ref, gmat_ref, kout_ref, vout_ref, kt_ref, vb_ref, *, tm):
    x = x_ref[0]
    hb = _rms(x, g_ref[...]).astype(jnp.bfloat16)
    k = _bdot(hb, wkv_ref[:, 0:E_B])
    kn = _head_rms(k, gmat_ref, gk_ref[...])
    v = _bdot(hb, wkv_ref[:, E_B:2 * E_B])
    kout_ref[0] = kn
    vout_ref[0] = v
    kt_w = kt_ref.shape[-1]
    for p in range(N_PAIRS):
        blk = kn[:, p * LANES:(p + 1) * LANES]
        if tm < LANES:
            blk = jnp.concatenate([blk, jnp.zeros((LANES - tm, LANES), jnp.float32)], axis=0)
        kt_ref[0, p] = blk.T[:, 0:kt_w].astype(jnp.bfloat16)
        vb_ref[0, p] = v[:, p * LANES:(p + 1) * LANES].astype(jnp.bfloat16)


def _kv_proj(x, g_kv, w_kv, gk_t, gmat, *, tm):
    s, t_len, _ = x.shape
    n_t = t_len // tm
    keep = min(KV_ROWS, t_len)
    assert keep == tm or n_t == 1
    kt_w = max(tm, LANES)
    const2 = lambda b, t: (0, 0)
    in_specs = [pl.BlockSpec((1, tm, D_MODEL), lambda b, t: (b, t, 0)),
                pl.BlockSpec((1, D_MODEL), const2),
                pl.BlockSpec((D_MODEL, 2 * E_B), const2),
                pl.BlockSpec((1, E_B), const2),
                pl.BlockSpec((MXU_DIM, MXU_DIM), const2)]
    args = [x, g_kv, w_kv, gk_t, gmat]
    out_shape = [jax.ShapeDtypeStruct((s, keep, E_B), jnp.float32),
                 jax.ShapeDtypeStruct((s, keep, E_B), jnp.float32),
                 jax.ShapeDtypeStruct((s, N_PAIRS, LANES, n_t * kt_w), jnp.bfloat16),
                 jax.ShapeDtypeStruct((s, N_PAIRS, t_len, LANES), jnp.bfloat16)]
    out_specs = [pl.BlockSpec((1, keep, E_B), lambda b, t: (b, 0, 0)),
                 pl.BlockSpec((1, keep, E_B), lambda b, t: (b, 0, 0)),
                 pl.BlockSpec((1, N_PAIRS, LANES, kt_w), lambda b, t: (b, 0, 0, t)),
                 pl.BlockSpec((1, N_PAIRS, tm, LANES), lambda b, t: (b, 0, t, 0))]
    return pl.pallas_call(
        functools.partial(_kv_kernel, tm=tm),
        out_shape=tuple(out_shape),
        grid=(s, n_t),
        in_specs=in_specs,
        out_specs=tuple(out_specs),
        compiler_params=pltpu.CompilerParams(
            dimension_semantics=("arbitrary", "arbitrary"),
            vmem_limit_bytes=VMEM_LIMIT_BYTES),
        name="kv_proj",
    )(*args)


BIAS_Q = 2 * CHUNK
BIAS_K = BIAS_Q + KV_ROWS
BIAS_BASE = 1024


BIAS_HEADS_PER_STEP = 4


def _bias_kernel(base_ref, wide_ref, narrow_ref):
    qc = lax.broadcasted_iota(jnp.int32, (BIAS_Q, BIAS_K), 0) // CHUNK
    kc = lax.broadcasted_iota(jnp.int32, (BIAS_Q, BIAS_K), 1) // CHUNK
    valid = (kc >= qc) & (kc <= qc + N_LEFT_CHUNKS)
    for h in range(BIAS_HEADS_PER_STEP):
        base = base_ref[h] * LOG2E
        tiled = jnp.broadcast_to(base, (BIAS_Q, BIAS_BASE))
        toep = pltpu.roll(tiled, 0, 1, stride=1, stride_axis=0)[:, 0:BIAS_K]
        table = jnp.where(valid, toep, NEG_INF)
        wide_ref[h] = table
        narrow_ref[h] = table[0:CHUNK, 0:CHUNK + KV_ROWS]


def _bias_tables(rel_bias):
    n = rel_bias.shape[0] * N_HEADS
    f = jnp.transpose(rel_bias, (0, 2, 1)).reshape(n, N_REL)
    rev = f[:, ::-1]
    far = f[:, N_REL - 1:N_REL]
    n_left = KV_ROWS - MAX_REL
    base = jnp.concatenate(
        [jnp.broadcast_to(far, (n, n_left)), rev,
         jnp.broadcast_to(far, (n, BIAS_BASE - n_left - N_REL))], axis=1)
    base = base.reshape(n, 1, BIAS_BASE)
    hs = BIAS_HEADS_PER_STEP
    return pl.pallas_call(
        _bias_kernel,
        out_shape=(jax.ShapeDtypeStruct((n, BIAS_Q, BIAS_K), jnp.float32),
                   jax.ShapeDtypeStruct((n, CHUNK, CHUNK + KV_ROWS), jnp.float32)),
        grid=(n // hs,),
        in_specs=[pl.BlockSpec((hs, 1, BIAS_BASE), lambda h: (h, 0, 0))],
        out_specs=(pl.BlockSpec((hs, BIAS_Q, BIAS_K), lambda h: (h, 0, 0)),
                   pl.BlockSpec((hs, CHUNK, CHUNK + KV_ROWS), lambda h: (h, 0, 0))),
        compiler_params=pltpu.CompilerParams(dimension_semantics=("arbitrary",)),
        name="bias_table",
    )(base)


def _attn_layer_kernel(*refs, tm, qg):
    tile = functools.partial(_attn_tile, *refs, tm=tm, qg=qg)
    first = pl.program_id(1) == 0
    pl.when(first)(functools.partial(tile, has_a=False))
    pl.when(jnp.logical_not(first))(functools.partial(tile, has_a=True))


def _attn_tile(x_ref, kta_ref, ktb_ref, va_ref, vb_ref, bias_ref,
               g_ref, win_ref, gq_ref, gmat_ref, wout_ref,
               y_ref, q_ref, o_ref, z_ref, *, tm, qg, has_a):
    n_grp = tm // qg
    cb = PIECE_COLS
    n_cb = E_B // cb
    pairs_per_cb = cb // LANES
    x = x_ref[0]
    hb = _rms(x, g_ref[...]).astype(jnp.bfloat16)
    q_gain = gq_ref[...] * (HEAD_DIM ** -0.5 * LOG2E)

    def rows_of(j):
        return slice(j * qg, (j + 1) * qg)

    def q_piece(c):
        cols = slice(c * cb, (c + 1) * cb)
        qn = _head_rms(_bdot(hb, win_ref[:, cols]), gmat_ref, q_gain[:, cols])
        for h in range(pairs_per_cb):
            q_ref[c * pairs_per_cb + h] = qn[:, h * LANES:(h + 1) * LANES].astype(jnp.bfloat16)

    def z_piece(c):
        cols = slice(c * cb, (c + 1) * cb)
        z_ref[:, cols] = _bdot(hb, win_ref[:, E_B + c * cb:E_B + (c + 1) * cb])

    def out_piece(c):
        cols = slice(c * cb, (c + 1) * cb)
        o = jnp.concatenate([o_ref[c * pairs_per_cb + h] for h in range(pairs_per_cb)],
                            axis=1)
        yb = (o * _silu(z_ref[:, cols])).astype(jnp.bfloat16)
        contrib = _bdot(yb, wout_ref[cols, :])
        if c == 0:
            y_ref[0] = x + contrib
        else:
            y_ref[0] += contrib

    lane_lo = lax.broadcasted_iota(jnp.int32, (1, LANES), 1) < HEAD_DIM
    widths = [(KV_ROWS - j * qg, (j + 1) * qg) for j in range(n_grp)]

    def scores(p, j):
        wa, wbj = widths[j]
        qb = q_ref[p, j * qg:(j + 1) * qg, :]
        zero = jnp.zeros_like(qb)
        lhs = jnp.concatenate([jnp.where(lane_lo, qb, zero),
                               jnp.where(lane_lo, zero, qb)], axis=0)
        s = _bdot(lhs, ktb_ref[0, p, :, 0:wbj])
        if has_a:
            sa = _bdot(lhs, kta_ref[0, p, :, KV_ROWS - wa:KV_ROWS])
            return jnp.concatenate([sa, s], axis=1) + bias_ref[p]
        return s + bias_ref[p, :, wa:wa + wbj]

    def attend(p, j, s):
        wa, wbj = widths[j]
        mx = jnp.max(s, axis=1, keepdims=True)
        e = jnp.exp2(s - mx)
        l = jnp.sum(e, axis=1, keepdims=True)
        pb = e.astype(jnp.bfloat16)
        if has_a:
            o2 = (_bdot(pb[:, 0:wa], va_ref[0, p, KV_ROWS - wa:KV_ROWS, :])
                  + _bdot(pb[:, wa:wa + wbj], vb_ref[0, p, 0:wbj, :]))
        else:
            o2 = _bdot(pb, vb_ref[0, p, 0:wbj, :])
        o2 = o2 * (1.0 / l)
        o_ref[p, rows_of(j), :] = jnp.where(lane_lo, o2[0:qg], o2[qg:2 * qg])

    items = [(c * pairs_per_cb + h, j) for c in range(n_cb) for j in range(n_grp)
             for h in range(pairs_per_cb)]
    per_cb = len(items) // n_cb
    pieces = []
    for c in range(n_cb):
        if c > 0:
            pieces.append((per_cb * c - QK_AHEAD - 1, 0, len(pieces), q_piece, c))
        pieces.append((per_cb * c + per_cb // 2, 0, len(pieces), z_piece, c))
        if c + 1 < n_cb:
            pieces.append((per_cb * (c + 1) + per_cb // 2, per_cb * (c + 1), len(pieces),
                           out_piece, c))
    n_pieces = len(pieces)
    q_piece(0)
    pending = {}
    for i in range(min(QK_AHEAD, len(items))):
        pending[i] = scores(*items[i])
    issued = 0
    for i in range(len(items)):
        quota = ((i + 1) * n_pieces) // len(items) - issued
        ready = sorted(pc for pc in pieces if pc[1] <= i)
        take = [pc for pc in ready if pc[0] <= i]
        take += [pc for pc in ready if pc[0] > i][:max(0, quota - len(take))]
        for pc in take:
            pieces.remove(pc)
            pc[3](pc[4])
            issued += 1
        if i + QK_AHEAD < len(items):
            pending[i + QK_AHEAD] = scores(*items[i + QK_AHEAD])
        attend(*items[i], pending.pop(i))
    for pc in sorted(pieces):
        pc[3](pc[4])
    out_piece(n_cb - 1)


def _attn_layer(x, kt, v, bias, g_norm, w_in, gq_t, gmat, w_out, *, layer, tm, qg):
    s, t_len, _ = x.shape
    n_t = t_len // tm
    const2 = lambda b, t: (0, 0)
    assert tm == KV_ROWS and kt.shape[-1] == t_len
    prev = lambda t: jnp.maximum(t - 1, 0)
    in_specs = [
        pl.BlockSpec((1, tm, D_MODEL), lambda b, t: (b, t, 0)),
        pl.BlockSpec((1, N_PAIRS, LANES, KV_ROWS), lambda b, t: (b, 0, 0, prev(t))),
        pl.BlockSpec((1, N_PAIRS, LANES, tm), lambda b, t: (b, 0, 0, t)),
        pl.BlockSpec((1, N_PAIRS, KV_ROWS, LANES), lambda b, t: (b, 0, prev(t), 0)),
        pl.BlockSpec((1, N_PAIRS, tm, LANES), lambda b, t: (b, 0, t, 0)),
        _layer_spec(bias, layer),
        _layer_spec(g_norm, layer),
        _layer_spec(w_in, layer),
        _layer_spec(gq_t, layer),
        pl.BlockSpec((MXU_DIM, MXU_DIM), const2),
        _layer_spec(w_out, layer),
    ]
    return pl.pallas_call(
        functools.partial(_attn_layer_kernel, tm=tm, qg=qg),
        out_shape=jax.ShapeDtypeStruct(x.shape, jnp.float32),
        grid=(s, n_t),
        in_specs=in_specs,
        out_specs=pl.BlockSpec((1, tm, D_MODEL), lambda b, t: (b, t, 0)),
        scratch_shapes=[pltpu.VMEM((N_PAIRS, tm, LANES), jnp.bfloat16),
                        pltpu.VMEM((N_PAIRS, tm, LANES), jnp.float32),
                        pltpu.VMEM((tm, E_B), jnp.float32)],
        compiler_params=pltpu.CompilerParams(
            dimension_semantics=("arbitrary", "arbitrary"),
            vmem_limit_bytes=VMEM_LIMIT_BYTES),
        name="attn_layer",
    )(x, kt, kt, v, v, bias, g_norm, w_in, gq_t, gmat, w_out)


def _attn_cached_kernel(x_ref, ktc_ref, ktn_ref, vtc_ref, vn_ref, bias_ref,
                        g_ref, win_ref, gq_ref, gmat_ref, wout_ref,
                        y_ref, q_ref, o_ref, z_ref, *, n_seq, qg):
    b = pl.program_id(0)
    lane_lo = lax.broadcasted_iota(jnp.int32, (1, LANES), 1) < HEAD_DIM
    contract_last = (((1,), (1,)), ((), ()))

    @pl.when(b == 0)
    def _():
        hb = _rms(x_ref[...], g_ref[...]).astype(jnp.bfloat16)
        q_gain = gq_ref[...] * (HEAD_DIM ** -0.5 * LOG2E)
        qn = _head_rms(_bdot(hb, win_ref[:, 0:E_B]), gmat_ref, q_gain)
        for p in range(N_PAIRS):
            q_ref[p] = qn[:, p * LANES:(p + 1) * LANES].astype(jnp.bfloat16)
        z_ref[...] = _bdot(hb, win_ref[:, E_B:2 * E_B])

    rows = pl.ds(pl.multiple_of(b * qg, qg), qg)

    def scores(p):
        qb = q_ref[p, rows, :]
        zero = jnp.zeros_like(qb)
        lhs = jnp.concatenate([jnp.where(lane_lo, qb, zero),
                               jnp.where(lane_lo, zero, qb)], axis=0)
        sa = _bdot(lhs, ktc_ref[0, p].astype(jnp.bfloat16))
        sb = _bdot(lhs, ktn_ref[0, p, :, 0:qg])
        return jnp.concatenate([sa, sb], axis=1) + bias_ref[p]

    def attend(p, s):
        keys = vtc_ref.shape[-1]
        mx = jnp.max(s, axis=1, keepdims=True)
        e = jnp.exp2(s - mx)
        l = jnp.sum(e, axis=1, keepdims=True)
        pb = e.astype(jnp.bfloat16)
        oa = lax.dot_general(pb[:, 0:keys], vtc_ref[0, p].astype(jnp.bfloat16), contract_last,
                             preferred_element_type=jnp.float32)
        o2 = (oa + _bdot(pb[:, keys:keys + qg], vn_ref[0, p])) * (1.0 / l)
        o_ref[p, rows, :] = jnp.where(lane_lo, o2[0:qg], o2[qg:2 * qg])

    pending = {p: scores(p) for p in range(min(QK_AHEAD, N_PAIRS))}
    for p in range(N_PAIRS):
        if p + QK_AHEAD < N_PAIRS:
            pending[p + QK_AHEAD] = scores(p + QK_AHEAD)
        attend(p, pending.pop(p))

    @pl.when(b == n_seq - 1)
    def _():
        o = jnp.concatenate([o_ref[p] for p in range(N_PAIRS)], axis=1)
        yb = (o * _silu(z_ref[...])).astype(jnp.bfloat16)
        y_ref[...] = x_ref[...] + _bdot(yb, wout_ref[...])


def _attn_layer_cached(x, kt_cache, kt_new, vt_cache, v_new, bias, g_norm, w_in, gq_t, gmat,
                       w_out, *, layer):
    s, qg, _ = x.shape
    m = s * qg
    keys = kt_cache.shape[-1]
    whole = pl.BlockSpec((m, D_MODEL), lambda b: (0, 0))
    per_seq = lambda a: pl.BlockSpec((1,) + a.shape[1:], lambda b: (b, 0, 0, 0))
    in_specs = [whole, per_seq(kt_cache), per_seq(kt_new), per_seq(vt_cache), per_seq(v_new),
                _layer_spec(bias, layer), _layer_spec(g_norm, layer), _layer_spec(w_in, layer),
                _layer_spec(gq_t, layer), pl.BlockSpec((MXU_DIM, MXU_DIM), lambda b: (0, 0)),
                _layer_spec(w_out, layer)]
    assert bias.shape[-1] == keys + qg
    y = pl.pallas_call(
        functools.partial(_attn_cached_kernel, n_seq=s, qg=qg),
        out_shape=jax.ShapeDtypeStruct((m, D_MODEL), jnp.float32),
        grid=(s,),
        in_specs=in_specs,
        out_specs=whole,
        scratch_shapes=[pltpu.VMEM((N_PAIRS, m, LANES), jnp.bfloat16),
                        pltpu.VMEM((N_PAIRS, m, LANES), jnp.float32),
                        pltpu.VMEM((m, E_B), jnp.float32)],
        compiler_params=pltpu.CompilerParams(
            dimension_semantics=("arbitrary",), vmem_limit_bytes=VMEM_LIMIT_BYTES),
        name="attn_cached",
    )(x.reshape(m, D_MODEL), kt_cache, kt_new, vt_cache, v_new, bias, g_norm, w_in, gq_t, gmat,
      w_out)
    return y.reshape(x.shape)


def _trunk(x, hist16, cache_k, cache_v, pos0, w, *, pool_seg, pool_len, tile):
    s, t_len, _ = x.shape
    hists = []
    for layer in range(w["w_in_a"].shape[0]):
        x, h_out = _pool_layer(x, hist16, layer, w["norm_a"], w["w_in_a"], w["w_grp_a"],
                               w["scale_a"], w["w_out_a"],
                               n_seg=pool_seg, seg_len=pool_len, pos0=pos0)
        hists.append(h_out[:, HIST_PAD - POOL_HIST:, :])
    k_new, v_new, kt, vb = _kv_proj(x, w["norm_kv"], w["w_kv"], w["gk_t"], w["gmat"], tm=tile)
    weights = (w["norm_b"], w["w_in_b"], w["gq_t"], w["gmat"], w["w_out_b"])
    if cache_k is None:
        for j in range(w["w_in_b"].shape[0]):
            x = _attn_layer(x, kt, vb, w["bias_prompt"], *weights, layer=j, tm=tile, qg=BIAS_Q)
    else:
        lc = cache_k.shape[1]
        as_pairs = lambda c: jnp.transpose(c, (0, 2, 3, 1)).reshape(s, N_PAIRS, LANES, lc)
        for j in range(w["w_in_b"].shape[0]):
            x = _attn_layer_cached(x, as_pairs(cache_k), kt, as_pairs(cache_v), vb,
                                   w["bias_sample"], *weights, layer=j)
    return x, jnp.stack(hists, axis=0), k_new, v_new


def kernel(x_prompt, x_sample, state_pool, cache_k, cache_v, norm_a, w_in_a, w_grp_a, scale_a,
           w_out_a, norm_kv, w_kv, g_k, norm_b, w_in_b, g_q, rel_bias_b, w_out_b):
    bf = jnp.bfloat16
    head_of_lane = jnp.arange(MXU_DIM, dtype=jnp.int32) // HEAD_DIM
    gmat = (head_of_lane[:, None] == head_of_lane[None, :]).astype(bf)
    n_b = rel_bias_b.shape[0]
    wide, narrow = _bias_tables(rel_bias_b)
    pair_rows = HEADS_PER_LANE_BLOCK
    w = dict(
        norm_a=norm_a[:, None, :], w_in_a=w_in_a.astype(bf), w_grp_a=w_grp_a.astype(bf),
        scale_a=scale_a[:, None, :], w_out_a=w_out_a.astype(bf),
        norm_kv=norm_kv[None, :], w_kv=w_kv.astype(bf), gk_t=jnp.tile(g_k, N_HEADS)[None, :],
        norm_b=norm_b[:, None, :], w_in_b=w_in_b.astype(bf),
        gq_t=jnp.tile(g_q, (1, N_HEADS))[:, None, :], w_out_b=w_out_b.astype(bf), gmat=gmat,
        bias_prompt=wide.reshape(n_b, N_PAIRS, pair_rows * BIAS_Q, BIAS_K),
        bias_sample=narrow.reshape(n_b, N_PAIRS, pair_rows * CHUNK, CHUNK + KV_ROWS),
    )
    bp, sp, _ = x_prompt.shape
    bs, ss, _ = x_sample.shape
    y_p, pool_p, k_p, v_p = _trunk(x_prompt, None, None, None, 0, w,
                                   pool_seg=1, pool_len=min(POOL_TILE, sp), tile=KV_ROWS)
    hist16 = jnp.pad(state_pool, ((0, 0), (0, 0), (HIST_PAD - POOL_HIST, 0), (0, 0)))
    y_s, pool_s, k_s, v_s = _trunk(x_sample, hist16, cache_k, cache_v, PAST_LEN, w,
                                   pool_seg=bs, pool_len=ss, tile=ss)
    lp = min(KV_ROWS, sp)
    return (y_p, y_s, pool_p, pool_s,
            k_p.reshape(bp, lp, N_HEADS, HEAD_DIM), v_p.reshape(bp, lp, N_HEADS, HEAD_DIM),
            k_s.reshape(bs, ss, N_HEADS, HEAD_DIM), v_s.reshape(bs, ss, N_HEADS, HEAD_DIM))
```

```python
import functools
import math

import jax
import jax.numpy as jnp
from jax import lax
from jax.experimental import pallas as pl
from jax.experimental.pallas import tpu as pltpu

D_MODEL = 1024
E_A = 2048
POOL_WINDOWS = (2, 4, 8, 16)
N_POOL_GROUPS = len(POOL_WINDOWS)
G_A = E_A // N_POOL_GROUPS
POOL_HIST = max(POOL_WINDOWS) - 1
HEAD_DIM = 64
N_HEADS = 16
E_B = N_HEADS * HEAD_DIM
CHUNK = 64
N_LEFT_CHUNKS = 8
KV_ROWS = N_LEFT_CHUNKS * CHUNK
MAX_REL = 128
N_REL = 2 * MAX_REL + 1
EPS = 1e-6
NEG_INF = -1e30
PAST_LEN = 4096
LOG2E = math.log2(math.e)

SUBLANES = 8
LANES = 128
MXU_DIM = 256
HEADS_PER_LANE_BLOCK = LANES // HEAD_DIM
N_PAIRS = N_HEADS // HEADS_PER_LANE_BLOCK
HIST_PAD = 2 * SUBLANES
FRONT = SUBLANES
QK_AHEAD = 2
PIECE_COLS = 2 * MXU_DIM
POOL_TILE = 1024
VMEM_LIMIT_BYTES = 56 * 1024 * 1024


def _bdot(a, b):
    return jnp.dot(a, b, preferred_element_type=jnp.float32)


def _rms(x, g):
    ms = jnp.mean(x * x, axis=-1, keepdims=True)
    return x * lax.rsqrt(ms + EPS) * g


def _silu(z):
    return z * (1.0 / (1.0 + jnp.exp(-z)))


def _side_cast_specs(arrays, n_steps, step_of):
    in_specs, out_specs, out_shapes = [], [], []
    for arr, lead in arrays:
        n_rows, n_cols = arr.shape[-2:]
        rows = n_rows // n_steps
        assert rows * n_steps == n_rows and rows % (2 * SUBLANES) == 0
        if lead is None:
            in_specs.append(pl.BlockSpec((rows, n_cols), lambda b, t: (step_of(b, t), 0)))
        else:
            in_specs.append(pl.BlockSpec((None, rows, n_cols),
                                         lambda b, t, lead=lead: (lead, step_of(b, t), 0)))
        out_specs.append(pl.BlockSpec((rows, n_cols), lambda b, t: (step_of(b, t), 0)))
        out_shapes.append(jax.ShapeDtypeStruct((n_rows, n_cols), jnp.bfloat16))
    return in_specs, out_specs, out_shapes


def _pool_layer_kernel(*refs, n_seg, seg_len, pos0, has_hist, n_side):
    it = iter(refs)
    x_ref = next(it)
    hist_ref = next(it) if has_hist else None
    g_ref, win_ref, wgrp_ref, scale_ref, wout_ref = (next(it) for _ in range(5))
    side_in = [next(it) for _ in range(n_side)]
    y_ref, hout_ref = next(it), next(it)
    side_out = [next(it) for _ in range(n_side)]
    acc_ref, upad_ref, s0_ref, s1_ref, carry_ref = it
    for src, dst in zip(side_in, side_out):
        dst[...] = src[...].astype(dst.dtype)
    t = pl.program_id(1)
    m = n_seg * seg_len
    rows = HIST_PAD + seg_len
    x = x_ref[...].reshape(m, D_MODEL)
    hb = _rms(x, g_ref[...]).astype(jnp.bfloat16)

    @pl.when(t == 0)
    def _():
        if has_hist:
            carry_ref[...] = hist_ref[...]
        else:
            carry_ref[...] = jnp.zeros_like(carry_ref)

    zero_front = jnp.zeros((n_seg, FRONT, G_A), jnp.float32)
    upad_ref[:, 0:FRONT, :] = zero_front
    s0_ref[:, 0:FRONT, :] = zero_front
    s1_ref[:, 0:FRONT, :] = zero_front

    pos = pos0 + t * seg_len + lax.broadcasted_iota(jnp.int32, (1, seg_len, 1), 1)

    for g, w in enumerate(POOL_WINDOWS):
        lo = g * G_A
        u = _bdot(hb, win_ref[:, lo:lo + G_A])
        z = _bdot(hb, win_ref[:, E_A + lo:E_A + lo + G_A])
        u3 = u.reshape(n_seg, seg_len, G_A)
        upad_ref[:, FRONT:FRONT + HIST_PAD, :] = carry_ref[:, :, lo:lo + G_A]
        upad_ref[:, FRONT + HIST_PAD:, :] = u3
        carry_ref[:, :, lo:lo + G_A] = upad_ref[:, FRONT + seg_len:, :]
        bufs = (upad_ref, s0_ref, s1_ref)
        src = 0
        for k in range(g + 1):
            dst = 1 if src != 1 else 2
            sh = 1 << k
            bufs[dst][:, FRONT:, :] = (bufs[src][:, FRONT:, :]
                                       + bufs[src][:, FRONT - sh:FRONT - sh + rows, :])
            src = dst
        wsum = bufs[src][:, FRONT + HIST_PAD:, :]
        inv_cnt = 1.0 / jnp.minimum(pos + 1, w).astype(jnp.float32)
        pooled = (wsum * inv_cnt - u3).reshape(m, G_A)
        pg = _bdot(pooled.astype(jnp.bfloat16), wgrp_ref[g]) * scale_ref[:, lo:lo + G_A]
        yb = (pg * _silu(z)).astype(jnp.bfloat16)
        contrib = _bdot(yb, wout_ref[lo:lo + G_A, :])
        if g == 0:
            acc_ref[...] = contrib
        else:
            acc_ref[...] += contrib

    y_ref[...] = (x + acc_ref[...]).reshape(n_seg, seg_len, D_MODEL)
    hout_ref[...] = carry_ref[...]


def _layer_spec(arr, layer):
    tail = (0,) * (arr.ndim - 1)
    return pl.BlockSpec((None,) + arr.shape[1:], lambda *_: (layer,) + tail,
                        pipeline_mode=pl.Buffered(1))


def _pool_layer(x, hist16, layer, g_norm, w_in, w_grp, scale, w_out, *, n_seg, seg_len, pos0,
                side_casts=()):
    s, t_len, _ = x.shape
    has_hist = hist16 is not None
    grid = (s // n_seg, t_len // seg_len)
    in_specs = [pl.BlockSpec((n_seg, seg_len, D_MODEL), lambda b, t: (b, t, 0))]
    args = [x]
    if has_hist:
        in_specs.append(pl.BlockSpec((None, n_seg, HIST_PAD, E_A), lambda b, t: (layer, b, 0, 0)))
        args.append(hist16)
    for arr, idx in (g_norm, w_in, w_grp, scale, w_out):
        args.append(arr)
        in_specs.append(_layer_spec(arr, idx))
    side_in, side_out, side_shapes = _side_cast_specs(
        side_casts, grid[0] * grid[1], lambda b, t: b * grid[1] + t)
    args += [arr for arr, _ in side_casts]
    in_specs += side_in
    m = n_seg * seg_len
    stage = pltpu.VMEM((n_seg, FRONT + HIST_PAD + seg_len, G_A), jnp.float32)
    return pl.pallas_call(
        functools.partial(_pool_layer_kernel, n_seg=n_seg, seg_len=seg_len, pos0=pos0,
                          has_hist=has_hist, n_side=len(side_casts)),
        out_shape=(jax.ShapeDtypeStruct(x.shape, jnp.float32),
                   jax.ShapeDtypeStruct((s, HIST_PAD, E_A), jnp.float32), *side_shapes),
        grid=grid,
        in_specs=in_specs,
        out_specs=(pl.BlockSpec((n_seg, seg_len, D_MODEL), lambda b, t: (b, t, 0)),
                   pl.BlockSpec((n_seg, HIST_PAD, E_A), lambda b, t: (b, 0, 0)), *side_out),
        scratch_shapes=[pltpu.VMEM((m, D_MODEL), jnp.float32), stage, stage, stage,
                        pltpu.VMEM((n_seg, HIST_PAD, E_A), jnp.float32)],
        compiler_params=pltpu.CompilerParams(
            dimension_semantics=("arbitrary", "arbitrary"),
            vmem_limit_bytes=VMEM_LIMIT_BYTES),
        name="pool_layer",
    )(*args)


def _head_rms(v, gmat_ref, gain):
    sq = (v * v).astype(jnp.bfloat16)
    ss = jnp.concatenate([_bdot(sq[:, c:c + MXU_DIM], gmat_ref[...])
                          for c in range(0, v.shape[-1], MXU_DIM)], axis=1)
    return v * lax.rsqrt(ss * (1.0 / HEAD_DIM) + EPS) * gain


def _kv_kernel(*refs, tm, n_side):
    it = iter(refs)
    x_ref, g_ref, wkv_ref, gk_ref, gmat_ref = (next(it) for _ in range(5))
    side_in = [next(it) for _ in range(n_side)]
    kout_ref, vout_ref, kt_ref, vb_ref = (next(it) for _ in range(4))
    for src, dst in zip(side_in, it):
        dst[...] = src[...].astype(dst.dtype)
    x = x_ref[0]
    hb = _rms(x, g_ref[...]).astype(jnp.bfloat16)
    k = _bdot(hb, wkv_ref[:, 0:E_B])
    kn = _head_rms(k, gmat_ref, gk_ref[...])
    v = _bdot(hb, wkv_ref[:, E_B:2 * E_B])
    kout_ref[0] = kn
    vout_ref[0] = v
    kt_w = kt_ref.shape[-1]
    for p in range(N_PAIRS):
        blk = kn[:, p * LANES:(p + 1) * LANES]
        if tm < LANES:
            blk = jnp.concatenate([blk, jnp.zeros((LANES - tm, LANES), jnp.float32)], axis=0)
        kt_ref[0, p] = blk.T[:, 0:kt_w].astype(jnp.bfloat16)
        vb_ref[0, p] = v[:, p * LANES:(p + 1) * LANES].astype(jnp.bfloat16)


def _kv_proj(x, g_kv, w_kv, gk_t, gmat, *, tm, side_casts=()):
    s, t_len, _ = x.shape
    n_t = t_len // tm
    keep = min(KV_ROWS, t_len)
    assert keep == tm or n_t == 1
    kt_w = max(tm, LANES)
    const2 = lambda b, t: (0, 0)
    in_specs = [pl.BlockSpec((1, tm, D_MODEL), lambda b, t: (b, t, 0)),
                pl.BlockSpec((1, D_MODEL), const2),
                pl.BlockSpec((D_MODEL, 2 * E_B), const2),
                pl.BlockSpec((1, E_B), const2),
                pl.BlockSpec((MXU_DIM, MXU_DIM), const2)]
    args = [x, g_kv, w_kv, gk_t, gmat]
    out_shape = [jax.ShapeDtypeStruct((s, keep, E_B), jnp.float32),
                 jax.ShapeDtypeStruct((s, keep, E_B), jnp.float32),
                 jax.ShapeDtypeStruct((s, N_PAIRS, LANES, n_t * kt_w), jnp.bfloat16),
                 jax.ShapeDtypeStruct((s, N_PAIRS, t_len, LANES), jnp.bfloat16)]
    out_specs = [pl.BlockSpec((1, keep, E_B), lambda b, t: (b, 0, 0)),
                 pl.BlockSpec((1, keep, E_B), lambda b, t: (b, 0, 0)),
                 pl.BlockSpec((1, N_PAIRS, LANES, kt_w), lambda b, t: (b, 0, 0, t)),
                 pl.BlockSpec((1, N_PAIRS, tm, LANES), lambda b, t: (b, 0, t, 0))]
    side_in, side_out, side_shapes = _side_cast_specs(side_casts, s * n_t,
                                                      lambda b, t: b * n_t + t)
    args += [arr for arr, _ in side_casts]
    in_specs += side_in
    out_shape += side_shapes
    out_specs += side_out
    return pl.pallas_call(
        functools.partial(_kv_kernel, tm=tm, n_side=len(side_casts)),
        out_shape=tuple(out_shape),
        grid=(s, n_t),
        in_specs=in_specs,
        out_specs=tuple(out_specs),
        compiler_params=pltpu.CompilerParams(
            dimension_semantics=("arbitrary", "arbitrary"),
            vmem_limit_bytes=VMEM_LIMIT_BYTES),
        name="kv_proj",
    )(*args)


BIAS_Q = 2 * CHUNK
BIAS_K = BIAS_Q + KV_ROWS
BIAS_BASE = 1024


BIAS_HEADS_PER_STEP = 4


def _bias_kernel(base_ref, wide_ref, narrow_ref):
    qc = lax.broadcasted_iota(jnp.int32, (BIAS_Q, BIAS_K), 0) // CHUNK
    kc = lax.broadcasted_iota(jnp.int32, (BIAS_Q, BIAS_K), 1) // CHUNK
    valid = (kc >= qc) & (kc <= qc + N_LEFT_CHUNKS)
    for h in range(BIAS_HEADS_PER_STEP):
        base = base_ref[h] * LOG2E
        tiled = jnp.broadcast_to(base, (BIAS_Q, BIAS_BASE))
        toep = pltpu.roll(tiled, 0, 1, stride=1, stride_axis=0)[:, 0:BIAS_K]
        table = jnp.where(valid, toep, NEG_INF)
        wide_ref[h] = table
        narrow_ref[h] = table[0:CHUNK, 0:CHUNK + KV_ROWS]


def _bias_tables(rel_bias):
    n = rel_bias.shape[0] * N_HEADS
    f = jnp.transpose(rel_bias, (0, 2, 1)).reshape(n, N_REL)
    rev = f[:, ::-1]
    far = f[:, N_REL - 1:N_REL]
    n_left = KV_ROWS - MAX_REL
    base = jnp.concatenate(
        [jnp.broadcast_to(far, (n, n_left)), rev,
         jnp.broadcast_to(far, (n, BIAS_BASE - n_left - N_REL))], axis=1)
    base = base.reshape(n, 1, BIAS_BASE)
    hs = BIAS_HEADS_PER_STEP
    return pl.pallas_call(
        _bias_kernel,
        out_shape=(jax.ShapeDtypeStruct((n, BIAS_Q, BIAS_K), jnp.float32),
                   jax.ShapeDtypeStruct((n, CHUNK, CHUNK + KV_ROWS), jnp.float32)),
        grid=(n // hs,),
        in_specs=[pl.BlockSpec((hs, 1, BIAS_BASE), lambda h: (h, 0, 0))],
        out_specs=(pl.BlockSpec((hs, BIAS_Q, BIAS_K), lambda h: (h, 0, 0)),
                   pl.BlockSpec((hs, CHUNK, CHUNK + KV_ROWS), lambda h: (h, 0, 0))),
        compiler_params=pltpu.CompilerParams(dimension_semantics=("arbitrary",)),
        name="bias_table",
    )(base)


def _attn_layer_kernel(*refs, tm, qg):
    tile = functools.partial(_attn_tile, *refs, tm=tm, qg=qg)
    first = pl.program_id(1) == 0
    pl.when(first)(functools.partial(tile, has_a=False))
    pl.when(jnp.logical_not(first))(functools.partial(tile, has_a=True))


def _attn_tile(x_ref, kta_ref, ktb_ref, va_ref, vb_ref, bias_ref,
               g_ref, win_ref, gq_ref, gmat_ref, wout_ref,
               y_ref, q_ref, o_ref, z_ref, *, tm, qg, has_a):
    n_grp = tm // qg
    cb = PIECE_COLS
    n_cb = E_B // cb
    pairs_per_cb = cb // LANES
    x = x_ref[0]
    hb = _rms(x, g_ref[...]).astype(jnp.bfloat16)
    q_gain = gq_ref[...] * (HEAD_DIM ** -0.5 * LOG2E)

    def rows_of(j):
        return slice(j * qg, (j + 1) * qg)

    def q_piece(c):
        cols = slice(c * cb, (c + 1) * cb)
        qn = _head_rms(_bdot(hb, win_ref[:, cols]), gmat_ref, q_gain[:, cols])
        for h in range(pairs_per_cb):
            q_ref[c * pairs_per_cb + h] = qn[:, h * LANES:(h + 1) * LANES].astype(jnp.bfloat16)

    def z_piece(c):
        cols = slice(c * cb, (c + 1) * cb)
        z_ref[:, cols] = _bdot(hb, win_ref[:, E_B + c * cb:E_B + (c + 1) * cb])

    def out_piece(c):
        cols = slice(c * cb, (c + 1) * cb)
        o = jnp.concatenate([o_ref[c * pairs_per_cb + h] for h in range(pairs_per_cb)],
                            axis=1)
        yb = (o * _silu(z_ref[:, cols])).astype(jnp.bfloat16)
        contrib = _bdot(yb, wout_ref[cols, :])
        if c == 0:
            y_ref[0] = x + contrib
        else:
            y_ref[0] += contrib

    lane_lo = lax.broadcasted_iota(jnp.int32, (1, LANES), 1) < HEAD_DIM
    widths = [(KV_ROWS - j * qg, (j + 1) * qg) for j in range(n_grp)]

    def scores(p, j):
        wa, wbj = widths[j]
        qb = q_ref[p, j * qg:(j + 1) * qg, :]
        zero = jnp.zeros_like(qb)
        lhs = jnp.concatenate([jnp.where(lane_lo, qb, zero),
                               jnp.where(lane_lo, zero, qb)], axis=0)
        s = _bdot(lhs, ktb_ref[0, p, :, 0:wbj])
        if has_a:
            sa = _bdot(lhs, kta_ref[0, p, :, KV_ROWS - wa:KV_ROWS])
            return jnp.concatenate([sa, s], axis=1) + bias_ref[p]
        return s + bias_ref[p, :, wa:wa + wbj]

    def attend(p, j, s):
        wa, wbj = widths[j]
        mx = jnp.max(s, axis=1, keepdims=True)
        e = jnp.exp2(s - mx)
        l = jnp.sum(e, axis=1, keepdims=True)
        pb = e.astype(jnp.bfloat16)
        if has_a:
            o2 = (_bdot(pb[:, 0:wa], va_ref[0, p, KV_ROWS - wa:KV_ROWS, :])
                  + _bdot(pb[:, wa:wa + wbj], vb_ref[0, p, 0:wbj, :]))
        else:
            o2 = _bdot(pb, vb_ref[0, p, 0:wbj, :])
        o2 = o2 * (1.0 / l)
        o_ref[p, rows_of(j), :] = jnp.where(lane_lo, o2[0:qg], o2[qg:2 * qg])

    items = [(c * pairs_per_cb + h, j) for c in range(n_cb) for j in range(n_grp)
             for h in range(pairs_per_cb)]
    per_cb = len(items) // n_cb
    pieces = []
    for c in range(n_cb):
        if c > 0:
            pieces.append((per_cb * c - QK_AHEAD - 1, 0, len(pieces), q_piece, c))
        pieces.append((per_cb * c + per_cb // 2, 0, len(pieces), z_piece, c))
        if c + 1 < n_cb:
            pieces.append((per_cb * (c + 1) + per_cb // 2, per_cb * (c + 1), len(pieces),
                           out_piece, c))
    n_pieces = len(pieces)
    q_piece(0)
    pending = {}
    for i in range(min(QK_AHEAD, len(items))):
        pending[i] = scores(*items[i])
    issued = 0
    for i in range(len(items)):
        quota = ((i + 1) * n_pieces) // len(items) - issued
        ready = sorted(pc for pc in pieces if pc[1] <= i)
        take = [pc for pc in ready if pc[0] <= i]
        take += [pc for pc in ready if pc[0] > i][:max(0, quota - len(take))]
        for pc in take:
            pieces.remove(pc)
            pc[3](pc[4])
            issued += 1
        if i + QK_AHEAD < len(items):
            pending[i + QK_AHEAD] = scores(*items[i + QK_AHEAD])
        attend(*items[i], pending.pop(i))
    for pc in sorted(pieces):
        pc[3](pc[4])
    out_piece(n_cb - 1)


def _attn_layer(x, kt, v, bias, g_norm, w_in, gq_t, gmat, w_out, *, layer, tm, qg):
    s, t_len, _ = x.shape
    n_t = t_len // tm
    const2 = lambda b, t: (0, 0)
    assert tm == KV_ROWS and kt.shape[-1] == t_len
    prev = lambda t: jnp.maximum(t - 1, 0)
    in_specs = [
        pl.BlockSpec((1, tm, D_MODEL), lambda b, t: (b, t, 0)),
        pl.BlockSpec((1, N_PAIRS, LANES, KV_ROWS), lambda b, t: (b, 0, 0, prev(t))),
        pl.BlockSpec((1, N_PAIRS, LANES, tm), lambda b, t: (b, 0, 0, t)),
        pl.BlockSpec((1, N_PAIRS, KV_ROWS, LANES), lambda b, t: (b, 0, prev(t), 0)),
        pl.BlockSpec((1, N_PAIRS, tm, LANES), lambda b, t: (b, 0, t, 0)),
        _layer_spec(bias, layer),
        _layer_spec(g_norm, layer),
        _layer_spec(w_in, layer),
        _layer_spec(gq_t, layer),
        pl.BlockSpec((MXU_DIM, MXU_DIM), const2),
        _layer_spec(w_out, layer),
    ]
    return pl.pallas_call(
        functools.partial(_attn_layer_kernel, tm=tm, qg=qg),
        out_shape=jax.ShapeDtypeStruct(x.shape, jnp.float32),
        grid=(s, n_t),
        in_specs=in_specs,
        out_specs=pl.BlockSpec((1, tm, D_MODEL), lambda b, t: (b, t, 0)),
        scratch_shapes=[pltpu.VMEM((N_PAIRS, tm, LANES), jnp.bfloat16),
                        pltpu.VMEM((N_PAIRS, tm, LANES), jnp.float32),
                        pltpu.VMEM((tm, E_B), jnp.float32)],
        compiler_params=pltpu.CompilerParams(
            dimension_semantics=("arbitrary", "arbitrary"),
            vmem_limit_bytes=VMEM_LIMIT_BYTES),
        name="attn_layer",
    )(x, kt, kt, v, v, bias, g_norm, w_in, gq_t, gmat, w_out)


def _attn_cached_kernel(x_ref, ktc_ref, ktn_ref, vtc_ref, vn_ref, bias_ref,
                        g_ref, win_ref, gq_ref, gmat_ref, wout_ref,
                        y_ref, q_ref, o_ref, z_ref, *, n_seq, qg):
    b = pl.program_id(0)
    lane_lo = lax.broadcasted_iota(jnp.int32, (1, LANES), 1) < HEAD_DIM
    contract_last = (((1,), (1,)), ((), ()))

    @pl.when(b == 0)
    def _():
        hb = _rms(x_ref[...], g_ref[...]).astype(jnp.bfloat16)
        q_gain = gq_ref[...] * (HEAD_DIM ** -0.5 * LOG2E)
        qn = _head_rms(_bdot(hb, win_ref[:, 0:E_B]), gmat_ref, q_gain)
        for p in range(N_PAIRS):
            q_ref[p] = qn[:, p * LANES:(p + 1) * LANES].astype(jnp.bfloat16)
        z_ref[...] = _bdot(hb, win_ref[:, E_B:2 * E_B])

    rows = pl.ds(pl.multiple_of(b * qg, qg), qg)

    def scores(p):
        qb = q_ref[p, rows, :]
        zero = jnp.zeros_like(qb)
        lhs = jnp.concatenate([jnp.where(lane_lo, qb, zero),
                               jnp.where(lane_lo, zero, qb)], axis=0)
        sa = _bdot(lhs, ktc_ref[0, p].astype(jnp.bfloat16))
        sb = _bdot(lhs, ktn_ref[0, p, :, 0:qg])
        return jnp.concatenate([sa, sb], axis=1) + bias_ref[p]

    def attend(p, s):
        keys = vtc_ref.shape[-1]
        mx = jnp.max(s, axis=1, keepdims=True)
        e = jnp.exp2(s - mx)
        l = jnp.sum(e, axis=1, keepdims=True)
        pb = e.astype(jnp.bfloat16)
        oa = lax.dot_general(pb[:, 0:keys], vtc_ref[0, p].astype(jnp.bfloat16), contract_last,
                             preferred_element_type=jnp.float32)
        o2 = (oa + _bdot(pb[:, keys:keys + qg], vn_ref[0, p])) * (1.0 / l)
        o_ref[p, rows, :] = jnp.where(lane_lo, o2[0:qg], o2[qg:2 * qg])

    pending = {p: scores(p) for p in range(min(QK_AHEAD, N_PAIRS))}
    for p in range(N_PAIRS):
        if p + QK_AHEAD < N_PAIRS:
            pending[p + QK_AHEAD] = scores(p + QK_AHEAD)
        attend(p, pending.pop(p))

    @pl.when(b == n_seq - 1)
    def _():
        o = jnp.concatenate([o_ref[p] for p in range(N_PAIRS)], axis=1)
        yb = (o * _silu(z_ref[...])).astype(jnp.bfloat16)
        y_ref[...] = x_ref[...] + _bdot(yb, wout_ref[...])


def _attn_layer_cached(x, kt_cache, kt_new, vt_cache, v_new, bias, g_norm, w_in, gq_t, gmat,
                       w_out, *, layer):
    s, qg, _ = x.shape
    m = s * qg
    keys = kt_cache.shape[-1]
    whole = pl.BlockSpec((m, D_MODEL), lambda b: (0, 0))
    per_seq = lambda a: pl.BlockSpec((1,) + a.shape[1:], lambda b: (b, 0, 0, 0))
    in_specs = [whole, per_seq(kt_cache), per_seq(kt_new), per_seq(vt_cache), per_seq(v_new),
                _layer_spec(bias, layer), _layer_spec(g_norm, layer), _layer_spec(w_in, layer),
                _layer_spec(gq_t, layer), pl.BlockSpec((MXU_DIM, MXU_DIM), lambda b: (0, 0)),
                _layer_spec(w_out, layer)]
    assert bias.shape[-1] == keys + qg
    y = pl.pallas_call(
        functools.partial(_attn_cached_kernel, n_seq=s, qg=qg),
        out_shape=jax.ShapeDtypeStruct((m, D_MODEL), jnp.float32),
        grid=(s,),
        in_specs=in_specs,
        out_specs=whole,
        scratch_shapes=[pltpu.VMEM((N_PAIRS, m, LANES), jnp.bfloat16),
                        pltpu.VMEM((N_PAIRS, m, LANES), jnp.float32),
                        pltpu.VMEM((m, E_B), jnp.float32)],
        compiler_params=pltpu.CompilerParams(
            dimension_semantics=("arbitrary",), vmem_limit_bytes=VMEM_LIMIT_BYTES),
        name="attn_cached",
    )(x.reshape(m, D_MODEL), kt_cache, kt_new, vt_cache, v_new, bias, g_norm, w_in, gq_t, gmat,
      w_out)
    return y.reshape(x.shape)


def _trunk(x, hist16, cache_k, cache_v, pos0, w, *, pool_seg, pool_len, tile):
    s, t_len, _ = x.shape
    hists = []
    n_a = len(w["w_in_a"])
    for layer in range(n_a):
        side = []
        if layer + 1 < n_a and w["w_in_a"][layer + 1] is None:
            side = [(w["f32"]["w_in_a"], layer + 1)]
        x, h_out, *cast = _pool_layer(
            x, hist16, layer, (w["norm_a"], layer), (w["w_in_a"][layer], 0),
            (w["w_grp_a"], layer), (w["scale_a"], layer), (w["w_out_a"], layer),
            n_seg=pool_seg, seg_len=pool_len, pos0=pos0, side_casts=side)
        if side:
            w["w_in_a"][layer + 1] = cast[0][None]
        hists.append(h_out[:, HIST_PAD - POOL_HIST:, :])
    side = []
    if w["w_in_b"] is None:
        side = [(w["f32"][name].reshape(-1, w["f32"][name].shape[-1]), None)
                for name in ("w_in_b", "w_out_b")]
    k_new, v_new, kt, vb, *cast = _kv_proj(x, w["norm_kv"], w["w_kv"], w["gk_t"], w["gmat"],
                                           tm=tile, side_casts=side)
    if side:
        w["w_in_b"] = cast[0].reshape(w["f32"]["w_in_b"].shape)
        w["w_out_b"] = cast[1].reshape(w["f32"]["w_out_b"].shape)
    weights = (w["norm_b"], w["w_in_b"], w["gq_t"], w["gmat"], w["w_out_b"])
    if cache_k is None:
        for j in range(w["w_in_b"].shape[0]):
            x = _attn_layer(x, kt, vb, w["bias_prompt"], *weights, layer=j, tm=tile, qg=BIAS_Q)
    else:
        lc = cache_k.shape[1]
        as_pairs = lambda c: jnp.transpose(c, (0, 2, 3, 1)).reshape(s, N_PAIRS, LANES, lc)
        for j in range(w["w_in_b"].shape[0]):
            x = _attn_layer_cached(x, as_pairs(cache_k), kt, as_pairs(cache_v), vb,
                                   w["bias_sample"], *weights, layer=j)
    return x, jnp.stack(hists, axis=0), k_new, v_new


def kernel(x_prompt, x_sample, state_pool, cache_k, cache_v, norm_a, w_in_a, w_grp_a, scale_a,
           w_out_a, norm_kv, w_kv, g_k, norm_b, w_in_b, g_q, rel_bias_b, w_out_b):
    bf = jnp.bfloat16
    head_of_lane = jnp.arange(MXU_DIM, dtype=jnp.int32) // HEAD_DIM
    gmat = (head_of_lane[:, None] == head_of_lane[None, :]).astype(bf)
    n_b = rel_bias_b.shape[0]
    wide, narrow = _bias_tables(rel_bias_b)
    pair_rows = HEADS_PER_LANE_BLOCK
    w = dict(
        f32=dict(w_in_a=w_in_a, w_in_b=w_in_b, w_out_b=w_out_b),
        norm_a=norm_a[:, None, :],
        w_in_a=[w_in_a[0:1].astype(bf)] + [None] * (w_in_a.shape[0] - 1),
        w_grp_a=w_grp_a.astype(bf), scale_a=scale_a[:, None, :], w_out_a=w_out_a.astype(bf),
        norm_kv=norm_kv[None, :], w_kv=w_kv.astype(bf), gk_t=jnp.tile(g_k, N_HEADS)[None, :],
        norm_b=norm_b[:, None, :], w_in_b=None,
        gq_t=jnp.tile(g_q, (1, N_HEADS))[:, None, :], w_out_b=None, gmat=gmat,
        bias_prompt=wide.reshape(n_b, N_PAIRS, pair_rows * BIAS_Q, BIAS_K),
        bias_sample=narrow.reshape(n_b, N_PAIRS, pair_rows * CHUNK, CHUNK + KV_ROWS),
    )
    bp, sp, _ = x_prompt.shape
    bs, ss, _ = x_sample.shape
    y_p, pool_p, k_p, v_p = _trunk(x_prompt, None, None, None, 0, w,
                                   pool_seg=1, pool_len=min(POOL_TILE, sp), tile=KV_ROWS)
    hist16 = jnp.pad(state_pool, ((0, 0), (0, 0), (HIST_PAD - POOL_HIST, 0), (0, 0)))
    y_s, pool_s, k_s, v_s = _trunk(x_sample, hist16, cache_k, cache_v, PAST_LEN, w,
                                   pool_seg=bs, pool_len=ss, tile=ss)
    lp = min(KV_ROWS, sp)
    return (y_p, y_s, pool_p, pool_s,
            k_p.reshape(bp, lp, N_HEADS, HEAD_DIM), v_p.reshape(bp, lp, N_HEADS, HEAD_DIM),
            k_s.reshape(bs, ss, N_HEADS, HEAD_DIM), v_s.reshape(bs, ss, N_HEADS, HEAD_DIM))
```

```python
import functools
import math

import jax
import jax.numpy as jnp
from jax import lax
from jax.experimental import pallas as pl
from jax.experimental.pallas import tpu as pltpu

D_MODEL = 1024
E_A = 2048
POOL_WINDOWS = (2, 4, 8, 16)
N_POOL_GROUPS = len(POOL_WINDOWS)
G_A = E_A // N_POOL_GROUPS
POOL_HIST = max(POOL_WINDOWS) - 1
HEAD_DIM = 64
N_HEADS = 16
E_B = N_HEADS * HEAD_DIM
CHUNK = 64
N_LEFT_CHUNKS = 8
KV_ROWS = N_LEFT_CHUNKS * CHUNK
MAX_REL = 128
N_REL = 2 * MAX_REL + 1
EPS = 1e-6
NEG_INF = -1e30
PAST_LEN = 4096
LOG2E = math.log2(math.e)

SUBLANES = 8
LANES = 128
MXU_DIM = 256
HEADS_PER_LANE_BLOCK = LANES // HEAD_DIM
N_PAIRS = N_HEADS // HEADS_PER_LANE_BLOCK
HIST_PAD = 2 * SUBLANES
FRONT = SUBLANES
QK_AHEAD = 2
PIECE_COLS = 2 * MXU_DIM
POOL_TILE = 1024
VMEM_LIMIT_BYTES = 56 * 1024 * 1024


def _bdot(a, b):
    return jnp.dot(a, b, preferred_element_type=jnp.float32)


def _rms(x, g):
    ms = jnp.mean(x * x, axis=-1, keepdims=True)
    return x * lax.rsqrt(ms + EPS) * g


def _silu(z):
    return z * (1.0 / (1.0 + jnp.exp(-z)))


def _side_cast_specs(arrays, n_steps, step_of):
    in_specs, out_specs, out_shapes = [], [], []
    for arr, lead in arrays:
        n_rows, n_cols = arr.shape[-2:]
        rows = n_rows // n_steps
        assert rows * n_steps == n_rows and rows % (2 * SUBLANES) == 0
        if lead is None:
            in_specs.append(pl.BlockSpec((rows, n_cols), lambda *i: (step_of(*i), 0)))
        else:
            in_specs.append(pl.BlockSpec((None, rows, n_cols),
                                         lambda *i, lead=lead: (lead, step_of(*i), 0)))
        out_specs.append(pl.BlockSpec((rows, n_cols), lambda *i: (step_of(*i), 0)))
        out_shapes.append(jax.ShapeDtypeStruct((n_rows, n_cols), jnp.bfloat16))
    return in_specs, out_specs, out_shapes


def _pool_layer_kernel(*refs, n_seg, seg_len, pos0, has_hist, n_side):
    it = iter(refs)
    x_ref = next(it)
    hist_ref = next(it) if has_hist else None
    g_ref, win_ref, wgrp_ref, scale_ref, wout_ref = (next(it) for _ in range(5))
    side_in = [next(it) for _ in range(n_side)]
    y_ref, hout_ref = next(it), next(it)
    side_out = [next(it) for _ in range(n_side)]
    acc_ref, upad_ref, s0_ref, s1_ref, carry_ref = it
    for src, dst in zip(side_in, side_out):
        dst[...] = src[...].astype(dst.dtype)
    t = pl.program_id(1)
    m = n_seg * seg_len
    rows = HIST_PAD + seg_len
    x = x_ref[...].reshape(m, D_MODEL)
    hb = _rms(x, g_ref[...]).astype(jnp.bfloat16)

    @pl.when(t == 0)
    def _():
        if has_hist:
            carry_ref[...] = hist_ref[...]
        else:
            carry_ref[...] = jnp.zeros_like(carry_ref)

    zero_front = jnp.zeros((n_seg, FRONT, G_A), jnp.float32)
    upad_ref[:, 0:FRONT, :] = zero_front
    s0_ref[:, 0:FRONT, :] = zero_front
    s1_ref[:, 0:FRONT, :] = zero_front

    pos = pos0 + t * seg_len + lax.broadcasted_iota(jnp.int32, (1, seg_len, 1), 1)

    for g, w in enumerate(POOL_WINDOWS):
        lo = g * G_A
        u = _bdot(hb, win_ref[:, lo:lo + G_A])
        z = _bdot(hb, win_ref[:, E_A + lo:E_A + lo + G_A])
        u3 = u.reshape(n_seg, seg_len, G_A)
        upad_ref[:, FRONT:FRONT + HIST_PAD, :] = carry_ref[:, :, lo:lo + G_A]
        upad_ref[:, FRONT + HIST_PAD:, :] = u3
        carry_ref[:, :, lo:lo + G_A] = upad_ref[:, FRONT + seg_len:, :]
        bufs = (upad_ref, s0_ref, s1_ref)
        src = 0
        for k in range(g + 1):
            dst = 1 if src != 1 else 2
            sh = 1 << k
            bufs[dst][:, FRONT:, :] = (bufs[src][:, FRONT:, :]
                                       + bufs[src][:, FRONT - sh:FRONT - sh + rows, :])
            src = dst
        wsum = bufs[src][:, FRONT + HIST_PAD:, :]
        inv_cnt = 1.0 / jnp.minimum(pos + 1, w).astype(jnp.float32)
        pooled = (wsum * inv_cnt - u3).reshape(m, G_A)
        pg = _bdot(pooled.astype(jnp.bfloat16), wgrp_ref[g]) * scale_ref[:, lo:lo + G_A]
        yb = (pg * _silu(z)).astype(jnp.bfloat16)
        contrib = _bdot(yb, wout_ref[lo:lo + G_A, :])
        if g == 0:
            acc_ref[...] = contrib
        else:
            acc_ref[...] += contrib

    y_ref[...] = (x + acc_ref[...]).reshape(n_seg, seg_len, D_MODEL)
    hout_ref[...] = carry_ref[...]


def _layer_spec(arr, layer):
    tail = (0,) * (arr.ndim - 1)
    return pl.BlockSpec((None,) + arr.shape[1:], lambda *_: (layer,) + tail,
                        pipeline_mode=pl.Buffered(1))


def _pool_layer(x, hist16, layer, g_norm, w_in, w_grp, scale, w_out, *, n_seg, seg_len, pos0,
                side_casts=()):
    s, t_len, _ = x.shape
    has_hist = hist16 is not None
    grid = (s // n_seg, t_len // seg_len)
    in_specs = [pl.BlockSpec((n_seg, seg_len, D_MODEL), lambda b, t: (b, t, 0))]
    args = [x]
    if has_hist:
        in_specs.append(pl.BlockSpec((None, n_seg, HIST_PAD, E_A), lambda b, t: (layer, b, 0, 0)))
        args.append(hist16)
    for arr, idx in (g_norm, w_in, w_grp, scale, w_out):
        args.append(arr)
        in_specs.append(_layer_spec(arr, idx))
    side_in, side_out, side_shapes = _side_cast_specs(
        side_casts, grid[0] * grid[1], lambda b, t: b * grid[1] + t)
    args += [arr for arr, _ in side_casts]
    in_specs += side_in
    m = n_seg * seg_len
    stage = pltpu.VMEM((n_seg, FRONT + HIST_PAD + seg_len, G_A), jnp.float32)
    return pl.pallas_call(
        functools.partial(_pool_layer_kernel, n_seg=n_seg, seg_len=seg_len, pos0=pos0,
                          has_hist=has_hist, n_side=len(side_casts)),
        out_shape=(jax.ShapeDtypeStruct(x.shape, jnp.float32),
                   jax.ShapeDtypeStruct((s, HIST_PAD, E_A), jnp.float32), *side_shapes),
        grid=grid,
        in_specs=in_specs,
        out_specs=(pl.BlockSpec((n_seg, seg_len, D_MODEL), lambda b, t: (b, t, 0)),
                   pl.BlockSpec((n_seg, HIST_PAD, E_A), lambda b, t: (b, 0, 0)), *side_out),
        scratch_shapes=[pltpu.VMEM((m, D_MODEL), jnp.float32), stage, stage, stage,
                        pltpu.VMEM((n_seg, HIST_PAD, E_A), jnp.float32)],
        compiler_params=pltpu.CompilerParams(
            dimension_semantics=("arbitrary", "arbitrary"),
            vmem_limit_bytes=VMEM_LIMIT_BYTES),
        name="pool_layer",
    )(*args)


def _head_rms(v, gmat_ref, gain):
    sq = (v * v).astype(jnp.bfloat16)
    ss = jnp.concatenate([_bdot(sq[:, c:c + MXU_DIM], gmat_ref[...])
                          for c in range(0, v.shape[-1], MXU_DIM)], axis=1)
    return v * lax.rsqrt(ss * (1.0 / HEAD_DIM) + EPS) * gain


def _kv_kernel(*refs, tm, n_side):
    it = iter(refs)
    x_ref, g_ref, wkv_ref, gk_ref, gmat_ref = (next(it) for _ in range(5))
    side_in = [next(it) for _ in range(n_side)]
    kout_ref, vout_ref, kt_ref, vb_ref = (next(it) for _ in range(4))
    for src, dst in zip(side_in, it):
        dst[...] = src[...].astype(dst.dtype)
    x = x_ref[0]
    hb = _rms(x, g_ref[...]).astype(jnp.bfloat16)
    k = _bdot(hb, wkv_ref[:, 0:E_B])
    kn = _head_rms(k, gmat_ref, gk_ref[...])
    v = _bdot(hb, wkv_ref[:, E_B:2 * E_B])
    kout_ref[0] = kn
    vout_ref[0] = v
    kt_w = kt_ref.shape[-1]
    for p in range(N_PAIRS):
        blk = kn[:, p * LANES:(p + 1) * LANES]
        if tm < LANES:
            blk = jnp.concatenate([blk, jnp.zeros((LANES - tm, LANES), jnp.float32)], axis=0)
        kt_ref[0, p] = blk.T[:, 0:kt_w].astype(jnp.bfloat16)
        vb_ref[0, p] = v[:, p * LANES:(p + 1) * LANES].astype(jnp.bfloat16)


def _kv_proj(x, g_kv, w_kv, gk_t, gmat, *, tm, side_casts=()):
    s, t_len, _ = x.shape
    n_t = t_len // tm
    keep = min(KV_ROWS, t_len)
    assert keep == tm or n_t == 1
    kt_w = max(tm, LANES)
    const2 = lambda b, t: (0, 0)
    in_specs = [pl.BlockSpec((1, tm, D_MODEL), lambda b, t: (b, t, 0)),
                pl.BlockSpec((1, D_MODEL), const2),
                pl.BlockSpec((D_MODEL, 2 * E_B), const2),
                pl.BlockSpec((1, E_B), const2),
                pl.BlockSpec((MXU_DIM, MXU_DIM), const2)]
    args = [x, g_kv, w_kv, gk_t, gmat]
    out_shape = [jax.ShapeDtypeStruct((s, keep, E_B), jnp.float32),
                 jax.ShapeDtypeStruct((s, keep, E_B), jnp.float32),
                 jax.ShapeDtypeStruct((s, N_PAIRS, LANES, n_t * kt_w), jnp.bfloat16),
                 jax.ShapeDtypeStruct((s, N_PAIRS, t_len, LANES), jnp.bfloat16)]
    out_specs = [pl.BlockSpec((1, keep, E_B), lambda b, t: (b, 0, 0)),
                 pl.BlockSpec((1, keep, E_B), lambda b, t: (b, 0, 0)),
                 pl.BlockSpec((1, N_PAIRS, LANES, kt_w), lambda b, t: (b, 0, 0, t)),
                 pl.BlockSpec((1, N_PAIRS, tm, LANES), lambda b, t: (b, 0, t, 0))]
    side_in, side_out, side_shapes = _side_cast_specs(side_casts, s * n_t,
                                                      lambda b, t: b * n_t + t)
    args += [arr for arr, _ in side_casts]
    in_specs += side_in
    out_shape += side_shapes
    out_specs += side_out
    return pl.pallas_call(
        functools.partial(_kv_kernel, tm=tm, n_side=len(side_casts)),
        out_shape=tuple(out_shape),
        grid=(s, n_t),
        in_specs=in_specs,
        out_specs=tuple(out_specs),
        compiler_params=pltpu.CompilerParams(
            dimension_semantics=("arbitrary", "arbitrary"),
            vmem_limit_bytes=VMEM_LIMIT_BYTES),
        name="kv_proj",
    )(*args)


BIAS_Q = 2 * CHUNK
BIAS_K = BIAS_Q + KV_ROWS
BIAS_BASE = 1024


BIAS_HEADS_PER_STEP = 4


def _bias_kernel(base_ref, *refs, n_side):
    side_in = refs[:n_side]
    wide_ref, narrow_ref = refs[n_side:n_side + 2]
    for src, dst in zip(side_in, refs[n_side + 2:]):
        dst[...] = src[...].astype(dst.dtype)
    qc = lax.broadcasted_iota(jnp.int32, (BIAS_Q, BIAS_K), 0) // CHUNK
    kc = lax.broadcasted_iota(jnp.int32, (BIAS_Q, BIAS_K), 1) // CHUNK
    valid = (kc >= qc) & (kc <= qc + N_LEFT_CHUNKS)
    for h in range(BIAS_HEADS_PER_STEP):
        base = base_ref[h] * LOG2E
        tiled = jnp.broadcast_to(base, (BIAS_Q, BIAS_BASE))
        toep = pltpu.roll(tiled, 0, 1, stride=1, stride_axis=0)[:, 0:BIAS_K]
        table = jnp.where(valid, toep, NEG_INF)
        wide_ref[h] = table
        narrow_ref[h] = table[0:CHUNK, 0:CHUNK + KV_ROWS]


def _bias_tables(rel_bias, side_casts=()):
    n = rel_bias.shape[0] * N_HEADS
    f = jnp.transpose(rel_bias, (0, 2, 1)).reshape(n, N_REL)
    rev = f[:, ::-1]
    far = f[:, N_REL - 1:N_REL]
    n_left = KV_ROWS - MAX_REL
    base = jnp.concatenate(
        [jnp.broadcast_to(far, (n, n_left)), rev,
         jnp.broadcast_to(far, (n, BIAS_BASE - n_left - N_REL))], axis=1)
    base = base.reshape(n, 1, BIAS_BASE)
    hs = BIAS_HEADS_PER_STEP
    side_in, side_out, side_shapes = _side_cast_specs(side_casts, n // hs, lambda h: h)
    return pl.pallas_call(
        functools.partial(_bias_kernel, n_side=len(side_casts)),
        out_shape=(jax.ShapeDtypeStruct((n, BIAS_Q, BIAS_K), jnp.float32),
                   jax.ShapeDtypeStruct((n, CHUNK, CHUNK + KV_ROWS), jnp.float32), *side_shapes),
        grid=(n // hs,),
        in_specs=[pl.BlockSpec((hs, 1, BIAS_BASE), lambda h: (h, 0, 0)), *side_in],
        out_specs=(pl.BlockSpec((hs, BIAS_Q, BIAS_K), lambda h: (h, 0, 0)),
                   pl.BlockSpec((hs, CHUNK, CHUNK + KV_ROWS), lambda h: (h, 0, 0)), *side_out),
        compiler_params=pltpu.CompilerParams(dimension_semantics=("arbitrary",),
                                             vmem_limit_bytes=VMEM_LIMIT_BYTES),
        name="bias_table",
    )(base, *[arr for arr, _ in side_casts])


def _attn_layer_kernel(*refs, tm, qg):
    tile = functools.partial(_attn_tile, *refs, tm=tm, qg=qg)
    first = pl.program_id(1) == 0
    pl.when(first)(functools.partial(tile, has_a=False))
    pl.when(jnp.logical_not(first))(functools.partial(tile, has_a=True))


def _attn_tile(x_ref, kta_ref, ktb_ref, va_ref, vb_ref, bias_ref,
               g_ref, win_ref, gq_ref, gmat_ref, wout_ref,
               y_ref, q_ref, o_ref, z_ref, *, tm, qg, has_a):
    n_grp = tm // qg
    cb = PIECE_COLS
    n_cb = E_B // cb
    pairs_per_cb = cb // LANES
    x = x_ref[0]
    hb = _rms(x, g_ref[...]).astype(jnp.bfloat16)
    q_gain = gq_ref[...] * (HEAD_DIM ** -0.5 * LOG2E)

    def rows_of(j):
        return slice(j * qg, (j + 1) * qg)

    def q_piece(c):
        cols = slice(c * cb, (c + 1) * cb)
        qn = _head_rms(_bdot(hb, win_ref[:, cols]), gmat_ref, q_gain[:, cols])
        for h in range(pairs_per_cb):
            q_ref[c * pairs_per_cb + h] = qn[:, h * LANES:(h + 1) * LANES].astype(jnp.bfloat16)

    def z_piece(c):
        cols = slice(c * cb, (c + 1) * cb)
        z_ref[:, cols] = _bdot(hb, win_ref[:, E_B + c * cb:E_B + (c + 1) * cb])

    def out_piece(c):
        cols = slice(c * cb, (c + 1) * cb)
        o = jnp.concatenate([o_ref[c * pairs_per_cb + h] for h in range(pairs_per_cb)],
                            axis=1)
        yb = (o * _silu(z_ref[:, cols])).astype(jnp.bfloat16)
        contrib = _bdot(yb, wout_ref[cols, :])
        if c == 0:
            y_ref[0] = x + contrib
        else:
            y_ref[0] += contrib

    lane_lo = lax.broadcasted_iota(jnp.int32, (1, LANES), 1) < HEAD_DIM
    widths = [(KV_ROWS - j * qg, (j + 1) * qg) for j in range(n_grp)]

    def scores(p, j):
        wa, wbj = widths[j]
        qb = q_ref[p, j * qg:(j + 1) * qg, :]
        zero = jnp.zeros_like(qb)
        lhs = jnp.concatenate([jnp.where(lane_lo, qb, zero),
                               jnp.where(lane_lo, zero, qb)], axis=0)
        s = _bdot(lhs, ktb_ref[0, p, :, 0:wbj])
        if has_a:
            sa = _bdot(lhs, kta_ref[0, p, :, KV_ROWS - wa:KV_ROWS])
            return jnp.concatenate([sa, s], axis=1) + bias_ref[p]
        return s + bias_ref[p, :, wa:wa + wbj]

    def attend(p, j, s):
        wa, wbj = widths[j]
        mx = jnp.max(s, axis=1, keepdims=True)
        e = jnp.exp2(s - mx)
        l = jnp.sum(e, axis=1, keepdims=True)
        pb = e.astype(jnp.bfloat16)
        if has_a:
            o2 = (_bdot(pb[:, 0:wa], va_ref[0, p, KV_ROWS - wa:KV_ROWS, :])
                  + _bdot(pb[:, wa:wa + wbj], vb_ref[0, p, 0:wbj, :]))
        else:
            o2 = _bdot(pb, vb_ref[0, p, 0:wbj, :])
        o2 = o2 * (1.0 / l)
        o_ref[p, rows_of(j), :] = jnp.where(lane_lo, o2[0:qg], o2[qg:2 * qg])

    items = [(c * pairs_per_cb + h, j) for c in range(n_cb) for j in range(n_grp)
             for h in range(pairs_per_cb)]
    per_cb = len(items) // n_cb
    pieces = []
    for c in range(n_cb):
        if c > 0:
            pieces.append((per_cb * c - QK_AHEAD - 1, 0, len(pieces), q_piece, c))
        pieces.append((per_cb * c + per_cb // 2, 0, len(pieces), z_piece, c))
        if c + 1 < n_cb:
            pieces.append((per_cb * (c + 1) + per_cb // 2, per_cb * (c + 1), len(pieces),
                           out_piece, c))
    n_pieces = len(pieces)
    q_piece(0)
    pending = {}
    for i in range(min(QK_AHEAD, len(items))):
        pending[i] = scores(*items[i])
    issued = 0
    for i in range(len(items)):
        quota = ((i + 1) * n_pieces) // len(items) - issued
        ready = sorted(pc for pc in pieces if pc[1] <= i)
        take = [pc for pc in ready if pc[0] <= i]
        take += [pc for pc in ready if pc[0] > i][:max(0, quota - len(take))]
        for pc in take:
            pieces.remove(pc)
            pc[3](pc[4])
            issued += 1
        if i + QK_AHEAD < len(items):
            pending[i + QK_AHEAD] = scores(*items[i + QK_AHEAD])
        attend(*items[i], pending.pop(i))
    for pc in sorted(pieces):
        pc[3](pc[4])
    out_piece(n_cb - 1)


def _attn_layer(x, kt, v, bias, g_norm, w_in, gq_t, gmat, w_out, *, layer, tm, qg):
    s, t_len, _ = x.shape
    n_t = t_len // tm
    const2 = lambda b, t: (0, 0)
    assert tm == KV_ROWS and kt.shape[-1] == t_len
    prev = lambda t: jnp.maximum(t - 1, 0)
    in_specs = [
        pl.BlockSpec((1, tm, D_MODEL), lambda b, t: (b, t, 0)),
        pl.BlockSpec((1, N_PAIRS, LANES, KV_ROWS), lambda b, t: (b, 0, 0, prev(t))),
        pl.BlockSpec((1, N_PAIRS, LANES, tm), lambda b, t: (b, 0, 0, t)),
        pl.BlockSpec((1, N_PAIRS, KV_ROWS, LANES), lambda b, t: (b, 0, prev(t), 0)),
        pl.BlockSpec((1, N_PAIRS, tm, LANES), lambda b, t: (b, 0, t, 0)),
        _layer_spec(bias, layer),
        _layer_spec(g_norm, layer),
        _layer_spec(w_in, layer),
        _layer_spec(gq_t, layer),
        pl.BlockSpec((MXU_DIM, MXU_DIM), const2),
        _layer_spec(w_out, layer),
    ]
    return pl.pallas_call(
        functools.partial(_attn_layer_kernel, tm=tm, qg=qg),
        out_shape=jax.ShapeDtypeStruct(x.shape, jnp.float32),
        grid=(s, n_t),
        in_specs=in_specs,
        out_specs=pl.BlockSpec((1, tm, D_MODEL), lambda b, t: (b, t, 0)),
        scratch_shapes=[pltpu.VMEM((N_PAIRS, tm, LANES), jnp.bfloat16),
                        pltpu.VMEM((N_PAIRS, tm, LANES), jnp.float32),
                        pltpu.VMEM((tm, E_B), jnp.float32)],
        compiler_params=pltpu.CompilerParams(
            dimension_semantics=("arbitrary", "arbitrary"),
            vmem_limit_bytes=VMEM_LIMIT_BYTES),
        name="attn_layer",
    )(x, kt, kt, v, v, bias, g_norm, w_in, gq_t, gmat, w_out)


def _attn_cached_kernel(x_ref, ktc_ref, ktn_ref, vtc_ref, vn_ref, bias_ref,
                        g_ref, win_ref, gq_ref, gmat_ref, wout_ref,
                        y_ref, q_ref, o_ref, z_ref, *, n_seq, qg):
    b = pl.program_id(0)
    lane_lo = lax.broadcasted_iota(jnp.int32, (1, LANES), 1) < HEAD_DIM
    contract_last = (((1,), (1,)), ((), ()))

    @pl.when(b == 0)
    def _():
        hb = _rms(x_ref[...], g_ref[...]).astype(jnp.bfloat16)
        q_gain = gq_ref[...] * (HEAD_DIM ** -0.5 * LOG2E)
        qn = _head_rms(_bdot(hb, win_ref[:, 0:E_B]), gmat_ref, q_gain)
        for p in range(N_PAIRS):
            q_ref[p] = qn[:, p * LANES:(p + 1) * LANES].astype(jnp.bfloat16)
        z_ref[...] = _bdot(hb, win_ref[:, E_B:2 * E_B])

    rows = pl.ds(pl.multiple_of(b * qg, qg), qg)

    def scores(p):
        qb = q_ref[p, rows, :]
        zero = jnp.zeros_like(qb)
        lhs = jnp.concatenate([jnp.where(lane_lo, qb, zero),
                               jnp.where(lane_lo, zero, qb)], axis=0)
        sa = _bdot(lhs, ktc_ref[0, p].astype(jnp.bfloat16))
        sb = _bdot(lhs, ktn_ref[0, p, :, 0:qg])
        return jnp.concatenate([sa, sb], axis=1) + bias_ref[p]

    def attend(p, s):
        keys = vtc_ref.shape[-1]
        mx = jnp.max(s, axis=1, keepdims=True)
        e = jnp.exp2(s - mx)
        l = jnp.sum(e, axis=1, keepdims=True)
        pb = e.astype(jnp.bfloat16)
        oa = lax.dot_general(pb[:, 0:keys], vtc_ref[0, p].astype(jnp.bfloat16), contract_last,
                             preferred_element_type=jnp.float32)
        o2 = (oa + _bdot(pb[:, keys:keys + qg], vn_ref[0, p])) * (1.0 / l)
        o_ref[p, rows, :] = jnp.where(lane_lo, o2[0:qg], o2[qg:2 * qg])

    pending = {p: scores(p) for p in range(min(QK_AHEAD, N_PAIRS))}
    for p in range(N_PAIRS):
        if p + QK_AHEAD < N_PAIRS:
            pending[p + QK_AHEAD] = scores(p + QK_AHEAD)
        attend(p, pending.pop(p))

    @pl.when(b == n_seq - 1)
    def _():
        o = jnp.concatenate([o_ref[p] for p in range(N_PAIRS)], axis=1)
        yb = (o * _silu(z_ref[...])).astype(jnp.bfloat16)
        y_ref[...] = x_ref[...] + _bdot(yb, wout_ref[...])


def _attn_layer_cached(x, kt_cache, kt_new, vt_cache, v_new, bias, g_norm, w_in, gq_t, gmat,
                       w_out, *, layer):
    s, qg, _ = x.shape
    m = s * qg
    keys = kt_cache.shape[-1]
    whole = pl.BlockSpec((m, D_MODEL), lambda b: (0, 0))
    per_seq = lambda a: pl.BlockSpec((1,) + a.shape[1:], lambda b: (b, 0, 0, 0))
    in_specs = [whole, per_seq(kt_cache), per_seq(kt_new), per_seq(vt_cache), per_seq(v_new),
                _layer_spec(bias, layer), _layer_spec(g_norm, layer), _layer_spec(w_in, layer),
                _layer_spec(gq_t, layer), pl.BlockSpec((MXU_DIM, MXU_DIM), lambda b: (0, 0)),
                _layer_spec(w_out, layer)]
    assert bias.shape[-1] == keys + qg
    y = pl.pallas_call(
        functools.partial(_attn_cached_kernel, n_seq=s, qg=qg),
        out_shape=jax.ShapeDtypeStruct((m, D_MODEL), jnp.float32),
        grid=(s,),
        in_specs=in_specs,
        out_specs=whole,
        scratch_shapes=[pltpu.VMEM((N_PAIRS, m, LANES), jnp.bfloat16),
                        pltpu.VMEM((N_PAIRS, m, LANES), jnp.float32),
                        pltpu.VMEM((m, E_B), jnp.float32)],
        compiler_params=pltpu.CompilerParams(
            dimension_semantics=("arbitrary",), vmem_limit_bytes=VMEM_LIMIT_BYTES),
        name="attn_cached",
    )(x.reshape(m, D_MODEL), kt_cache, kt_new, vt_cache, v_new, bias, g_norm, w_in, gq_t, gmat,
      w_out)
    return y.reshape(x.shape)


def _trunk(x, hist16, cache_k, cache_v, pos0, w, *, pool_seg, pool_len, tile):
    s, t_len, _ = x.shape
    hists = []
    n_a = len(w["w_in_a"])
    for layer in range(n_a):
        side = []
        cast_next = layer + 1 < n_a and w["w_in_a"][layer + 1] is None
        cast_kv = layer + 1 == n_a and w["w_kv"] is None
        if cast_next:
            side.append((w["f32"]["w_in_a"], layer + 1))
        if cast_kv:
            side.append((w["f32"]["w_kv"], None))
        x, h_out, *cast = _pool_layer(
            x, hist16, layer, (w["norm_a"], layer), (w["w_in_a"][layer], 0),
            (w["w_grp_a"], layer), (w["scale_a"], layer), (w["w_out_a"], layer),
            n_seg=pool_seg, seg_len=pool_len, pos0=pos0, side_casts=side)
        if cast_next:
            w["w_in_a"][layer + 1] = cast.pop(0)[None]
        if cast_kv:
            w["w_kv"] = cast.pop(0)
        hists.append(h_out[:, HIST_PAD - POOL_HIST:, :])
    side = []
    if w["w_in_b"] is None:
        side = [(w["f32"][name].reshape(-1, w["f32"][name].shape[-1]), None)
                for name in ("w_in_b", "w_out_b")]
    k_new, v_new, kt, vb, *cast = _kv_proj(x, w["norm_kv"], w["w_kv"], w["gk_t"], w["gmat"],
                                           tm=tile, side_casts=side)
    if side:
        w["w_in_b"] = cast[0].reshape(w["f32"]["w_in_b"].shape)
        w["w_out_b"] = cast[1].reshape(w["f32"]["w_out_b"].shape)
    weights = (w["norm_b"], w["w_in_b"], w["gq_t"], w["gmat"], w["w_out_b"])
    if cache_k is None:
        for j in range(w["w_in_b"].shape[0]):
            x = _attn_layer(x, kt, vb, w["bias_prompt"], *weights, layer=j, tm=tile, qg=BIAS_Q)
    else:
        lc = cache_k.shape[1]
        as_pairs = lambda c: jnp.transpose(c, (0, 2, 3, 1)).reshape(s, N_PAIRS, LANES, lc)
        for j in range(w["w_in_b"].shape[0]):
            x = _attn_layer_cached(x, as_pairs(cache_k), kt, as_pairs(cache_v), vb,
                                   w["bias_sample"], *weights, layer=j)
    return x, jnp.stack(hists, axis=0), k_new, v_new


def kernel(x_prompt, x_sample, state_pool, cache_k, cache_v, norm_a, w_in_a, w_grp_a, scale_a,
           w_out_a, norm_kv, w_kv, g_k, norm_b, w_in_b, g_q, rel_bias_b, w_out_b):
    bf = jnp.bfloat16
    head_of_lane = jnp.arange(MXU_DIM, dtype=jnp.int32) // HEAD_DIM
    gmat = (head_of_lane[:, None] == head_of_lane[None, :]).astype(bf)
    n_b = rel_bias_b.shape[0]
    wide, narrow, w_in_a0, w_grp_bf, w_out_a_bf = _bias_tables(
        rel_bias_b, side_casts=[(w_in_a, 0), (w_grp_a.reshape(-1, G_A), None),
                                (w_out_a.reshape(-1, D_MODEL), None)])
    pair_rows = HEADS_PER_LANE_BLOCK
    w = dict(
        f32=dict(w_in_a=w_in_a, w_kv=w_kv, w_in_b=w_in_b, w_out_b=w_out_b),
        norm_a=norm_a[:, None, :],
        w_in_a=[w_in_a0[None]] + [None] * (w_in_a.shape[0] - 1),
        w_grp_a=w_grp_bf.reshape(w_grp_a.shape), scale_a=scale_a[:, None, :],
        w_out_a=w_out_a_bf.reshape(w_out_a.shape),
        norm_kv=norm_kv[None, :], w_kv=None, gk_t=jnp.tile(g_k, N_HEADS)[None, :],
        norm_b=norm_b[:, None, :], w_in_b=None,
        gq_t=jnp.tile(g_q, (1, N_HEADS))[:, None, :], w_out_b=None, gmat=gmat,
        bias_prompt=wide.reshape(n_b, N_PAIRS, pair_rows * BIAS_Q, BIAS_K),
        bias_sample=narrow.reshape(n_b, N_PAIRS, pair_rows * CHUNK, CHUNK + KV_ROWS),
    )
    bp, sp, _ = x_prompt.shape
    bs, ss, _ = x_sample.shape
    y_p, pool_p, k_p, v_p = _trunk(x_prompt, None, None, None, 0, w,
                                   pool_seg=1, pool_len=min(POOL_TILE, sp), tile=KV_ROWS)
    hist16 = jnp.pad(state_pool, ((0, 0), (0, 0), (HIST_PAD - POOL_HIST, 0), (0, 0)))
    y_s, pool_s, k_s, v_s = _trunk(x_sample, hist16, cache_k, cache_v, PAST_LEN, w,
                                   pool_seg=bs, pool_len=ss, tile=ss)
    lp = min(KV_ROWS, sp)
    return (y_p, y_s, pool_p, pool_s,
            k_p.reshape(bp, lp, N_HEADS, HEAD_DIM), v_p.reshape(bp, lp, N_HEADS, HEAD_DIM),
            k_s.reshape(bs, ss, N_HEADS, HEAD_DIM), v_s.reshape(bs, ss, N_HEADS, HEAD_DIM))
```

```python
import functools
import math

import jax
import jax.numpy as jnp
from jax import lax
from jax.experimental import pallas as pl
from jax.experimental.pallas import tpu as pltpu

D_MODEL = 1024
E_A = 2048
POOL_WINDOWS = (2, 4, 8, 16)
N_POOL_GROUPS = len(POOL_WINDOWS)
G_A = E_A // N_POOL_GROUPS
POOL_HIST = max(POOL_WINDOWS) - 1
HEAD_DIM = 64
N_HEADS = 16
E_B = N_HEADS * HEAD_DIM
CHUNK = 64
N_LEFT_CHUNKS = 8
KV_ROWS = N_LEFT_CHUNKS * CHUNK
MAX_REL = 128
N_REL = 2 * MAX_REL + 1
EPS = 1e-6
NEG_INF = -1e30
PAST_LEN = 4096
LOG2E = math.log2(math.e)

SUBLANES = 8
LANES = 128
MXU_DIM = 256
HEADS_PER_LANE_BLOCK = LANES // HEAD_DIM
N_PAIRS = N_HEADS // HEADS_PER_LANE_BLOCK
HIST_PAD = 2 * SUBLANES
FRONT = SUBLANES
QK_AHEAD = 2
PIECE_COLS = 2 * MXU_DIM
POOL_TILE = 1024
POOL_SUB_TILE = 512
VMEM_LIMIT_BYTES = 56 * 1024 * 1024


def _bdot(a, b):
    return jnp.dot(a, b, preferred_element_type=jnp.float32)


def _rms(x, g):
    ms = jnp.mean(x * x, axis=-1, keepdims=True)
    return x * lax.rsqrt(ms + EPS) * g


def _silu(z):
    return z * (1.0 / (1.0 + jnp.exp(-z)))


def _side_cast_specs(arrays, n_steps, step_of):
    in_specs, out_specs, out_shapes = [], [], []
    for arr, lead in arrays:
        n_rows, n_cols = arr.shape[-2:]
        rows = n_rows // n_steps
        assert rows * n_steps == n_rows and rows % (2 * SUBLANES) == 0
        if lead is None:
            in_specs.append(pl.BlockSpec((rows, n_cols), lambda *i: (step_of(*i), 0)))
        else:
            in_specs.append(pl.BlockSpec((None, rows, n_cols),
                                         lambda *i, lead=lead: (lead, step_of(*i), 0)))
        out_specs.append(pl.BlockSpec((rows, n_cols), lambda *i: (step_of(*i), 0)))
        out_shapes.append(jax.ShapeDtypeStruct((n_rows, n_cols), jnp.bfloat16))
    return in_specs, out_specs, out_shapes


def _pool_layer_kernel(*refs, n_seg, seg_len, sub_len, pos0, has_hist, n_side):
    it = iter(refs)
    x_ref = next(it)
    hist_ref = next(it) if has_hist else None
    g_ref, win_ref, wgrp_ref, scale_ref, wout_ref = (next(it) for _ in range(5))
    side_in = [next(it) for _ in range(n_side)]
    y_ref, hout_ref = next(it), next(it)
    side_out = [next(it) for _ in range(n_side)]
    acc_ref, upad_ref, s0_ref, s1_ref, carry_ref = it
    for src, dst in zip(side_in, side_out):
        dst[...] = src[...].astype(dst.dtype)
    t = pl.program_id(1)
    m = n_seg * sub_len
    rows = HIST_PAD + sub_len

    @pl.when(t == 0)
    def _():
        if has_hist:
            carry_ref[...] = hist_ref[...]
        else:
            carry_ref[...] = jnp.zeros_like(carry_ref)

    zero_front = jnp.zeros((n_seg, FRONT, G_A), jnp.float32)
    upad_ref[:, 0:FRONT, :] = zero_front
    s0_ref[:, 0:FRONT, :] = zero_front
    s1_ref[:, 0:FRONT, :] = zero_front

    for r0 in range(0, seg_len, sub_len):
        x = x_ref[:, r0:r0 + sub_len, :].reshape(m, D_MODEL)
        hb = _rms(x, g_ref[...]).astype(jnp.bfloat16)
        pos = (pos0 + t * seg_len + r0
               + lax.broadcasted_iota(jnp.int32, (1, sub_len, 1), 1))
        for g, w in enumerate(POOL_WINDOWS):
            lo = g * G_A
            u = _bdot(hb, win_ref[:, lo:lo + G_A])
            z = _bdot(hb, win_ref[:, E_A + lo:E_A + lo + G_A])
            u3 = u.reshape(n_seg, sub_len, G_A)
            upad_ref[:, FRONT:FRONT + HIST_PAD, :] = carry_ref[:, :, lo:lo + G_A]
            upad_ref[:, FRONT + HIST_PAD:, :] = u3
            carry_ref[:, :, lo:lo + G_A] = upad_ref[:, FRONT + sub_len:, :]
            bufs = (upad_ref, s0_ref, s1_ref)
            src = 0
            for k in range(g + 1):
                dst = 1 if src != 1 else 2
                sh = 1 << k
                bufs[dst][:, FRONT:, :] = (bufs[src][:, FRONT:, :]
                                           + bufs[src][:, FRONT - sh:FRONT - sh + rows, :])
                src = dst
            wsum = bufs[src][:, FRONT + HIST_PAD:, :]
            inv_cnt = 1.0 / jnp.minimum(pos + 1, w).astype(jnp.float32)
            pooled = (wsum * inv_cnt - u3).reshape(m, G_A)
            pg = _bdot(pooled.astype(jnp.bfloat16), wgrp_ref[g]) * scale_ref[:, lo:lo + G_A]
            yb = (pg * _silu(z)).astype(jnp.bfloat16)
            contrib = _bdot(yb, wout_ref[lo:lo + G_A, :])
            if g == 0:
                acc_ref[...] = contrib
            else:
                acc_ref[...] += contrib
        y_ref[:, r0:r0 + sub_len, :] = (x + acc_ref[...]).reshape(n_seg, sub_len, D_MODEL)
    hout_ref[...] = carry_ref[...]


def _layer_spec(arr, layer):
    tail = (0,) * (arr.ndim - 1)
    return pl.BlockSpec((None,) + arr.shape[1:], lambda *_: (layer,) + tail,
                        pipeline_mode=pl.Buffered(1))


def _pool_layer(x, hist16, layer, g_norm, w_in, w_grp, scale, w_out, *, n_seg, seg_len, pos0,
                side_casts=()):
    s, t_len, _ = x.shape
    has_hist = hist16 is not None
    grid = (s // n_seg, t_len // seg_len)
    in_specs = [pl.BlockSpec((n_seg, seg_len, D_MODEL), lambda b, t: (b, t, 0))]
    args = [x]
    if has_hist:
        in_specs.append(pl.BlockSpec((None, n_seg, HIST_PAD, E_A), lambda b, t: (layer, b, 0, 0)))
        args.append(hist16)
    for arr, idx in (g_norm, w_in, w_grp, scale, w_out):
        args.append(arr)
        in_specs.append(_layer_spec(arr, idx))
    side_in, side_out, side_shapes = _side_cast_specs(
        side_casts, grid[0] * grid[1], lambda b, t: b * grid[1] + t)
    args += [arr for arr, _ in side_casts]
    in_specs += side_in
    sub_len = min(seg_len, POOL_SUB_TILE)
    m = n_seg * sub_len
    stage = pltpu.VMEM((n_seg, FRONT + HIST_PAD + sub_len, G_A), jnp.float32)
    return pl.pallas_call(
        functools.partial(_pool_layer_kernel, n_seg=n_seg, seg_len=seg_len, sub_len=sub_len,
                          pos0=pos0, has_hist=has_hist, n_side=len(side_casts)),
        out_shape=(jax.ShapeDtypeStruct(x.shape, jnp.float32),
                   jax.ShapeDtypeStruct((s, HIST_PAD, E_A), jnp.float32), *side_shapes),
        grid=grid,
        in_specs=in_specs,
        out_specs=(pl.BlockSpec((n_seg, seg_len, D_MODEL), lambda b, t: (b, t, 0)),
                   pl.BlockSpec((n_seg, HIST_PAD, E_A), lambda b, t: (b, 0, 0)), *side_out),
        scratch_shapes=[pltpu.VMEM((m, D_MODEL), jnp.float32), stage, stage, stage,
                        pltpu.VMEM((n_seg, HIST_PAD, E_A), jnp.float32)],
        compiler_params=pltpu.CompilerParams(
            dimension_semantics=("arbitrary", "arbitrary"),
            vmem_limit_bytes=VMEM_LIMIT_BYTES),
        name="pool_layer",
    )(*args)


def _head_rms(v, gmat_ref, gain):
    sq = (v * v).astype(jnp.bfloat16)
    ss = jnp.concatenate([_bdot(sq[:, c:c + MXU_DIM], gmat_ref[...])
                          for c in range(0, v.shape[-1], MXU_DIM)], axis=1)
    return v * lax.rsqrt(ss * (1.0 / HEAD_DIM) + EPS) * gain


def _kv_kernel(*refs, tm, n_side):
    it = iter(refs)
    x_ref, g_ref, wkv_ref, gk_ref, gmat_ref = (next(it) for _ in range(5))
    side_in = [next(it) for _ in range(n_side)]
    kout_ref, vout_ref, kt_ref, vb_ref = (next(it) for _ in range(4))
    for src, dst in zip(side_in, it):
        dst[...] = src[...].astype(dst.dtype)
    keep = kout_ref.shape[1]
    sub = min(tm, keep)
    for r0 in range(0, tm, sub):
        x = x_ref[0, r0:r0 + sub, :]
        hb = _rms(x, g_ref[...]).astype(jnp.bfloat16)
        k = _bdot(hb, wkv_ref[:, 0:E_B])
        kn = _head_rms(k, gmat_ref, gk_ref[...])
        v = _bdot(hb, wkv_ref[:, E_B:2 * E_B])
        if r0 + sub == tm:
            kout_ref[0] = kn
            vout_ref[0] = v
        for p in range(N_PAIRS):
            blk = kn[:, p * LANES:(p + 1) * LANES]
            if sub < LANES:
                blk = jnp.concatenate([blk, jnp.zeros((LANES - sub, LANES), jnp.float32)],
                                      axis=0)
                kt_ref[0, p] = blk.T.astype(jnp.bfloat16)
            else:
                kt_ref[0, p, :, r0:r0 + sub] = blk.T.astype(jnp.bfloat16)
            vb_ref[0, p, r0:r0 + sub, :] = v[:, p * LANES:(p + 1) * LANES].astype(jnp.bfloat16)


def _kv_proj(x, g_kv, w_kv, gk_t, gmat, *, tm, side_casts=()):
    s, t_len, _ = x.shape
    n_t = t_len // tm
    keep = min(KV_ROWS, t_len)
    assert tm % keep == 0
    kt_w = max(tm, LANES)
    const2 = lambda b, t: (0, 0)
    in_specs = [pl.BlockSpec((1, tm, D_MODEL), lambda b, t: (b, t, 0)),
                pl.BlockSpec((1, D_MODEL), const2),
                pl.BlockSpec((D_MODEL, 2 * E_B), const2),
                pl.BlockSpec((1, E_B), const2),
                pl.BlockSpec((MXU_DIM, MXU_DIM), const2)]
    args = [x, g_kv, w_kv, gk_t, gmat]
    out_shape = [jax.ShapeDtypeStruct((s, keep, E_B), jnp.float32),
                 jax.ShapeDtypeStruct((s, keep, E_B), jnp.float32),
                 jax.ShapeDtypeStruct((s, N_PAIRS, LANES, n_t * kt_w), jnp.bfloat16),
                 jax.ShapeDtypeStruct((s, N_PAIRS, t_len, LANES), jnp.bfloat16)]
    out_specs = [pl.BlockSpec((1, keep, E_B), lambda b, t: (b, 0, 0)),
                 pl.BlockSpec((1, keep, E_B), lambda b, t: (b, 0, 0)),
                 pl.BlockSpec((1, N_PAIRS, LANES, kt_w), lambda b, t: (b, 0, 0, t)),
                 pl.BlockSpec((1, N_PAIRS, tm, LANES), lambda b, t: (b, 0, t, 0))]
    side_in, side_out, side_shapes = _side_cast_specs(side_casts, s * n_t,
                                                      lambda b, t: b * n_t + t)
    args += [arr for arr, _ in side_casts]
    in_specs += side_in
    out_shape += side_shapes
    out_specs += side_out
    return pl.pallas_call(
        functools.partial(_kv_kernel, tm=tm, n_side=len(side_casts)),
        out_shape=tuple(out_shape),
        grid=(s, n_t),
        in_specs=in_specs,
        out_specs=tuple(out_specs),
        compiler_params=pltpu.CompilerParams(
            dimension_semantics=("arbitrary", "arbitrary"),
            vmem_limit_bytes=VMEM_LIMIT_BYTES),
        name="kv_proj",
    )(*args)


BIAS_Q = 2 * CHUNK
BIAS_K = BIAS_Q + KV_ROWS
BIAS_BASE = 1024


BIAS_HEADS_PER_STEP = 4


def _bias_kernel(base_ref, *refs, n_side):
    side_in = refs[:n_side]
    wide_ref, narrow_ref = refs[n_side:n_side + 2]
    for src, dst in zip(side_in, refs[n_side + 2:]):
        dst[...] = src[...].astype(dst.dtype)
    qc = lax.broadcasted_iota(jnp.int32, (BIAS_Q, BIAS_K), 0) // CHUNK
    kc = lax.broadcasted_iota(jnp.int32, (BIAS_Q, BIAS_K), 1) // CHUNK
    valid = (kc >= qc) & (kc <= qc + N_LEFT_CHUNKS)
    for h in range(BIAS_HEADS_PER_STEP):
        base = base_ref[h] * LOG2E
        tiled = jnp.broadcast_to(base, (BIAS_Q, BIAS_BASE))
        toep = pltpu.roll(tiled, 0, 1, stride=1, stride_axis=0)[:, 0:BIAS_K]
        table = jnp.where(valid, toep, NEG_INF)
        wide_ref[h] = table
        narrow_ref[h] = table[0:CHUNK, 0:CHUNK + KV_ROWS]


def _bias_tables(rel_bias, side_casts=()):
    n = rel_bias.shape[0] * N_HEADS
    f = jnp.transpose(rel_bias, (0, 2, 1)).reshape(n, N_REL)
    rev = f[:, ::-1]
    far = f[:, N_REL - 1:N_REL]
    n_left = KV_ROWS - MAX_REL
    base = jnp.concatenate(
        [jnp.broadcast_to(far, (n, n_left)), rev,
         jnp.broadcast_to(far, (n, BIAS_BASE - n_left - N_REL))], axis=1)
    base = base.reshape(n, 1, BIAS_BASE)
    hs = BIAS_HEADS_PER_STEP
    side_in, side_out, side_shapes = _side_cast_specs(side_casts, n // hs, lambda h: h)
    return pl.pallas_call(
        functools.partial(_bias_kernel, n_side=len(side_casts)),
        out_shape=(jax.ShapeDtypeStruct((n, BIAS_Q, BIAS_K), jnp.float32),
                   jax.ShapeDtypeStruct((n, CHUNK, CHUNK + KV_ROWS), jnp.float32), *side_shapes),
        grid=(n // hs,),
        in_specs=[pl.BlockSpec((hs, 1, BIAS_BASE), lambda h: (h, 0, 0)), *side_in],
        out_specs=(pl.BlockSpec((hs, BIAS_Q, BIAS_K), lambda h: (h, 0, 0)),
                   pl.BlockSpec((hs, CHUNK, CHUNK + KV_ROWS), lambda h: (h, 0, 0)), *side_out),
        compiler_params=pltpu.CompilerParams(dimension_semantics=("arbitrary",),
                                             vmem_limit_bytes=VMEM_LIMIT_BYTES),
        name="bias_table",
    )(base, *[arr for arr, _ in side_casts])


def _attn_layer_kernel(*refs, tm, qg):
    tile = functools.partial(_attn_tile, *refs, tm=tm, qg=qg)
    first = pl.program_id(1) == 0
    pl.when(first)(functools.partial(tile, has_a=False))
    pl.when(jnp.logical_not(first))(functools.partial(tile, has_a=True))


def _attn_tile(x_ref, kta_ref, ktb_ref, va_ref, vb_ref, bias_ref,
               g_ref, win_ref, gq_ref, gmat_ref, wout_ref,
               y_ref, q_ref, o_ref, z_ref, *, tm, qg, has_a):
    n_grp = tm // qg
    cb = PIECE_COLS
    n_cb = E_B // cb
    pairs_per_cb = cb // LANES
    x = x_ref[0]
    hb = _rms(x, g_ref[...]).astype(jnp.bfloat16)
    q_gain = gq_ref[...] * (HEAD_DIM ** -0.5 * LOG2E)

    def rows_of(j):
        return slice(j * qg, (j + 1) * qg)

    def q_piece(c):
        cols = slice(c * cb, (c + 1) * cb)
        qn = _head_rms(_bdot(hb, win_ref[:, cols]), gmat_ref, q_gain[:, cols])
        for h in range(pairs_per_cb):
            q_ref[c * pairs_per_cb + h] = qn[:, h * LANES:(h + 1) * LANES].astype(jnp.bfloat16)

    def z_piece(c):
        cols = slice(c * cb, (c + 1) * cb)
        z_ref[:, cols] = _bdot(hb, win_ref[:, E_B + c * cb:E_B + (c + 1) * cb])

    def out_piece(c):
        cols = slice(c * cb, (c + 1) * cb)
        o = jnp.concatenate([o_ref[c * pairs_per_cb + h] for h in range(pairs_per_cb)],
                            axis=1)
        yb = (o * _silu(z_ref[:, cols])).astype(jnp.bfloat16)
        contrib = _bdot(yb, wout_ref[cols, :])
        if c == 0:
            y_ref[0] = x + contrib
        else:
            y_ref[0] += contrib

    lane_lo = lax.broadcasted_iota(jnp.int32, (1, LANES), 1) < HEAD_DIM
    widths = [(KV_ROWS - j * qg, (j + 1) * qg) for j in range(n_grp)]

    def scores(p, j):
        wa, wbj = widths[j]
        qb = q_ref[p, j * qg:(j + 1) * qg, :]
        zero = jnp.zeros_like(qb)
        lhs = jnp.concatenate([jnp.where(lane_lo, qb, zero),
                               jnp.where(lane_lo, zero, qb)], axis=0)
        s = _bdot(lhs, ktb_ref[0, p, :, 0:wbj])
        if has_a:
            sa = _bdot(lhs, kta_ref[0, p, :, KV_ROWS - wa:KV_ROWS])
            return jnp.concatenate([sa, s], axis=1) + bias_ref[p]
        return s + bias_ref[p, :, wa:wa + wbj]

    def attend(p, j, s):
        wa, wbj = widths[j]
        mx = jnp.max(s, axis=1, keepdims=True)
        e = jnp.exp2(s - mx)
        l = jnp.sum(e, axis=1, keepdims=True)
        pb = e.astype(jnp.bfloat16)
        if has_a:
            o2 = (_bdot(pb[:, 0:wa], va_ref[0, p, KV_ROWS - wa:KV_ROWS, :])
                  + _bdot(pb[:, wa:wa + wbj], vb_ref[0, p, 0:wbj, :]))
        else:
            o2 = _bdot(pb, vb_ref[0, p, 0:wbj, :])
        o2 = o2 * (1.0 / l)
        o_ref[p, rows_of(j), :] = jnp.where(lane_lo, o2[0:qg], o2[qg:2 * qg])

    items = [(c * pairs_per_cb + h, j) for c in range(n_cb) for j in range(n_grp)
             for h in range(pairs_per_cb)]
    per_cb = len(items) // n_cb
    pieces = []
    for c in range(n_cb):
        if c > 0:
            pieces.append((per_cb * c - QK_AHEAD - 1, 0, len(pieces), q_piece, c))
        pieces.append((per_cb * c + per_cb // 2, 0, len(pieces), z_piece, c))
        if c + 1 < n_cb:
            pieces.append((per_cb * (c + 1) + per_cb // 2, per_cb * (c + 1), len(pieces),
                           out_piece, c))
    n_pieces = len(pieces)
    q_piece(0)
    pending = {}
    for i in range(min(QK_AHEAD, len(items))):
        pending[i] = scores(*items[i])
    issued = 0
    for i in range(len(items)):
        quota = ((i + 1) * n_pieces) // len(items) - issued
        ready = sorted(pc for pc in pieces if pc[1] <= i)
        take = [pc for pc in ready if pc[0] <= i]
        take += [pc for pc in ready if pc[0] > i][:max(0, quota - len(take))]
        for pc in take:
            pieces.remove(pc)
            pc[3](pc[4])
            issued += 1
        if i + QK_AHEAD < len(items):
            pending[i + QK_AHEAD] = scores(*items[i + QK_AHEAD])
        attend(*items[i], pending.pop(i))
    for pc in sorted(pieces):
        pc[3](pc[4])
    out_piece(n_cb - 1)


def _attn_layer(x, kt, v, bias, g_norm, w_in, gq_t, gmat, w_out, *, layer, tm, qg):
    s, t_len, _ = x.shape
    n_t = t_len // tm
    const2 = lambda b, t: (0, 0)
    assert tm == KV_ROWS and kt.shape[-1] == t_len
    prev = lambda t: jnp.maximum(t - 1, 0)
    in_specs = [
        pl.BlockSpec((1, tm, D_MODEL), lambda b, t: (b, t, 0)),
        pl.BlockSpec((1, N_PAIRS, LANES, KV_ROWS), lambda b, t: (b, 0, 0, prev(t))),
        pl.BlockSpec((1, N_PAIRS, LANES, tm), lambda b, t: (b, 0, 0, t)),
        pl.BlockSpec((1, N_PAIRS, KV_ROWS, LANES), lambda b, t: (b, 0, prev(t), 0)),
        pl.BlockSpec((1, N_PAIRS, tm, LANES), lambda b, t: (b, 0, t, 0)),
        _layer_spec(bias, layer),
        _layer_spec(g_norm, layer),
        _layer_spec(w_in, layer),
        _layer_spec(gq_t, layer),
        pl.BlockSpec((MXU_DIM, MXU_DIM), const2),
        _layer_spec(w_out, layer),
    ]
    return pl.pallas_call(
        functools.partial(_attn_layer_kernel, tm=tm, qg=qg),
        out_shape=jax.ShapeDtypeStruct(x.shape, jnp.float32),
        grid=(s, n_t),
        in_specs=in_specs,
        out_specs=pl.BlockSpec((1, tm, D_MODEL), lambda b, t: (b, t, 0)),
        scratch_shapes=[pltpu.VMEM((N_PAIRS, tm, LANES), jnp.bfloat16),
                        pltpu.VMEM((N_PAIRS, tm, LANES), jnp.float32),
                        pltpu.VMEM((tm, E_B), jnp.float32)],
        compiler_params=pltpu.CompilerParams(
            dimension_semantics=("arbitrary", "arbitrary"),
            vmem_limit_bytes=VMEM_LIMIT_BYTES),
        name="attn_layer",
    )(x, kt, kt, v, v, bias, g_norm, w_in, gq_t, gmat, w_out)


def _attn_cached_kernel(x_ref, ktc_ref, ktn_ref, vtc_ref, vn_ref, bias_ref,
                        g_ref, win_ref, gq_ref, gmat_ref, wout_ref,
                        y_ref, q_ref, o_ref, z_ref, *, n_seq, qg):
    b = pl.program_id(0)
    lane_lo = lax.broadcasted_iota(jnp.int32, (1, LANES), 1) < HEAD_DIM
    contract_last = (((1,), (1,)), ((), ()))

    @pl.when(b == 0)
    def _():
        hb = _rms(x_ref[...], g_ref[...]).astype(jnp.bfloat16)
        q_gain = gq_ref[...] * (HEAD_DIM ** -0.5 * LOG2E)
        qn = _head_rms(_bdot(hb, win_ref[:, 0:E_B]), gmat_ref, q_gain)
        for p in range(N_PAIRS):
            q_ref[p] = qn[:, p * LANES:(p + 1) * LANES].astype(jnp.bfloat16)
        z_ref[...] = _bdot(hb, win_ref[:, E_B:2 * E_B])

    rows = pl.ds(pl.multiple_of(b * qg, qg), qg)

    def scores(p):
        qb = q_ref[p, rows, :]
        zero = jnp.zeros_like(qb)
        lhs = jnp.concatenate([jnp.where(lane_lo, qb, zero),
                               jnp.where(lane_lo, zero, qb)], axis=0)
        sa = _bdot(lhs, ktc_ref[0, p].astype(jnp.bfloat16))
        sb = _bdot(lhs, ktn_ref[0, p, :, 0:qg])
        return jnp.concatenate([sa, sb], axis=1) + bias_ref[p]

    def attend(p, s):
        keys = vtc_ref.shape[-1]
        mx = jnp.max(s, axis=1, keepdims=True)
        e = jnp.exp2(s - mx)
        l = jnp.sum(e, axis=1, keepdims=True)
        pb = e.astype(jnp.bfloat16)
        oa = lax.dot_general(pb[:, 0:keys], vtc_ref[0, p].astype(jnp.bfloat16), contract_last,
                             preferred_element_type=jnp.float32)
        o2 = (oa + _bdot(pb[:, keys:keys + qg], vn_ref[0, p])) * (1.0 / l)
        o_ref[p, rows, :] = jnp.where(lane_lo, o2[0:qg], o2[qg:2 * qg])

    pending = {p: scores(p) for p in range(min(QK_AHEAD, N_PAIRS))}
    for p in range(N_PAIRS):
        if p + QK_AHEAD < N_PAIRS:
            pending[p + QK_AHEAD] = scores(p + QK_AHEAD)
        attend(p, pending.pop(p))

    @pl.when(b == n_seq - 1)
    def _():
        o = jnp.concatenate([o_ref[p] for p in range(N_PAIRS)], axis=1)
        yb = (o * _silu(z_ref[...])).astype(jnp.bfloat16)
        y_ref[...] = x_ref[...] + _bdot(yb, wout_ref[...])


def _attn_layer_cached(x, kt_cache, kt_new, vt_cache, v_new, bias, g_norm, w_in, gq_t, gmat,
                       w_out, *, layer):
    s, qg, _ = x.shape
    m = s * qg
    keys = kt_cache.shape[-1]
    whole = pl.BlockSpec((m, D_MODEL), lambda b: (0, 0))
    per_seq = lambda a: pl.BlockSpec((1,) + a.shape[1:], lambda b: (b, 0, 0, 0))
    in_specs = [whole, per_seq(kt_cache), per_seq(kt_new), per_seq(vt_cache), per_seq(v_new),
                _layer_spec(bias, layer), _layer_spec(g_norm, layer), _layer_spec(w_in, layer),
                _layer_spec(gq_t, layer), pl.BlockSpec((MXU_DIM, MXU_DIM), lambda b: (0, 0)),
                _layer_spec(w_out, layer)]
    assert bias.shape[-1] == keys + qg
    y = pl.pallas_call(
        functools.partial(_attn_cached_kernel, n_seq=s, qg=qg),
        out_shape=jax.ShapeDtypeStruct((m, D_MODEL), jnp.float32),
        grid=(s,),
        in_specs=in_specs,
        out_specs=whole,
        scratch_shapes=[pltpu.VMEM((N_PAIRS, m, LANES), jnp.bfloat16),
                        pltpu.VMEM((N_PAIRS, m, LANES), jnp.float32),
                        pltpu.VMEM((m, E_B), jnp.float32)],
        compiler_params=pltpu.CompilerParams(
            dimension_semantics=("arbitrary",), vmem_limit_bytes=VMEM_LIMIT_BYTES),
        name="attn_cached",
    )(x.reshape(m, D_MODEL), kt_cache, kt_new, vt_cache, v_new, bias, g_norm, w_in, gq_t, gmat,
      w_out)
    return y.reshape(x.shape)


def _trunk(x, hist16, cache_k, cache_v, pos0, w, *, pool_seg, pool_len, tile):
    s, t_len, _ = x.shape
    hists = []
    n_a = len(w["w_in_a"])
    for layer in range(n_a):
        side = []
        cast_next = layer + 1 < n_a and w["w_in_a"][layer + 1] is None
        cast_kv = layer + 1 == n_a and w["w_kv"] is None
        if cast_next:
            side.append((w["f32"]["w_in_a"], layer + 1))
        if cast_kv:
            side.append((w["f32"]["w_kv"], None))
        x, h_out, *cast = _pool_layer(
            x, hist16, layer, (w["norm_a"], layer), (w["w_in_a"][layer], 0),
            (w["w_grp_a"], layer), (w["scale_a"], layer), (w["w_out_a"], layer),
            n_seg=pool_seg, seg_len=pool_len, pos0=pos0, side_casts=side)
        if cast_next:
            w["w_in_a"][layer + 1] = cast.pop(0)[None]
        if cast_kv:
            w["w_kv"] = cast.pop(0)
        hists.append(h_out[:, HIST_PAD - POOL_HIST:, :])
    side = []
    if w["w_in_b"] is None:
        side = [(w["f32"][name].reshape(-1, w["f32"][name].shape[-1]), None)
                for name in ("w_in_b", "w_out_b")]
    k_new, v_new, kt, vb, *cast = _kv_proj(x, w["norm_kv"], w["w_kv"], w["gk_t"], w["gmat"],
                                           tm=min(pool_len, t_len), side_casts=side)
    if side:
        w["w_in_b"] = cast[0].reshape(w["f32"]["w_in_b"].shape)
        w["w_out_b"] = cast[1].reshape(w["f32"]["w_out_b"].shape)
    weights = (w["norm_b"], w["w_in_b"], w["gq_t"], w["gmat"], w["w_out_b"])
    if cache_k is None:
        for j in range(w["w_in_b"].shape[0]):
            x = _attn_layer(x, kt, vb, w["bias_prompt"], *weights, layer=j, tm=tile, qg=BIAS_Q)
    else:
        lc = cache_k.shape[1]
        as_pairs = lambda c: jnp.transpose(c, (0, 2, 3, 1)).reshape(s, N_PAIRS, LANES, lc)
        for j in range(w["w_in_b"].shape[0]):
            x = _attn_layer_cached(x, as_pairs(cache_k), kt, as_pairs(cache_v), vb,
                                   w["bias_sample"], *weights, layer=j)
    return x, jnp.stack(hists, axis=0), k_new, v_new


def kernel(x_prompt, x_sample, state_pool, cache_k, cache_v, norm_a, w_in_a, w_grp_a, scale_a,
           w_out_a, norm_kv, w_kv, g_k, norm_b, w_in_b, g_q, rel_bias_b, w_out_b):
    bf = jnp.bfloat16
    head_of_lane = jnp.arange(MXU_DIM, dtype=jnp.int32) // HEAD_DIM
    gmat = (head_of_lane[:, None] == head_of_lane[None, :]).astype(bf)
    n_b = rel_bias_b.shape[0]
    wide, narrow, w_in_a0, w_grp_bf, w_out_a_bf = _bias_tables(
        rel_bias_b, side_casts=[(w_in_a, 0), (w_grp_a.reshape(-1, G_A), None),
                                (w_out_a.reshape(-1, D_MODEL), None)])
    pair_rows = HEADS_PER_LANE_BLOCK
    w = dict(
        f32=dict(w_in_a=w_in_a, w_kv=w_kv, w_in_b=w_in_b, w_out_b=w_out_b),
        norm_a=norm_a[:, None, :],
        w_in_a=[w_in_a0[None]] + [None] * (w_in_a.shape[0] - 1),
        w_grp_a=w_grp_bf.reshape(w_grp_a.shape), scale_a=scale_a[:, None, :],
        w_out_a=w_out_a_bf.reshape(w_out_a.shape),
        norm_kv=norm_kv[None, :], w_kv=None, gk_t=jnp.tile(g_k, N_HEADS)[None, :],
        norm_b=norm_b[:, None, :], w_in_b=None,
        gq_t=jnp.tile(g_q, (1, N_HEADS))[:, None, :], w_out_b=None, gmat=gmat,
        bias_prompt=wide.reshape(n_b, N_PAIRS, pair_rows * BIAS_Q, BIAS_K),
        bias_sample=narrow.reshape(n_b, N_PAIRS, pair_rows * CHUNK, CHUNK + KV_ROWS),
    )
    bp, sp, _ = x_prompt.shape
    bs, ss, _ = x_sample.shape
    y_p, pool_p, k_p, v_p = _trunk(x_prompt, None, None, None, 0, w,
                                   pool_seg=1, pool_len=min(POOL_TILE, sp), tile=KV_ROWS)
    hist16 = jnp.pad(state_pool, ((0, 0), (0, 0), (HIST_PAD - POOL_HIST, 0), (0, 0)))
    y_s, pool_s, k_s, v_s = _trunk(x_sample, hist16, cache_k, cache_v, PAST_LEN, w,
                                   pool_seg=bs, pool_len=ss, tile=ss)
    lp = min(KV_ROWS, sp)
    return (y_p, y_s, pool_p, pool_s,
            k_p.reshape(bp, lp, N_HEADS, HEAD_DIM), v_p.reshape(bp, lp, N_HEADS, HEAD_DIM),
            k_s.reshape(bs, ss, N_HEADS, HEAD_DIM), v_s.reshape(bs, ss, N_HEADS, HEAD_DIM))
```

```python
import functools
import math

import jax
import jax.numpy as jnp
from jax import lax
from jax.experimental import pallas as pl
from jax.experimental.pallas import tpu as pltpu

D_MODEL = 1024
E_A = 2048
POOL_WINDOWS = (2, 4, 8, 16)
N_POOL_GROUPS = len(POOL_WINDOWS)
G_A = E_A // N_POOL_GROUPS
POOL_HIST = max(POOL_WINDOWS) - 1
HEAD_DIM = 64
N_HEADS = 16
E_B = N_HEADS * HEAD_DIM
CHUNK = 64
N_LEFT_CHUNKS = 8
KV_ROWS = N_LEFT_CHUNKS * CHUNK
MAX_REL = 128
N_REL = 2 * MAX_REL + 1
EPS = 1e-6
NEG_INF = -1e30
PAST_LEN = 4096
LOG2E = math.log2(math.e)

SUBLANES = 8
LANES = 128
MXU_DIM = 256
HEADS_PER_LANE_BLOCK = LANES // HEAD_DIM
N_PAIRS = N_HEADS // HEADS_PER_LANE_BLOCK
HIST_PAD = 2 * SUBLANES
FRONT = SUBLANES
QK_AHEAD = 2
PIECE_COLS = 2 * MXU_DIM
POOL_TILE = 1024
POOL_SUB_TILE = 512
VMEM_LIMIT_BYTES = 56 * 1024 * 1024


def _bdot(a, b):
    return jnp.dot(a, b, preferred_element_type=jnp.float32)


def _rms(x, g):
    ms = jnp.mean(x * x, axis=-1, keepdims=True)
    return x * lax.rsqrt(ms + EPS) * g


def _silu(z):
    return z * (1.0 / (1.0 + jnp.exp(-z)))


def _side_cast_specs(arrays, n_steps, step_of):
    in_specs, out_specs, out_shapes = [], [], []
    for arr, lead in arrays:
        n_rows, n_cols = arr.shape[-2:]
        rows = n_rows // n_steps
        assert rows * n_steps == n_rows and rows % (2 * SUBLANES) == 0
        if lead is None:
            in_specs.append(pl.BlockSpec((rows, n_cols), lambda *i: (step_of(*i), 0)))
        else:
            in_specs.append(pl.BlockSpec((None, rows, n_cols),
                                         lambda *i, lead=lead: (lead, step_of(*i), 0)))
        out_specs.append(pl.BlockSpec((rows, n_cols), lambda *i: (step_of(*i), 0)))
        out_shapes.append(jax.ShapeDtypeStruct((n_rows, n_cols), jnp.bfloat16))
    return in_specs, out_specs, out_shapes


def _pool_layer_kernel(*refs, n_seg, seg_len, sub_len, pos0, has_hist, n_side):
    it = iter(refs)
    x_ref = next(it)
    hist_ref = next(it) if has_hist else None
    g_ref, win_ref, wgrp_ref, scale_ref, wout_ref = (next(it) for _ in range(5))
    side_in = [next(it) for _ in range(n_side)]
    y_ref, hout_ref = next(it), next(it)
    side_out = [next(it) for _ in range(n_side)]
    acc_ref, upad_ref, s0_ref, s1_ref, carry_ref = it
    for src, dst in zip(side_in, side_out):
        dst[...] = src[...].astype(dst.dtype)
    t = pl.program_id(1)
    m = n_seg * sub_len
    rows = HIST_PAD + sub_len

    @pl.when(t == 0)
    def _():
        if has_hist:
            carry_ref[...] = hist_ref[...]
        else:
            carry_ref[...] = jnp.zeros_like(carry_ref)

    zero_front = jnp.zeros((n_seg, FRONT, G_A), jnp.float32)
    upad_ref[:, 0:FRONT, :] = zero_front
    s0_ref[:, 0:FRONT, :] = zero_front
    s1_ref[:, 0:FRONT, :] = zero_front

    for r0 in range(0, seg_len, sub_len):
        x = x_ref[:, r0:r0 + sub_len, :].reshape(m, D_MODEL)
        hb = _rms(x, g_ref[...]).astype(jnp.bfloat16)
        pos = (pos0 + t * seg_len + r0
               + lax.broadcasted_iota(jnp.int32, (1, sub_len, 1), 1))
        for g, w in enumerate(POOL_WINDOWS):
            lo = g * G_A
            u = _bdot(hb, win_ref[:, lo:lo + G_A])
            z = _bdot(hb, win_ref[:, E_A + lo:E_A + lo + G_A])
            u3 = u.reshape(n_seg, sub_len, G_A)
            upad_ref[:, FRONT:FRONT + HIST_PAD, :] = carry_ref[:, :, lo:lo + G_A]
            upad_ref[:, FRONT + HIST_PAD:, :] = u3
            carry_ref[:, :, lo:lo + G_A] = upad_ref[:, FRONT + sub_len:, :]
            bufs = (upad_ref, s0_ref, s1_ref)
            src = 0
            for k in range(g + 1):
                dst = 1 if src != 1 else 2
                sh = 1 << k
                bufs[dst][:, FRONT:, :] = (bufs[src][:, FRONT:, :]
                                           + bufs[src][:, FRONT - sh:FRONT - sh + rows, :])
                src = dst
            wsum = bufs[src][:, FRONT + HIST_PAD:, :]
            inv_cnt = 1.0 / jnp.minimum(pos + 1, w).astype(jnp.float32)
            pooled = (wsum * inv_cnt - u3).reshape(m, G_A)
            pg = _bdot(pooled.astype(jnp.bfloat16), wgrp_ref[g]) * scale_ref[:, lo:lo + G_A]
            yb = (pg * _silu(z)).astype(jnp.bfloat16)
            contrib = _bdot(yb, wout_ref[lo:lo + G_A, :])
            if g == 0:
                acc_ref[...] = contrib
            else:
                acc_ref[...] += contrib
        y_ref[:, r0:r0 + sub_len, :] = (x + acc_ref[...]).reshape(n_seg, sub_len, D_MODEL)
    hout_ref[...] = carry_ref[...]


def _layer_spec(arr, layer):
    tail = (0,) * (arr.ndim - 1)
    return pl.BlockSpec((None,) + arr.shape[1:], lambda *_: (layer,) + tail,
                        pipeline_mode=pl.Buffered(1))


def _pool_layer(x, hist16, layer, g_norm, w_in, w_grp, scale, w_out, *, n_seg, seg_len, pos0,
                side_casts=()):
    s, t_len, _ = x.shape
    has_hist = hist16 is not None
    grid = (s // n_seg, t_len // seg_len)
    in_specs = [pl.BlockSpec((n_seg, seg_len, D_MODEL), lambda b, t: (b, t, 0))]
    args = [x]
    if has_hist:
        in_specs.append(pl.BlockSpec((None, n_seg, HIST_PAD, E_A), lambda b, t: (layer, b, 0, 0)))
        args.append(hist16)
    for arr, idx in (g_norm, w_in, w_grp, scale, w_out):
        args.append(arr)
        in_specs.append(_layer_spec(arr, idx))
    side_in, side_out, side_shapes = _side_cast_specs(
        side_casts, grid[0] * grid[1], lambda b, t: b * grid[1] + t)
    args += [arr for arr, _ in side_casts]
    in_specs += side_in
    sub_len = min(seg_len, POOL_SUB_TILE)
    m = n_seg * sub_len
    stage = pltpu.VMEM((n_seg, FRONT + HIST_PAD + sub_len, G_A), jnp.float32)
    return pl.pallas_call(
        functools.partial(_pool_layer_kernel, n_seg=n_seg, seg_len=seg_len, sub_len=sub_len,
                          pos0=pos0, has_hist=has_hist, n_side=len(side_casts)),
        out_shape=(jax.ShapeDtypeStruct(x.shape, jnp.float32),
                   jax.ShapeDtypeStruct((s, HIST_PAD, E_A), jnp.float32), *side_shapes),
        grid=grid,
        in_specs=in_specs,
        out_specs=(pl.BlockSpec((n_seg, seg_len, D_MODEL), lambda b, t: (b, t, 0)),
                   pl.BlockSpec((n_seg, HIST_PAD, E_A), lambda b, t: (b, 0, 0)), *side_out),
        scratch_shapes=[pltpu.VMEM((m, D_MODEL), jnp.float32), stage, stage, stage,
                        pltpu.VMEM((n_seg, HIST_PAD, E_A), jnp.float32)],
        compiler_params=pltpu.CompilerParams(
            dimension_semantics=("arbitrary", "arbitrary"),
            vmem_limit_bytes=VMEM_LIMIT_BYTES),
        name="pool_layer",
    )(*args)


def _head_rms(v, gmat_ref, gain):
    sq = (v * v).astype(jnp.bfloat16)
    ss = jnp.concatenate([_bdot(sq[:, c:c + MXU_DIM], gmat_ref[...])
                          for c in range(0, v.shape[-1], MXU_DIM)], axis=1)
    return v * lax.rsqrt(ss * (1.0 / HEAD_DIM) + EPS) * gain


def _kv_kernel(*refs, tm, n_side):
    it = iter(refs)
    x_ref, g_ref, wkv_ref, gk_ref, gmat_ref = (next(it) for _ in range(5))
    side_in = [next(it) for _ in range(n_side)]
    kout_ref, vout_ref, kt_ref, vb_ref = (next(it) for _ in range(4))
    for src, dst in zip(side_in, it):
        dst[...] = src[...].astype(dst.dtype)
    keep = kout_ref.shape[1]
    sub = min(tm, keep)
    for r0 in range(0, tm, sub):
        x = x_ref[0, r0:r0 + sub, :]
        hb = _rms(x, g_ref[...]).astype(jnp.bfloat16)
        k = _bdot(hb, wkv_ref[:, 0:E_B])
        kn = _head_rms(k, gmat_ref, gk_ref[...])
        v = _bdot(hb, wkv_ref[:, E_B:2 * E_B])
        if r0 + sub == tm:
            kout_ref[0] = kn
            vout_ref[0] = v
        for p in range(N_PAIRS):
            blk = kn[:, p * LANES:(p + 1) * LANES]
            if sub < LANES:
                blk = jnp.concatenate([blk, jnp.zeros((LANES - sub, LANES), jnp.float32)],
                                      axis=0)
                kt_ref[0, p] = blk.T.astype(jnp.bfloat16)
            else:
                kt_ref[0, p, :, r0:r0 + sub] = blk.T.astype(jnp.bfloat16)
            vb_ref[0, p, r0:r0 + sub, :] = v[:, p * LANES:(p + 1) * LANES].astype(jnp.bfloat16)


def _kv_proj(x, g_kv, w_kv, gk_t, gmat, *, tm, side_casts=()):
    s, t_len, _ = x.shape
    n_t = t_len // tm
    keep = min(KV_ROWS, t_len)
    assert tm % keep == 0
    kt_w = max(tm, LANES)
    const2 = lambda b, t: (0, 0)
    in_specs = [pl.BlockSpec((1, tm, D_MODEL), lambda b, t: (b, t, 0)),
                pl.BlockSpec((1, D_MODEL), const2),
                pl.BlockSpec((D_MODEL, 2 * E_B), const2),
                pl.BlockSpec((1, E_B), const2),
                pl.BlockSpec((MXU_DIM, MXU_DIM), const2)]
    args = [x, g_kv, w_kv, gk_t, gmat]
    out_shape = [jax.ShapeDtypeStruct((s, keep, E_B), jnp.float32),
                 jax.ShapeDtypeStruct((s, keep, E_B), jnp.float32),
                 jax.ShapeDtypeStruct((s, N_PAIRS, LANES, n_t * kt_w), jnp.bfloat16),
                 jax.ShapeDtypeStruct((s, N_PAIRS, t_len, LANES), jnp.bfloat16)]
    out_specs = [pl.BlockSpec((1, keep, E_B), lambda b, t: (b, 0, 0)),
                 pl.BlockSpec((1, keep, E_B), lambda b, t: (b, 0, 0)),
                 pl.BlockSpec((1, N_PAIRS, LANES, kt_w), lambda b, t: (b, 0, 0, t)),
                 pl.BlockSpec((1, N_PAIRS, tm, LANES), lambda b, t: (b, 0, t, 0))]
    side_in, side_out, side_shapes = _side_cast_specs(side_casts, s * n_t,
                                                      lambda b, t: b * n_t + t)
    args += [arr for arr, _ in side_casts]
    in_specs += side_in
    out_shape += side_shapes
    out_specs += side_out
    return pl.pallas_call(
        functools.partial(_kv_kernel, tm=tm, n_side=len(side_casts)),
        out_shape=tuple(out_shape),
        grid=(s, n_t),
        in_specs=in_specs,
        out_specs=tuple(out_specs),
        compiler_params=pltpu.CompilerParams(
            dimension_semantics=("arbitrary", "arbitrary"),
            vmem_limit_bytes=VMEM_LIMIT_BYTES),
        name="kv_proj",
    )(*args)


BIAS_Q = 2 * CHUNK
BIAS_K = BIAS_Q + KV_ROWS
BIAS_BASE = 1024


BIAS_HEADS_PER_STEP = 4


def _bias_kernel(base_ref, *refs, n_side):
    side_in = refs[:n_side]
    wide_ref, narrow_ref = refs[n_side:n_side + 2]
    for src, dst in zip(side_in, refs[n_side + 2:]):
        dst[...] = src[...].astype(dst.dtype)
    qc = lax.broadcasted_iota(jnp.int32, (BIAS_Q, BIAS_K), 0) // CHUNK
    kc = lax.broadcasted_iota(jnp.int32, (BIAS_Q, BIAS_K), 1) // CHUNK
    valid = (kc >= qc) & (kc <= qc + N_LEFT_CHUNKS)
    for h in range(BIAS_HEADS_PER_STEP):
        base = base_ref[h] * LOG2E
        tiled = jnp.broadcast_to(base, (BIAS_Q, BIAS_BASE))
        toep = pltpu.roll(tiled, 0, 1, stride=1, stride_axis=0)[:, 0:BIAS_K]
        table = jnp.where(valid, toep, NEG_INF)
        wide_ref[h] = table
        narrow_ref[h] = table[0:CHUNK, 0:CHUNK + KV_ROWS]


def _bias_tables(rel_bias, side_casts=()):
    n = rel_bias.shape[0] * N_HEADS
    f = jnp.transpose(rel_bias, (0, 2, 1)).reshape(n, N_REL)
    rev = f[:, ::-1]
    far = f[:, N_REL - 1:N_REL]
    n_left = KV_ROWS - MAX_REL
    base = jnp.concatenate(
        [jnp.broadcast_to(far, (n, n_left)), rev,
         jnp.broadcast_to(far, (n, BIAS_BASE - n_left - N_REL))], axis=1)
    base = base.reshape(n, 1, BIAS_BASE)
    hs = BIAS_HEADS_PER_STEP
    side_in, side_out, side_shapes = _side_cast_specs(side_casts, n // hs, lambda h: h)
    return pl.pallas_call(
        functools.partial(_bias_kernel, n_side=len(side_casts)),
        out_shape=(jax.ShapeDtypeStruct((n, BIAS_Q, BIAS_K), jnp.float32),
                   jax.ShapeDtypeStruct((n, CHUNK, CHUNK + KV_ROWS), jnp.float32), *side_shapes),
        grid=(n // hs,),
        in_specs=[pl.BlockSpec((hs, 1, BIAS_BASE), lambda h: (h, 0, 0)), *side_in],
        out_specs=(pl.BlockSpec((hs, BIAS_Q, BIAS_K), lambda h: (h, 0, 0)),
                   pl.BlockSpec((hs, CHUNK, CHUNK + KV_ROWS), lambda h: (h, 0, 0)), *side_out),
        compiler_params=pltpu.CompilerParams(dimension_semantics=("arbitrary",),
                                             vmem_limit_bytes=VMEM_LIMIT_BYTES),
        name="bias_table",
    )(base, *[arr for arr, _ in side_casts])


def _attn_layer_kernel(*refs, tm, qg):
    tile = functools.partial(_attn_tile, *refs, tm=tm, qg=qg)
    first = pl.program_id(1) == 0
    pl.when(first)(functools.partial(tile, has_a=False))
    pl.when(jnp.logical_not(first))(functools.partial(tile, has_a=True))


def _attn_tile(x_ref, kta_ref, ktb_ref, va_ref, vb_ref, bias_ref,
               g_ref, win_ref, gq_ref, gmat_ref, wout_ref,
               y_ref, q_ref, o_ref, z_ref, *, tm, qg, has_a):
    n_grp = tm // qg
    cb = PIECE_COLS
    n_cb = E_B // cb
    pairs_per_cb = cb // LANES
    x = x_ref[0]
    hb = _rms(x, g_ref[...]).astype(jnp.bfloat16)
    q_gain = gq_ref[...] * (HEAD_DIM ** -0.5 * LOG2E)

    def rows_of(j):
        return slice(j * qg, (j + 1) * qg)

    def q_piece(c):
        cols = slice(c * cb, (c + 1) * cb)
        qn = _head_rms(_bdot(hb, win_ref[:, cols]), gmat_ref, q_gain[:, cols])
        for h in range(pairs_per_cb):
            q_ref[c * pairs_per_cb + h] = qn[:, h * LANES:(h + 1) * LANES].astype(jnp.bfloat16)

    def z_piece(c):
        cols = slice(c * cb, (c + 1) * cb)
        z_ref[:, cols] = _bdot(hb, win_ref[:, E_B + c * cb:E_B + (c + 1) * cb])

    def out_piece(c):
        cols = slice(c * cb, (c + 1) * cb)
        o = jnp.concatenate([o_ref[c * pairs_per_cb + h] for h in range(pairs_per_cb)],
                            axis=1)
        yb = (o * _silu(z_ref[:, cols])).astype(jnp.bfloat16)
        contrib = _bdot(yb, wout_ref[cols, :])
        if c == 0:
            y_ref[0] = x + contrib
        else:
            y_ref[0] += contrib

    lane_lo = lax.broadcasted_iota(jnp.int32, (1, LANES), 1) < HEAD_DIM
    widths = [(KV_ROWS - j * qg, (j + 1) * qg) for j in range(n_grp)]

    def scores(p, j):
        wa, wbj = widths[j]
        qb = q_ref[p, j * qg:(j + 1) * qg, :]
        zero = jnp.zeros_like(qb)
        lhs = jnp.concatenate([jnp.where(lane_lo, qb, zero),
                               jnp.where(lane_lo, zero, qb)], axis=0)
        s = _bdot(lhs, ktb_ref[0, p, :, 0:wbj])
        if has_a:
            sa = _bdot(lhs, kta_ref[0, p, :, KV_ROWS - wa:KV_ROWS])
            return jnp.concatenate([sa, s], axis=1) + bias_ref[p]
        return s + bias_ref[p, :, wa:wa + wbj]

    def attend(p, j, s):
        wa, wbj = widths[j]
        mx = jnp.max(s, axis=1, keepdims=True)
        e = jnp.exp2(s - mx)
        l = jnp.sum(e, axis=1, keepdims=True)
        pb = e.astype(jnp.bfloat16)
        if has_a:
            o2 = (_bdot(pb[:, 0:wa], va_ref[0, p, KV_ROWS - wa:KV_ROWS, :])
                  + _bdot(pb[:, wa:wa + wbj], vb_ref[0, p, 0:wbj, :]))
        else:
            o2 = _bdot(pb, vb_ref[0, p, 0:wbj, :])
        o2 = o2 * (1.0 / l)
        o_ref[p, rows_of(j), :] = jnp.where(lane_lo, o2[0:qg], o2[qg:2 * qg])

    items = [(c * pairs_per_cb + h, j) for c in range(n_cb) for j in range(n_grp)
             for h in range(pairs_per_cb)]
    per_cb = len(items) // n_cb
    pieces = []
    for c in range(n_cb):
        if c > 0:
            pieces.append((per_cb * c - QK_AHEAD - 1, 0, len(pieces), q_piece, c))
        pieces.append((per_cb * c + per_cb // 2, 0, len(pieces), z_piece, c))
        if c + 1 < n_cb:
            pieces.append((per_cb * (c + 1) + per_cb // 2, per_cb * (c + 1), len(pieces),
                           out_piece, c))
    n_pieces = len(pieces)
    q_piece(0)
    pending = {}
    for i in range(min(QK_AHEAD, len(items))):
        pending[i] = scores(*items[i])
    issued = 0
    for i in range(len(items)):
        quota = ((i + 1) * n_pieces) // len(items) - issued
        ready = sorted(pc for pc in pieces if pc[1] <= i)
        take = [pc for pc in ready if pc[0] <= i]
        take += [pc for pc in ready if pc[0] > i][:max(0, quota - len(take))]
        for pc in take:
            pieces.remove(pc)
            pc[3](pc[4])
            issued += 1
        if i + QK_AHEAD < len(items):
            pending[i + QK_AHEAD] = scores(*items[i + QK_AHEAD])
        attend(*items[i], pending.pop(i))
    for pc in sorted(pieces):
        pc[3](pc[4])
    out_piece(n_cb - 1)


def _attn_layer(x, kt, v, bias, g_norm, w_in, gq_t, gmat, w_out, *, layer, tm, qg):
    s, t_len, _ = x.shape
    n_t = t_len // tm
    const2 = lambda b, t: (0, 0)
    assert tm == KV_ROWS and kt.shape[-1] == t_len
    prev = lambda t: jnp.maximum(t - 1, 0)
    in_specs = [
        pl.BlockSpec((1, tm, D_MODEL), lambda b, t: (b, t, 0)),
        pl.BlockSpec((1, N_PAIRS, LANES, KV_ROWS), lambda b, t: (b, 0, 0, prev(t))),
        pl.BlockSpec((1, N_PAIRS, LANES, tm), lambda b, t: (b, 0, 0, t)),
        pl.BlockSpec((1, N_PAIRS, KV_ROWS, LANES), lambda b, t: (b, 0, prev(t), 0)),
        pl.BlockSpec((1, N_PAIRS, tm, LANES), lambda b, t: (b, 0, t, 0)),
        _layer_spec(bias, layer),
        _layer_spec(g_norm, layer),
        _layer_spec(w_in, layer),
        _layer_spec(gq_t, layer),
        pl.BlockSpec((MXU_DIM, MXU_DIM), const2),
        _layer_spec(w_out, layer),
    ]
    return pl.pallas_call(
        functools.partial(_attn_layer_kernel, tm=tm, qg=qg),
        out_shape=jax.ShapeDtypeStruct(x.shape, jnp.float32),
        grid=(s, n_t),
        in_specs=in_specs,
        out_specs=pl.BlockSpec((1, tm, D_MODEL), lambda b, t: (b, t, 0)),
        scratch_shapes=[pltpu.VMEM((N_PAIRS, tm, LANES), jnp.bfloat16),
                        pltpu.VMEM((N_PAIRS, tm, LANES), jnp.float32),
                        pltpu.VMEM((tm, E_B), jnp.float32)],
        compiler_params=pltpu.CompilerParams(
            dimension_semantics=("arbitrary", "arbitrary"),
            vmem_limit_bytes=VMEM_LIMIT_BYTES),
        name="attn_layer",
    )(x, kt, kt, v, v, bias, g_norm, w_in, gq_t, gmat, w_out)


def _attn_cached_kernel(x_ref, ktc_ref, ktn_ref, vtc_ref, vn_ref, bias_ref,
                        g_ref, win_ref, gq_ref, gmat_ref, wout_ref,
                        y_ref, q_ref, o_ref, z_ref, *, n_seq, qg):
    b = pl.program_id(0)
    lane_lo = lax.broadcasted_iota(jnp.int32, (1, LANES), 1) < HEAD_DIM
    contract_last = (((1,), (1,)), ((), ()))

    @pl.when(b == 0)
    def _():
        hb = _rms(x_ref[...], g_ref[...]).astype(jnp.bfloat16)
        q_gain = gq_ref[...] * (HEAD_DIM ** -0.5 * LOG2E)
        qn = _head_rms(_bdot(hb, win_ref[:, 0:E_B]), gmat_ref, q_gain)
        for p in range(N_PAIRS):
            q_ref[p] = qn[:, p * LANES:(p + 1) * LANES].astype(jnp.bfloat16)
        z_ref[...] = _bdot(hb, win_ref[:, E_B:2 * E_B])

    rows = pl.ds(pl.multiple_of(b * qg, qg), qg)

    def scores(p):
        qb = q_ref[p, rows, :]
        zero = jnp.zeros_like(qb)
        lhs = jnp.concatenate([jnp.where(lane_lo, qb, zero),
                               jnp.where(lane_lo, zero, qb)], axis=0)
        sa = _bdot(lhs, ktc_ref[0, p].astype(jnp.bfloat16))
        sb = _bdot(lhs, ktn_ref[0, p, :, 0:qg])
        return jnp.concatenate([sa, sb], axis=1) + bias_ref[p]

    def attend(p, s):
        keys = vtc_ref.shape[-1]
        mx = jnp.max(s, axis=1, keepdims=True)
        e = jnp.exp2(s - mx)
        l = jnp.sum(e, axis=1, keepdims=True)
        pb = e.astype(jnp.bfloat16)
        oa = lax.dot_general(pb[:, 0:keys], vtc_ref[0, p].astype(jnp.bfloat16), contract_last,
                             preferred_element_type=jnp.float32)
        o2 = (oa + _bdot(pb[:, keys:keys + qg], vn_ref[0, p])) * (1.0 / l)
        o_ref[p, rows, :] = jnp.where(lane_lo, o2[0:qg], o2[qg:2 * qg])

    pending = {p: scores(p) for p in range(min(QK_AHEAD, N_PAIRS))}
    for p in range(N_PAIRS):
        if p + QK_AHEAD < N_PAIRS:
            pending[p + QK_AHEAD] = scores(p + QK_AHEAD)
        attend(p, pending.pop(p))

    @pl.when(b == n_seq - 1)
    def _():
        o = jnp.concatenate([o_ref[p] for p in range(N_PAIRS)], axis=1)
        yb = (o * _silu(z_ref[...])).astype(jnp.bfloat16)
        y_ref[...] = x_ref[...] + _bdot(yb, wout_ref[...])


def _attn_layer_cached(x, kt_cache, kt_new, vt_cache, v_new, bias, g_norm, w_in, gq_t, gmat,
                       w_out, *, layer):
    s, qg, _ = x.shape
    m = s * qg
    keys = kt_cache.shape[-1]
    whole = pl.BlockSpec((m, D_MODEL), lambda b: (0, 0))
    per_seq = lambda a: pl.BlockSpec((1,) + a.shape[1:], lambda b: (b, 0, 0, 0))
    in_specs = [whole, per_seq(kt_cache), per_seq(kt_new), per_seq(vt_cache), per_seq(v_new),
                _layer_spec(bias, layer), _layer_spec(g_norm, layer), _layer_spec(w_in, layer),
                _layer_spec(gq_t, layer), pl.BlockSpec((MXU_DIM, MXU_DIM), lambda b: (0, 0)),
                _layer_spec(w_out, layer)]
    assert bias.shape[-1] == keys + qg
    y = pl.pallas_call(
        functools.partial(_attn_cached_kernel, n_seq=s, qg=qg),
        out_shape=jax.ShapeDtypeStruct((m, D_MODEL), jnp.float32),
        grid=(s,),
        in_specs=in_specs,
        out_specs=whole,
        scratch_shapes=[pltpu.VMEM((N_PAIRS, m, LANES), jnp.bfloat16),
                        pltpu.VMEM((N_PAIRS, m, LANES), jnp.float32),
                        pltpu.VMEM((m, E_B), jnp.float32)],
        compiler_params=pltpu.CompilerParams(
            dimension_semantics=("arbitrary",), vmem_limit_bytes=VMEM_LIMIT_BYTES),
        name="attn_cached",
    )(x.reshape(m, D_MODEL), kt_cache, kt_new, vt_cache, v_new, bias, g_norm, w_in, gq_t, gmat,
      w_out)
    return y.reshape(x.shape)


def _trunk(x, hist16, cache_k, cache_v, pos0, w, *, pool_seg, pool_len, tile):
    s, t_len, _ = x.shape
    hists = []
    pool_names = ("w_in_a", "w_grp_a", "w_out_a")
    n_a = len(w["w_in_a"])
    for layer in range(n_a):
        side = []
        cast_next = layer + 1 < n_a and w["w_in_a"][layer + 1] is None
        cast_kv = w["w_kv"] is None
        if cast_next:
            side += [(w["f32"][name], layer + 1) for name in pool_names]
        if cast_kv:
            side.append((w["f32"]["w_kv"], None))
        x, h_out, *cast = _pool_layer(
            x, hist16, layer, (w["norm_a"], layer), (w["w_in_a"][layer], 0),
            (w["w_grp_a"][layer], 0), (w["scale_a"], layer), (w["w_out_a"][layer], 0),
            n_seg=pool_seg, seg_len=pool_len, pos0=pos0, side_casts=side)
        if cast_next:
            for name in pool_names:
                w[name][layer + 1] = cast.pop(0).reshape((1,) + w["shape"][name][1:])
        if cast_kv:
            w["w_kv"] = cast.pop(0)
        hists.append(h_out[:, HIST_PAD - POOL_HIST:, :])
    side = []
    if w["w_in_b"] is None:
        side = [(w["f32"][name].reshape(-1, w["f32"][name].shape[-1]), None)
                for name in ("w_in_b", "w_out_b")]
    k_new, v_new, kt, vb, *cast = _kv_proj(x, w["norm_kv"], w["w_kv"], w["gk_t"], w["gmat"],
                                           tm=min(pool_len, t_len), side_casts=side)
    if side:
        w["w_in_b"] = cast[0].reshape(w["f32"]["w_in_b"].shape)
        w["w_out_b"] = cast[1].reshape(w["f32"]["w_out_b"].shape)
    weights = (w["norm_b"], w["w_in_b"], w["gq_t"], w["gmat"], w["w_out_b"])
    if cache_k is None:
        for j in range(w["w_in_b"].shape[0]):
            x = _attn_layer(x, kt, vb, w["bias_prompt"], *weights, layer=j, tm=tile, qg=BIAS_Q)
    else:
        lc = cache_k.shape[1]
        as_pairs = lambda c: jnp.transpose(c, (0, 2, 3, 1)).reshape(s, N_PAIRS, LANES, lc)
        for j in range(w["w_in_b"].shape[0]):
            x = _attn_layer_cached(x, as_pairs(cache_k), kt, as_pairs(cache_v), vb,
                                   w["bias_sample"], *weights, layer=j)
    return x, jnp.stack(hists, axis=0), k_new, v_new


def kernel(x_prompt, x_sample, state_pool, cache_k, cache_v, norm_a, w_in_a, w_grp_a, scale_a,
           w_out_a, norm_kv, w_kv, g_k, norm_b, w_in_b, g_q, rel_bias_b, w_out_b):
    bf = jnp.bfloat16
    head_of_lane = jnp.arange(MXU_DIM, dtype=jnp.int32) // HEAD_DIM
    gmat = (head_of_lane[:, None] == head_of_lane[None, :]).astype(bf)
    n_b = rel_bias_b.shape[0]
    n_a = w_in_a.shape[0]
    w_grp_rows = w_grp_a.reshape(n_a, -1, G_A)
    pool_f32 = dict(w_in_a=w_in_a, w_grp_a=w_grp_rows, w_out_a=w_out_a)
    pool_shape = dict(w_in_a=w_in_a.shape, w_grp_a=w_grp_a.shape, w_out_a=w_out_a.shape)
    wide, narrow, *first = _bias_tables(rel_bias_b,
                                        side_casts=[(a, 0) for a in pool_f32.values()])
    pool_bf = {name: [c.reshape((1,) + pool_shape[name][1:])] + [None] * (n_a - 1)
               for name, c in zip(pool_f32, first)}
    pair_rows = HEADS_PER_LANE_BLOCK
    w = dict(
        f32=dict(w_kv=w_kv, w_in_b=w_in_b, w_out_b=w_out_b, **pool_f32), shape=pool_shape,
        norm_a=norm_a[:, None, :], scale_a=scale_a[:, None, :], **pool_bf,
        norm_kv=norm_kv[None, :], w_kv=None, gk_t=jnp.tile(g_k, N_HEADS)[None, :],
        norm_b=norm_b[:, None, :], w_in_b=None,
        gq_t=jnp.tile(g_q, (1, N_HEADS))[:, None, :], w_out_b=None, gmat=gmat,
        bias_prompt=wide.reshape(n_b, N_PAIRS, pair_rows * BIAS_Q, BIAS_K),
        bias_sample=narrow.reshape(n_b, N_PAIRS, pair_rows * CHUNK, CHUNK + KV_ROWS),
    )
    bp, sp, _ = x_prompt.shape
    bs, ss, _ = x_sample.shape
    y_p, pool_p, k_p, v_p = _trunk(x_prompt, None, None, None, 0, w,
                                   pool_seg=1, pool_len=min(POOL_TILE, sp), tile=KV_ROWS)
    hist16 = jnp.pad(state_pool, ((0, 0), (0, 0), (HIST_PAD - POOL_HIST, 0), (0, 0)))
    y_s, pool_s, k_s, v_s = _trunk(x_sample, hist16, cache_k, cache_v, PAST_LEN, w,
                                   pool_seg=bs, pool_len=ss, tile=ss)
    lp = min(KV_ROWS, sp)
    return (y_p, y_s, pool_p, pool_s,
            k_p.reshape(bp, lp, N_HEADS, HEAD_DIM), v_p.reshape(bp, lp, N_HEADS, HEAD_DIM),
            k_s.reshape(bs, ss, N_HEADS, HEAD_DIM), v_s.reshape(bs, ss, N_HEADS, HEAD_DIM))
```

```python
import functools
import math

import jax
import jax.numpy as jnp
from jax import lax
from jax.experimental import pallas as pl
from jax.experimental.pallas import tpu as pltpu

D_MODEL = 1024
E_A = 2048
POOL_WINDOWS = (2, 4, 8, 16)
N_POOL_GROUPS = len(POOL_WINDOWS)
G_A = E_A // N_POOL_GROUPS
POOL_HIST = max(POOL_WINDOWS) - 1
HEAD_DIM = 64
N_HEADS = 16
E_B = N_HEADS * HEAD_DIM
CHUNK = 64
N_LEFT_CHUNKS = 8
KV_ROWS = N_LEFT_CHUNKS * CHUNK
MAX_REL = 128
N_REL = 2 * MAX_REL + 1
EPS = 1e-6
NEG_INF = -1e30
PAST_LEN = 4096
LOG2E = math.log2(math.e)

SUBLANES = 8
LANES = 128
MXU_DIM = 256
HEADS_PER_LANE_BLOCK = LANES // HEAD_DIM
N_PAIRS = N_HEADS // HEADS_PER_LANE_BLOCK
HIST_PAD = 2 * SUBLANES
FRONT = SUBLANES
QK_AHEAD = 2
PIECE_COLS = 2 * MXU_DIM
POOL_TILE = 1024
POOL_SUB_TILE = 512
VMEM_LIMIT_BYTES = 56 * 1024 * 1024


def _bdot(a, b):
    return jnp.dot(a, b, preferred_element_type=jnp.float32)


def _rms(x, g):
    ms = jnp.mean(x * x, axis=-1, keepdims=True)
    return x * lax.rsqrt(ms + EPS) * g


def _silu(z):
    h = 0.5 * z
    return h * jnp.tanh(h) + h


def _side_cast_specs(arrays, n_steps, step_of):
    in_specs, out_specs, out_shapes = [], [], []
    for arr, lead in arrays:
        n_rows, n_cols = arr.shape[-2:]
        rows = n_rows // n_steps
        assert rows * n_steps == n_rows and rows % (2 * SUBLANES) == 0
        if lead is None:
            in_specs.append(pl.BlockSpec((rows, n_cols), lambda *i: (step_of(*i), 0)))
        else:
            in_specs.append(pl.BlockSpec((None, rows, n_cols),
                                         lambda *i, lead=lead: (lead, step_of(*i), 0)))
        out_specs.append(pl.BlockSpec((rows, n_cols), lambda *i: (step_of(*i), 0)))
        out_shapes.append(jax.ShapeDtypeStruct((n_rows, n_cols), jnp.bfloat16))
    return in_specs, out_specs, out_shapes


def _pool_layer_kernel(*refs, n_seg, seg_len, sub_len, pos0, has_hist, n_side):
    it = iter(refs)
    x_ref = next(it)
    hist_ref = next(it) if has_hist else None
    g_ref, win_ref, wgrp_ref, scale_ref, wout_ref = (next(it) for _ in range(5))
    side_in = [next(it) for _ in range(n_side)]
    y_ref, hout_ref = next(it), next(it)
    side_out = [next(it) for _ in range(n_side)]
    acc_ref, upad_ref, s0_ref, s1_ref, carry_ref = it
    for src, dst in zip(side_in, side_out):
        dst[...] = src[...].astype(dst.dtype)
    t = pl.program_id(1)
    m = n_seg * sub_len
    rows = HIST_PAD + sub_len

    @pl.when(t == 0)
    def _():
        if has_hist:
            carry_ref[...] = hist_ref[...]
        else:
            carry_ref[...] = jnp.zeros_like(carry_ref)

    zero_front = jnp.zeros((n_seg, FRONT, G_A), jnp.float32)
    upad_ref[:, 0:FRONT, :] = zero_front
    s0_ref[:, 0:FRONT, :] = zero_front
    s1_ref[:, 0:FRONT, :] = zero_front

    for r0 in range(0, seg_len, sub_len):
        x = x_ref[:, r0:r0 + sub_len, :].reshape(m, D_MODEL)
        hb = _rms(x, g_ref[...]).astype(jnp.bfloat16)
        pos = (pos0 + t * seg_len + r0
               + lax.broadcasted_iota(jnp.int32, (1, sub_len, 1), 1))
        for g, w in enumerate(POOL_WINDOWS):
            lo = g * G_A
            u = _bdot(hb, win_ref[:, lo:lo + G_A])
            z = _bdot(hb, win_ref[:, E_A + lo:E_A + lo + G_A])
            u3 = u.reshape(n_seg, sub_len, G_A)
            upad_ref[:, FRONT:FRONT + HIST_PAD, :] = carry_ref[:, :, lo:lo + G_A]
            upad_ref[:, FRONT + HIST_PAD:, :] = u3
            carry_ref[:, :, lo:lo + G_A] = upad_ref[:, FRONT + sub_len:, :]
            bufs = (upad_ref, s0_ref, s1_ref)
            src = 0
            for k in range(g + 1):
                dst = 1 if src != 1 else 2
                sh = 1 << k
                bufs[dst][:, FRONT:, :] = (bufs[src][:, FRONT:, :]
                                           + bufs[src][:, FRONT - sh:FRONT - sh + rows, :])
                src = dst
            wsum = bufs[src][:, FRONT + HIST_PAD:, :]
            inv_cnt = 1.0 / jnp.minimum(pos + 1, w).astype(jnp.float32)
            pooled = (wsum * inv_cnt - u3).reshape(m, G_A)
            pg = _bdot(pooled.astype(jnp.bfloat16), wgrp_ref[g]) * scale_ref[:, lo:lo + G_A]
            yb = (pg * _silu(z)).astype(jnp.bfloat16)
            contrib = _bdot(yb, wout_ref[lo:lo + G_A, :])
            if g == 0:
                acc_ref[...] = contrib
            else:
                acc_ref[...] += contrib
        y_ref[:, r0:r0 + sub_len, :] = (x + acc_ref[...]).reshape(n_seg, sub_len, D_MODEL)
    hout_ref[...] = carry_ref[...]


def _layer_spec(arr, layer):
    tail = (0,) * (arr.ndim - 1)
    return pl.BlockSpec((None,) + arr.shape[1:], lambda *_: (layer,) + tail,
                        pipeline_mode=pl.Buffered(1))


def _pool_layer(x, hist16, layer, g_norm, w_in, w_grp, scale, w_out, *, n_seg, seg_len, pos0,
                side_casts=()):
    s, t_len, _ = x.shape
    has_hist = hist16 is not None
    grid = (s // n_seg, t_len // seg_len)
    in_specs = [pl.BlockSpec((n_seg, seg_len, D_MODEL), lambda b, t: (b, t, 0))]
    args = [x]
    if has_hist:
        in_specs.append(pl.BlockSpec((None, n_seg, HIST_PAD, E_A), lambda b, t: (layer, b, 0, 0)))
        args.append(hist16)
    for arr, idx in (g_norm, w_in, w_grp, scale, w_out):
        args.append(arr)
        in_specs.append(_layer_spec(arr, idx))
    side_in, side_out, side_shapes = _side_cast_specs(
        side_casts, grid[0] * grid[1], lambda b, t: b * grid[1] + t)
    args += [arr for arr, _ in side_casts]
    in_specs += side_in
    sub_len = min(seg_len, POOL_SUB_TILE)
    m = n_seg * sub_len
    stage = pltpu.VMEM((n_seg, FRONT + HIST_PAD + sub_len, G_A), jnp.float32)
    return pl.pallas_call(
        functools.partial(_pool_layer_kernel, n_seg=n_seg, seg_len=seg_len, sub_len=sub_len,
                          pos0=pos0, has_hist=has_hist, n_side=len(side_casts)),
        out_shape=(jax.ShapeDtypeStruct(x.shape, jnp.float32),
                   jax.ShapeDtypeStruct((s, HIST_PAD, E_A), jnp.float32), *side_shapes),
        grid=grid,
        in_specs=in_specs,
        out_specs=(pl.BlockSpec((n_seg, seg_len, D_MODEL), lambda b, t: (b, t, 0)),
                   pl.BlockSpec((n_seg, HIST_PAD, E_A), lambda b, t: (b, 0, 0)), *side_out),
        scratch_shapes=[pltpu.VMEM((m, D_MODEL), jnp.float32), stage, stage, stage,
                        pltpu.VMEM((n_seg, HIST_PAD, E_A), jnp.float32)],
        compiler_params=pltpu.CompilerParams(
            dimension_semantics=("arbitrary", "arbitrary"),
            vmem_limit_bytes=VMEM_LIMIT_BYTES),
        name="pool_layer",
    )(*args)


def _head_rms(v, gmat_ref, gain):
    sq = (v * v).astype(jnp.bfloat16)
    ss = jnp.concatenate([_bdot(sq[:, c:c + MXU_DIM], gmat_ref[...])
                          for c in range(0, v.shape[-1], MXU_DIM)], axis=1)
    return v * lax.rsqrt(ss * (1.0 / HEAD_DIM) + EPS) * gain


def _kv_kernel(*refs, tm, n_side):
    it = iter(refs)
    x_ref, g_ref, wkv_ref, gk_ref, gmat_ref = (next(it) for _ in range(5))
    side_in = [next(it) for _ in range(n_side)]
    kout_ref, vout_ref, kt_ref, vb_ref = (next(it) for _ in range(4))
    for src, dst in zip(side_in, it):
        dst[...] = src[...].astype(dst.dtype)
    keep = kout_ref.shape[1]
    sub = min(tm, keep)
    for r0 in range(0, tm, sub):
        x = x_ref[0, r0:r0 + sub, :]
        hb = _rms(x, g_ref[...]).astype(jnp.bfloat16)
        k = _bdot(hb, wkv_ref[:, 0:E_B])
        kn = _head_rms(k, gmat_ref, gk_ref[...])
        v = _bdot(hb, wkv_ref[:, E_B:2 * E_B])
        if r0 + sub == tm:
            kout_ref[0] = kn
            vout_ref[0] = v
        for p in range(N_PAIRS):
            blk = kn[:, p * LANES:(p + 1) * LANES]
            if sub < LANES:
                blk = jnp.concatenate([blk, jnp.zeros((LANES - sub, LANES), jnp.float32)],
                                      axis=0)
                kt_ref[0, p] = blk.T.astype(jnp.bfloat16)
            else:
                kt_ref[0, p, :, r0:r0 + sub] = blk.T.astype(jnp.bfloat16)
            vb_ref[0, p, r0:r0 + sub, :] = v[:, p * LANES:(p + 1) * LANES].astype(jnp.bfloat16)


def _kv_proj(x, g_kv, w_kv, gk_t, gmat, *, tm, side_casts=()):
    s, t_len, _ = x.shape
    n_t = t_len // tm
    keep = min(KV_ROWS, t_len)
    assert tm % keep == 0
    kt_w = max(tm, LANES)
    const2 = lambda b, t: (0, 0)
    in_specs = [pl.BlockSpec((1, tm, D_MODEL), lambda b, t: (b, t, 0)),
                pl.BlockSpec((1, D_MODEL), const2),
                pl.BlockSpec((D_MODEL, 2 * E_B), const2),
                pl.BlockSpec((1, E_B), const2),
                pl.BlockSpec((MXU_DIM, MXU_DIM), const2)]
    args = [x, g_kv, w_kv, gk_t, gmat]
    out_shape = [jax.ShapeDtypeStruct((s, keep, E_B), jnp.float32),
                 jax.ShapeDtypeStruct((s, keep, E_B), jnp.float32),
                 jax.ShapeDtypeStruct((s, N_PAIRS, LANES, n_t * kt_w), jnp.bfloat16),
                 jax.ShapeDtypeStruct((s, N_PAIRS, t_len, LANES), jnp.bfloat16)]
    out_specs = [pl.BlockSpec((1, keep, E_B), lambda b, t: (b, 0, 0)),
                 pl.BlockSpec((1, keep, E_B), lambda b, t: (b, 0, 0)),
                 pl.BlockSpec((1, N_PAIRS, LANES, kt_w), lambda b, t: (b, 0, 0, t)),
                 pl.BlockSpec((1, N_PAIRS, tm, LANES), lambda b, t: (b, 0, t, 0))]
    side_in, side_out, side_shapes = _side_cast_specs(side_casts, s * n_t,
                                                      lambda b, t: b * n_t + t)
    args += [arr for arr, _ in side_casts]
    in_specs += side_in
    out_shape += side_shapes
    out_specs += side_out
    return pl.pallas_call(
        functools.partial(_kv_kernel, tm=tm, n_side=len(side_casts)),
        out_shape=tuple(out_shape),
        grid=(s, n_t),
        in_specs=in_specs,
        out_specs=tuple(out_specs),
        compiler_params=pltpu.CompilerParams(
            dimension_semantics=("arbitrary", "arbitrary"),
            vmem_limit_bytes=VMEM_LIMIT_BYTES),
        name="kv_proj",
    )(*args)


BIAS_Q = 2 * CHUNK
BIAS_K = BIAS_Q + KV_ROWS
BIAS_BASE = 1024


BIAS_HEADS_PER_STEP = 4


def _bias_kernel(base_ref, *refs, n_side):
    side_in = refs[:n_side]
    wide_ref, narrow_ref = refs[n_side:n_side + 2]
    for src, dst in zip(side_in, refs[n_side + 2:]):
        dst[...] = src[...].astype(dst.dtype)
    qc = lax.broadcasted_iota(jnp.int32, (BIAS_Q, BIAS_K), 0) // CHUNK
    kc = lax.broadcasted_iota(jnp.int32, (BIAS_Q, BIAS_K), 1) // CHUNK
    valid = (kc >= qc) & (kc <= qc + N_LEFT_CHUNKS)
    for h in range(BIAS_HEADS_PER_STEP):
        base = base_ref[h] * LOG2E
        tiled = jnp.broadcast_to(base, (BIAS_Q, BIAS_BASE))
        toep = pltpu.roll(tiled, 0, 1, stride=1, stride_axis=0)[:, 0:BIAS_K]
        table = jnp.where(valid, toep, NEG_INF)
        wide_ref[h] = table
        narrow_ref[h] = table[0:CHUNK, 0:CHUNK + KV_ROWS]


def _bias_tables(rel_bias, side_casts=()):
    n = rel_bias.shape[0] * N_HEADS
    f = jnp.transpose(rel_bias, (0, 2, 1)).reshape(n, N_REL)
    rev = f[:, ::-1]
    far = f[:, N_REL - 1:N_REL]
    n_left = KV_ROWS - MAX_REL
    base = jnp.concatenate(
        [jnp.broadcast_to(far, (n, n_left)), rev,
         jnp.broadcast_to(far, (n, BIAS_BASE - n_left - N_REL))], axis=1)
    base = base.reshape(n, 1, BIAS_BASE)
    hs = BIAS_HEADS_PER_STEP
    side_in, side_out, side_shapes = _side_cast_specs(side_casts, n // hs, lambda h: h)
    return pl.pallas_call(
        functools.partial(_bias_kernel, n_side=len(side_casts)),
        out_shape=(jax.ShapeDtypeStruct((n, BIAS_Q, BIAS_K), jnp.float32),
                   jax.ShapeDtypeStruct((n, CHUNK, CHUNK + KV_ROWS), jnp.float32), *side_shapes),
        grid=(n // hs,),
        in_specs=[pl.BlockSpec((hs, 1, BIAS_BASE), lambda h: (h, 0, 0)), *side_in],
        out_specs=(pl.BlockSpec((hs, BIAS_Q, BIAS_K), lambda h: (h, 0, 0)),
                   pl.BlockSpec((hs, CHUNK, CHUNK + KV_ROWS), lambda h: (h, 0, 0)), *side_out),
        compiler_params=pltpu.CompilerParams(dimension_semantics=("arbitrary",),
                                             vmem_limit_bytes=VMEM_LIMIT_BYTES),
        name="bias_table",
    )(base, *[arr for arr, _ in side_casts])


def _attn_layer_kernel(*refs, tm, qg):
    tile = functools.partial(_attn_tile, *refs, tm=tm, qg=qg)
    first = pl.program_id(1) == 0
    pl.when(first)(functools.partial(tile, has_a=False))
    pl.when(jnp.logical_not(first))(functools.partial(tile, has_a=True))


def _attn_tile(x_ref, kta_ref, ktb_ref, va_ref, vb_ref, bias_ref,
               g_ref, win_ref, gq_ref, gmat_ref, wout_ref,
               y_ref, q_ref, o_ref, z_ref, *, tm, qg, has_a):
    n_grp = tm // qg
    cb = PIECE_COLS
    n_cb = E_B // cb
    pairs_per_cb = cb // LANES
    x = x_ref[0]
    hb = _rms(x, g_ref[...]).astype(jnp.bfloat16)
    q_gain = gq_ref[...] * (HEAD_DIM ** -0.5 * LOG2E)

    def rows_of(j):
        return slice(j * qg, (j + 1) * qg)

    def q_piece(c):
        cols = slice(c * cb, (c + 1) * cb)
        qn = _head_rms(_bdot(hb, win_ref[:, cols]), gmat_ref, q_gain[:, cols])
        for h in range(pairs_per_cb):
            q_ref[c * pairs_per_cb + h] = qn[:, h * LANES:(h + 1) * LANES].astype(jnp.bfloat16)

    def z_piece(c):
        cols = slice(c * cb, (c + 1) * cb)
        z_ref[:, cols] = _bdot(hb, win_ref[:, E_B + c * cb:E_B + (c + 1) * cb])

    def out_piece(c):
        cols = slice(c * cb, (c + 1) * cb)
        o = jnp.concatenate([o_ref[c * pairs_per_cb + h] for h in range(pairs_per_cb)],
                            axis=1)
        yb = (o * _silu(z_ref[:, cols])).astype(jnp.bfloat16)
        contrib = _bdot(yb, wout_ref[cols, :])
        if c == 0:
            y_ref[0] = x + contrib
        else:
            y_ref[0] += contrib

    lane_lo = lax.broadcasted_iota(jnp.int32, (1, LANES), 1) < HEAD_DIM
    widths = [(KV_ROWS - j * qg, (j + 1) * qg) for j in range(n_grp)]

    def scores(p, j):
        wa, wbj = widths[j]
        qb = q_ref[p, j * qg:(j + 1) * qg, :]
        zero = jnp.zeros_like(qb)
        lhs = jnp.concatenate([jnp.where(lane_lo, qb, zero),
                               jnp.where(lane_lo, zero, qb)], axis=0)
        s = _bdot(lhs, ktb_ref[0, p, :, 0:wbj])
        if has_a:
            sa = _bdot(lhs, kta_ref[0, p, :, KV_ROWS - wa:KV_ROWS])
            return jnp.concatenate([sa, s], axis=1) + bias_ref[p]
        return s + bias_ref[p, :, wa:wa + wbj]

    def attend(p, j, s):
        wa, wbj = widths[j]
        mx = jnp.max(s, axis=1, keepdims=True)
        e = jnp.exp2(s - mx)
        l = jnp.sum(e, axis=1, keepdims=True)
        pb = e.astype(jnp.bfloat16)
        if has_a:
            o2 = (_bdot(pb[:, 0:wa], va_ref[0, p, KV_ROWS - wa:KV_ROWS, :])
                  + _bdot(pb[:, wa:wa + wbj], vb_ref[0, p, 0:wbj, :]))
        else:
            o2 = _bdot(pb, vb_ref[0, p, 0:wbj, :])
        o2 = o2 * (1.0 / l)
        o_ref[p, rows_of(j), :] = jnp.where(lane_lo, o2[0:qg], o2[qg:2 * qg])

    items = [(c * pairs_per_cb + h, j) for c in range(n_cb) for j in range(n_grp)
             for h in range(pairs_per_cb)]
    per_cb = len(items) // n_cb
    pieces = []
    for c in range(n_cb):
        if c > 0:
            pieces.append((per_cb * c - QK_AHEAD - 1, 0, len(pieces), q_piece, c))
        pieces.append((per_cb * c + per_cb // 2, 0, len(pieces), z_piece, c))
        if c + 1 < n_cb:
            pieces.append((per_cb * (c + 1) + per_cb // 2, per_cb * (c + 1), len(pieces),
                           out_piece, c))
    n_pieces = len(pieces)
    q_piece(0)
    pending = {}
    for i in range(min(QK_AHEAD, len(items))):
        pending[i] = scores(*items[i])
    issued = 0
    for i in range(len(items)):
        quota = ((i + 1) * n_pieces) // len(items) - issued
        ready = sorted(pc for pc in pieces if pc[1] <= i)
        take = [pc for pc in ready if pc[0] <= i]
        take += [pc for pc in ready if pc[0] > i][:max(0, quota - len(take))]
        for pc in take:
            pieces.remove(pc)
            pc[3](pc[4])
            issued += 1
        if i + QK_AHEAD < len(items):
            pending[i + QK_AHEAD] = scores(*items[i + QK_AHEAD])
        attend(*items[i], pending.pop(i))
    for pc in sorted(pieces):
        pc[3](pc[4])
    out_piece(n_cb - 1)


def _attn_layer(x, kt, v, bias, g_norm, w_in, gq_t, gmat, w_out, *, layer, tm, qg):
    s, t_len, _ = x.shape
    n_t = t_len // tm
    const2 = lambda b, t: (0, 0)
    assert tm == KV_ROWS and kt.shape[-1] == t_len
    prev = lambda t: jnp.maximum(t - 1, 0)
    in_specs = [
        pl.BlockSpec((1, tm, D_MODEL), lambda b, t: (b, t, 0)),
        pl.BlockSpec((1, N_PAIRS, LANES, KV_ROWS), lambda b, t: (b, 0, 0, prev(t))),
        pl.BlockSpec((1, N_PAIRS, LANES, tm), lambda b, t: (b, 0, 0, t)),
        pl.BlockSpec((1, N_PAIRS, KV_ROWS, LANES), lambda b, t: (b, 0, prev(t), 0)),
        pl.BlockSpec((1, N_PAIRS, tm, LANES), lambda b, t: (b, 0, t, 0)),
        _layer_spec(bias, layer),
        _layer_spec(g_norm, layer),
        _layer_spec(w_in, layer),
        _layer_spec(gq_t, layer),
        pl.BlockSpec((MXU_DIM, MXU_DIM), const2),
        _layer_spec(w_out, layer),
    ]
    return pl.pallas_call(
        functools.partial(_attn_layer_kernel, tm=tm, qg=qg),
        out_shape=jax.ShapeDtypeStruct(x.shape, jnp.float32),
        grid=(s, n_t),
        in_specs=in_specs,
        out_specs=pl.BlockSpec((1, tm, D_MODEL), lambda b, t: (b, t, 0)),
        scratch_shapes=[pltpu.VMEM((N_PAIRS, tm, LANES), jnp.bfloat16),
                        pltpu.VMEM((N_PAIRS, tm, LANES), jnp.float32),
                        pltpu.VMEM((tm, E_B), jnp.float32)],
        compiler_params=pltpu.CompilerParams(
            dimension_semantics=("arbitrary", "arbitrary"),
            vmem_limit_bytes=VMEM_LIMIT_BYTES),
        name="attn_layer",
    )(x, kt, kt, v, v, bias, g_norm, w_in, gq_t, gmat, w_out)


def _attn_cached_kernel(x_ref, ktc_ref, ktn_ref, vtc_ref, vn_ref, bias_ref,
                        g_ref, win_ref, gq_ref, gmat_ref, wout_ref,
                        y_ref, q_ref, o_ref, z_ref, *, n_seq, qg):
    b = pl.program_id(0)
    lane_lo = lax.broadcasted_iota(jnp.int32, (1, LANES), 1) < HEAD_DIM
    contract_last = (((1,), (1,)), ((), ()))

    @pl.when(b == 0)
    def _():
        hb = _rms(x_ref[...], g_ref[...]).astype(jnp.bfloat16)
        q_gain = gq_ref[...] * (HEAD_DIM ** -0.5 * LOG2E)
        qn = _head_rms(_bdot(hb, win_ref[:, 0:E_B]), gmat_ref, q_gain)
        for p in range(N_PAIRS):
            q_ref[p] = qn[:, p * LANES:(p + 1) * LANES].astype(jnp.bfloat16)
        z_ref[...] = _bdot(hb, win_ref[:, E_B:2 * E_B])

    rows = pl.ds(pl.multiple_of(b * qg, qg), qg)

    def scores(p):
        qb = q_ref[p, rows, :]
        zero = jnp.zeros_like(qb)
        lhs = jnp.concatenate([jnp.where(lane_lo, qb, zero),
                               jnp.where(lane_lo, zero, qb)], axis=0)
        sa = _bdot(lhs, ktc_ref[0, p].astype(jnp.bfloat16))
        sb = _bdot(lhs, ktn_ref[0, p, :, 0:qg])
        return jnp.concatenate([sa, sb], axis=1) + bias_ref[p]

    def attend(p, s):
        keys = vtc_ref.shape[-1]
        mx = jnp.max(s, axis=1, keepdims=True)
        e = jnp.exp2(s - mx)
        l = jnp.sum(e, axis=1, keepdims=True)
        pb = e.astype(jnp.bfloat16)
        oa = lax.dot_general(pb[:, 0:keys], vtc_ref[0, p].astype(jnp.bfloat16), contract_last,
                             preferred_element_type=jnp.float32)
        o2 = (oa + _bdot(pb[:, keys:keys + qg], vn_ref[0, p])) * (1.0 / l)
        o_ref[p, rows, :] = jnp.where(lane_lo, o2[0:qg], o2[qg:2 * qg])

    pending = {p: scores(p) for p in range(min(QK_AHEAD, N_PAIRS))}
    for p in range(N_PAIRS):
        if p + QK_AHEAD < N_PAIRS:
            pending[p + QK_AHEAD] = scores(p + QK_AHEAD)
        attend(p, pending.pop(p))

    @pl.when(b == n_seq - 1)
    def _():
        o = jnp.concatenate([o_ref[p] for p in range(N_PAIRS)], axis=1)
        yb = (o * _silu(z_ref[...])).astype(jnp.bfloat16)
        y_ref[...] = x_ref[...] + _bdot(yb, wout_ref[...])


def _attn_layer_cached(x, kt_cache, kt_new, vt_cache, v_new, bias, g_norm, w_in, gq_t, gmat,
                       w_out, *, layer):
    s, qg, _ = x.shape
    m = s * qg
    keys = kt_cache.shape[-1]
    whole = pl.BlockSpec((m, D_MODEL), lambda b: (0, 0))
    per_seq = lambda a: pl.BlockSpec((1,) + a.shape[1:], lambda b: (b, 0, 0, 0))
    in_specs = [whole, per_seq(kt_cache), per_seq(kt_new), per_seq(vt_cache), per_seq(v_new),
                _layer_spec(bias, layer), _layer_spec(g_norm, layer), _layer_spec(w_in, layer),
                _layer_spec(gq_t, layer), pl.BlockSpec((MXU_DIM, MXU_DIM), lambda b: (0, 0)),
                _layer_spec(w_out, layer)]
    assert bias.shape[-1] == keys + qg
    y = pl.pallas_call(
        functools.partial(_attn_cached_kernel, n_seq=s, qg=qg),
        out_shape=jax.ShapeDtypeStruct((m, D_MODEL), jnp.float32),
        grid=(s,),
        in_specs=in_specs,
        out_specs=whole,
        scratch_shapes=[pltpu.VMEM((N_PAIRS, m, LANES), jnp.bfloat16),
                        pltpu.VMEM((N_PAIRS, m, LANES), jnp.float32),
                        pltpu.VMEM((m, E_B), jnp.float32)],
        compiler_params=pltpu.CompilerParams(
            dimension_semantics=("arbitrary",), vmem_limit_bytes=VMEM_LIMIT_BYTES),
        name="attn_cached",
    )(x.reshape(m, D_MODEL), kt_cache, kt_new, vt_cache, v_new, bias, g_norm, w_in, gq_t, gmat,
      w_out)
    return y.reshape(x.shape)


def _trunk(x, hist16, cache_k, cache_v, pos0, w, *, pool_seg, pool_len, tile):
    s, t_len, _ = x.shape
    hists = []
    pool_names = ("w_in_a", "w_grp_a", "w_out_a")
    n_a = len(w["w_in_a"])
    for layer in range(n_a):
        side = []
        cast_next = layer + 1 < n_a and w["w_in_a"][layer + 1] is None
        cast_kv = w["w_kv"] is None
        if cast_next:
            side += [(w["f32"][name], layer + 1) for name in pool_names]
        if cast_kv:
            side.append((w["f32"]["w_kv"], None))
        x, h_out, *cast = _pool_layer(
            x, hist16, layer, (w["norm_a"], layer), (w["w_in_a"][layer], 0),
            (w["w_grp_a"][layer], 0), (w["scale_a"], layer), (w["w_out_a"][layer], 0),
            n_seg=pool_seg, seg_len=pool_len, pos0=pos0, side_casts=side)
        if cast_next:
            for name in pool_names:
                w[name][layer + 1] = cast.pop(0).reshape((1,) + w["shape"][name][1:])
        if cast_kv:
            w["w_kv"] = cast.pop(0)
        hists.append(h_out[:, HIST_PAD - POOL_HIST:, :])
    side = []
    if w["w_in_b"] is None:
        side = [(w["f32"][name].reshape(-1, w["f32"][name].shape[-1]), None)
                for name in ("w_in_b", "w_out_b")]
    k_new, v_new, kt, vb, *cast = _kv_proj(x, w["norm_kv"], w["w_kv"], w["gk_t"], w["gmat"],
                                           tm=min(pool_len, t_len), side_casts=side)
    if side:
        w["w_in_b"] = cast[0].reshape(w["f32"]["w_in_b"].shape)
        w["w_out_b"] = cast[1].reshape(w["f32"]["w_out_b"].shape)
    weights = (w["norm_b"], w["w_in_b"], w["gq_t"], w["gmat"], w["w_out_b"])
    if cache_k is None:
        for j in range(w["w_in_b"].shape[0]):
            x = _attn_layer(x, kt, vb, w["bias_prompt"], *weights, layer=j, tm=tile, qg=BIAS_Q)
    else:
        lc = cache_k.shape[1]
        as_pairs = lambda c: jnp.transpose(c, (0, 2, 3, 1)).reshape(s, N_PAIRS, LANES, lc)
        for j in range(w["w_in_b"].shape[0]):
            x = _attn_layer_cached(x, as_pairs(cache_k), kt, as_pairs(cache_v), vb,
                                   w["bias_sample"], *weights, layer=j)
    return x, jnp.stack(hists, axis=0), k_new, v_new


def kernel(x_prompt, x_sample, state_pool, cache_k, cache_v, norm_a, w_in_a, w_grp_a, scale_a,
           w_out_a, norm_kv, w_kv, g_k, norm_b, w_in_b, g_q, rel_bias_b, w_out_b):
    bf = jnp.bfloat16
    head_of_lane = jnp.arange(MXU_DIM, dtype=jnp.int32) // HEAD_DIM
    gmat = (head_of_lane[:, None] == head_of_lane[None, :]).astype(bf)
    n_b = rel_bias_b.shape[0]
    n_a = w_in_a.shape[0]
    w_grp_rows = w_grp_a.reshape(n_a, -1, G_A)
    pool_f32 = dict(w_in_a=w_in_a, w_grp_a=w_grp_rows, w_out_a=w_out_a)
    pool_shape = dict(w_in_a=w_in_a.shape, w_grp_a=w_grp_a.shape, w_out_a=w_out_a.shape)
    wide, narrow, *first = _bias_tables(rel_bias_b,
                                        side_casts=[(a, 0) for a in pool_f32.values()])
    pool_bf = {name: [c.reshape((1,) + pool_shape[name][1:])] + [None] * (n_a - 1)
               for name, c in zip(pool_f32, first)}
    pair_rows = HEADS_PER_LANE_BLOCK
    w = dict(
        f32=dict(w_kv=w_kv, w_in_b=w_in_b, w_out_b=w_out_b, **pool_f32), shape=pool_shape,
        norm_a=norm_a[:, None, :], scale_a=scale_a[:, None, :], **pool_bf,
        norm_kv=norm_kv[None, :], w_kv=None, gk_t=jnp.tile(g_k, N_HEADS)[None, :],
        norm_b=norm_b[:, None, :], w_in_b=None,
        gq_t=jnp.tile(g_q, (1, N_HEADS))[:, None, :], w_out_b=None, gmat=gmat,
        bias_prompt=wide.reshape(n_b, N_PAIRS, pair_rows * BIAS_Q, BIAS_K),
        bias_sample=narrow.reshape(n_b, N_PAIRS, pair_rows * CHUNK, CHUNK + KV_ROWS),
    )
    bp, sp, _ = x_prompt.shape
    bs, ss, _ = x_sample.shape
    y_p, pool_p, k_p, v_p = _trunk(x_prompt, None, None, None, 0, w,
                                   pool_seg=1, pool_len=min(POOL_TILE, sp), tile=KV_ROWS)
    hist16 = jnp.pad(state_pool, ((0, 0), (0, 0), (HIST_PAD - POOL_HIST, 0), (0, 0)))
    y_s, pool_s, k_s, v_s = _trunk(x_sample, hist16, cache_k, cache_v, PAST_LEN, w,
                                   pool_seg=bs, pool_len=ss, tile=ss)
    lp = min(KV_ROWS, sp)
    return (y_p, y_s, pool_p, pool_s,
            k_p.reshape(bp, lp, N_HEADS, HEAD_DIM), v_p.reshape(bp, lp, N_HEADS, HEAD_DIM),
            k_s.reshape(bs, ss, N_HEADS, HEAD_DIM), v_s.reshape(bs, ss, N_HEADS, HEAD_DIM))
```

```python
import functools
import math

import jax
import jax.numpy as jnp
from jax import lax
from jax.experimental import pallas as pl
from jax.experimental.pallas import tpu as pltpu

D_MODEL = 1024
E_A = 2048
POOL_WINDOWS = (2, 4, 8, 16)
N_POOL_GROUPS = len(POOL_WINDOWS)
G_A = E_A // N_POOL_GROUPS
POOL_HIST = max(POOL_WINDOWS) - 1
HEAD_DIM = 64
N_HEADS = 16
E_B = N_HEADS * HEAD_DIM
CHUNK = 64
N_LEFT_CHUNKS = 8
KV_ROWS = N_LEFT_CHUNKS * CHUNK
MAX_REL = 128
N_REL = 2 * MAX_REL + 1
EPS = 1e-6
NEG_INF = -1e30
PAST_LEN = 4096
LOG2E = math.log2(math.e)

SUBLANES = 8
LANES = 128
MXU_DIM = 256
HEADS_PER_LANE_BLOCK = LANES // HEAD_DIM
N_PAIRS = N_HEADS // HEADS_PER_LANE_BLOCK
HIST_PAD = 2 * SUBLANES
FRONT = SUBLANES
QK_AHEAD = 2
PIECE_COLS = 2 * MXU_DIM
POOL_TILE = 1024
POOL_SUB_TILE = 512
VMEM_LIMIT_BYTES = 56 * 1024 * 1024


def _bdot(a, b):
    return jnp.dot(a, b, preferred_element_type=jnp.float32)


def _rms(x, g):
    ms = jnp.mean(x * x, axis=-1, keepdims=True)
    return x * lax.rsqrt(ms + EPS) * g


def _silu(z):
    return z * (1.0 / (1.0 + jnp.exp(-z)))


def _side_cast_specs(arrays, n_steps, step_of):
    in_specs, out_specs, out_shapes = [], [], []
    for arr, lead in arrays:
        n_rows, n_cols = arr.shape[-2:]
        rows = n_rows // n_steps
        assert rows * n_steps == n_rows and rows % (2 * SUBLANES) == 0
        if lead is None:
            in_specs.append(pl.BlockSpec((rows, n_cols), lambda *i: (step_of(*i), 0)))
        else:
            in_specs.append(pl.BlockSpec((None, rows, n_cols),
                                         lambda *i, lead=lead: (lead, step_of(*i), 0)))
        out_specs.append(pl.BlockSpec((rows, n_cols), lambda *i: (step_of(*i), 0)))
        out_shapes.append(jax.ShapeDtypeStruct((n_rows, n_cols), jnp.bfloat16))
    return in_specs, out_specs, out_shapes


def _pool_layer_kernel(*refs, n_seg, seg_len, sub_len, pos0, has_hist, n_side):
    it = iter(refs)
    x_ref = next(it)
    hist_ref = next(it) if has_hist else None
    g_ref, win_ref, wgrp_ref, scale_ref, wout_ref = (next(it) for _ in range(5))
    side_in = [next(it) for _ in range(n_side)]
    y_ref, hout_ref = next(it), next(it)
    side_out = [next(it) for _ in range(n_side)]
    acc_ref, upad_ref, s0_ref, s1_ref, carry_ref = it
    for src, dst in zip(side_in, side_out):
        dst[...] = src[...].astype(dst.dtype)
    t = pl.program_id(1)
    m = n_seg * sub_len
    rows = HIST_PAD + sub_len

    @pl.when(t == 0)
    def _():
        if has_hist:
            carry_ref[...] = hist_ref[...]
        else:
            carry_ref[...] = jnp.zeros_like(carry_ref)

    zero_front = jnp.zeros((n_seg, FRONT, G_A), jnp.float32)
    upad_ref[:, 0:FRONT, :] = zero_front
    s0_ref[:, 0:FRONT, :] = zero_front
    s1_ref[:, 0:FRONT, :] = zero_front

    for r0 in range(0, seg_len, sub_len):
        x = x_ref[:, r0:r0 + sub_len, :].reshape(m, D_MODEL)
        hb = _rms(x, g_ref[...]).astype(jnp.bfloat16)
        pos = (pos0 + t * seg_len + r0
               + lax.broadcasted_iota(jnp.int32, (1, sub_len, 1), 1))
        for g, w in enumerate(POOL_WINDOWS):
            lo = g * G_A
            u = _bdot(hb, win_ref[:, lo:lo + G_A])
            z = _bdot(hb, win_ref[:, E_A + lo:E_A + lo + G_A])
            u3 = u.reshape(n_seg, sub_len, G_A)
            upad_ref[:, FRONT:FRONT + HIST_PAD, :] = carry_ref[:, :, lo:lo + G_A]
            upad_ref[:, FRONT + HIST_PAD:, :] = u3
            carry_ref[:, :, lo:lo + G_A] = upad_ref[:, FRONT + sub_len:, :]
            bufs = (upad_ref, s0_ref, s1_ref)
            src = 0
            for k in range(g + 1):
                dst = 1 if src != 1 else 2
                sh = 1 << k
                bufs[dst][:, FRONT:, :] = (bufs[src][:, FRONT:, :]
                                           + bufs[src][:, FRONT - sh:FRONT - sh + rows, :])
                src = dst
            wsum = bufs[src][:, FRONT + HIST_PAD:, :]
            inv_cnt = 1.0 / jnp.minimum(pos + 1, w).astype(jnp.float32)
            pooled = (wsum * inv_cnt - u3).reshape(m, G_A)
            pg = _bdot(pooled.astype(jnp.bfloat16), wgrp_ref[g]) * scale_ref[:, lo:lo + G_A]
            yb = (pg * _silu(z)).astype(jnp.bfloat16)
            contrib = _bdot(yb, wout_ref[lo:lo + G_A, :])
            if g == 0:
                acc_ref[...] = contrib
            else:
                acc_ref[...] += contrib
        y_ref[:, r0:r0 + sub_len, :] = (x + acc_ref[...]).reshape(n_seg, sub_len, D_MODEL)
    hout_ref[...] = carry_ref[...]


def _layer_spec(arr, layer):
    tail = (0,) * (arr.ndim - 1)
    return pl.BlockSpec((None,) + arr.shape[1:], lambda *_: (layer,) + tail,
                        pipeline_mode=pl.Buffered(1))


def _pool_layer(x, hist16, layer, g_norm, w_in, w_grp, scale, w_out, *, n_seg, seg_len, pos0,
                side_casts=()):
    s, t_len, _ = x.shape
    has_hist = hist16 is not None
    grid = (s // n_seg, t_len // seg_len)
    in_specs = [pl.BlockSpec((n_seg, seg_len, D_MODEL), lambda b, t: (b, t, 0))]
    args = [x]
    if has_hist:
        in_specs.append(pl.BlockSpec((None, n_seg, HIST_PAD, E_A), lambda b, t: (layer, b, 0, 0)))
        args.append(hist16)
    for arr, idx in (g_norm, w_in, w_grp, scale, w_out):
        args.append(arr)
        in_specs.append(_layer_spec(arr, idx))
    side_in, side_out, side_shapes = _side_cast_specs(
        side_casts, grid[0] * grid[1], lambda b, t: b * grid[1] + t)
    args += [arr for arr, _ in side_casts]
    in_specs += side_in
    sub_len = min(seg_len, POOL_SUB_TILE)
    m = n_seg * sub_len
    stage = pltpu.VMEM((n_seg, FRONT + HIST_PAD + sub_len, G_A), jnp.float32)
    return pl.pallas_call(
        functools.partial(_pool_layer_kernel, n_seg=n_seg, seg_len=seg_len, sub_len=sub_len,
                          pos0=pos0, has_hist=has_hist, n_side=len(side_casts)),
        out_shape=(jax.ShapeDtypeStruct(x.shape, jnp.float32),
                   jax.ShapeDtypeStruct((s, HIST_PAD, E_A), jnp.float32), *side_shapes),
        grid=grid,
        in_specs=in_specs,
        out_specs=(pl.BlockSpec((n_seg, seg_len, D_MODEL), lambda b, t: (b, t, 0)),
                   pl.BlockSpec((n_seg, HIST_PAD, E_A), lambda b, t: (b, 0, 0)), *side_out),
        scratch_shapes=[pltpu.VMEM((m, D_MODEL), jnp.float32), stage, stage, stage,
                        pltpu.VMEM((n_seg, HIST_PAD, E_A), jnp.float32)],
        compiler_params=pltpu.CompilerParams(
            dimension_semantics=("arbitrary", "arbitrary"),
            vmem_limit_bytes=VMEM_LIMIT_BYTES),
        name="pool_layer",
    )(*args)


def _head_rms(v, gmat_ref, gain):
    sq = (v * v).astype(jnp.bfloat16)
    ss = jnp.concatenate([_bdot(sq[:, c:c + MXU_DIM], gmat_ref[...])
                          for c in range(0, v.shape[-1], MXU_DIM)], axis=1)
    return v * lax.rsqrt(ss * (1.0 / HEAD_DIM) + EPS) * gain


def _kv_kernel(*refs, tm, n_side):
    it = iter(refs)
    x_ref, g_ref, wkv_ref, gk_ref, gmat_ref = (next(it) for _ in range(5))
    side_in = [next(it) for _ in range(n_side)]
    kout_ref, vout_ref, kt_ref, vb_ref = (next(it) for _ in range(4))
    for src, dst in zip(side_in, it):
        dst[...] = src[...].astype(dst.dtype)
    keep = kout_ref.shape[1]
    sub = min(tm, keep)
    for r0 in range(0, tm, sub):
        x = x_ref[0, r0:r0 + sub, :]
        hb = _rms(x, g_ref[...]).astype(jnp.bfloat16)
        k = _bdot(hb, wkv_ref[:, 0:E_B])
        kn = _head_rms(k, gmat_ref, gk_ref[...])
        v = _bdot(hb, wkv_ref[:, E_B:2 * E_B])
        if r0 + sub == tm:
            kout_ref[0] = kn
            vout_ref[0] = v
        for p in range(N_PAIRS):
            blk = kn[:, p * LANES:(p + 1) * LANES]
            if sub < LANES:
                blk = jnp.concatenate([blk, jnp.zeros((LANES - sub, LANES), jnp.float32)],
                                      axis=0)
                kt_ref[0, p] = blk.T.astype(jnp.bfloat16)
            else:
                kt_ref[0, p, :, r0:r0 + sub] = blk.T.astype(jnp.bfloat16)
            vb_ref[0, p, r0:r0 + sub, :] = v[:, p * LANES:(p + 1) * LANES].astype(jnp.bfloat16)


def _kv_proj(x, g_kv, w_kv, gk_t, gmat, *, tm, side_casts=()):
    s, t_len, _ = x.shape
    n_t = t_len // tm
    keep = min(KV_ROWS, t_len)
    assert tm % keep == 0
    kt_w = max(tm, LANES)
    const2 = lambda b, t: (0, 0)
    in_specs = [pl.BlockSpec((1, tm, D_MODEL), lambda b, t: (b, t, 0)),
                pl.BlockSpec((1, D_MODEL), const2),
                pl.BlockSpec((D_MODEL, 2 * E_B), const2),
                pl.BlockSpec((1, E_B), const2),
                pl.BlockSpec((MXU_DIM, MXU_DIM), const2)]
    args = [x, g_kv, w_kv, gk_t, gmat]
    out_shape = [jax.ShapeDtypeStruct((s, keep, E_B), jnp.float32),
                 jax.ShapeDtypeStruct((s, keep, E_B), jnp.float32),
                 jax.ShapeDtypeStruct((s, N_PAIRS, LANES, n_t * kt_w), jnp.bfloat16),
                 jax.ShapeDtypeStruct((s, N_PAIRS, t_len, LANES), jnp.bfloat16)]
    out_specs = [pl.BlockSpec((1, keep, E_B), lambda b, t: (b, 0, 0)),
                 pl.BlockSpec((1, keep, E_B), lambda b, t: (b, 0, 0)),
                 pl.BlockSpec((1, N_PAIRS, LANES, kt_w), lambda b, t: (b, 0, 0, t)),
                 pl.BlockSpec((1, N_PAIRS, tm, LANES), lambda b, t: (b, 0, t, 0))]
    side_in, side_out, side_shapes = _side_cast_specs(side_casts, s * n_t,
                                                      lambda b, t: b * n_t + t)
    args += [arr for arr, _ in side_casts]
    in_specs += side_in
    out_shape += side_shapes
    out_specs += side_out
    return pl.pallas_call(
        functools.partial(_kv_kernel, tm=tm, n_side=len(side_casts)),
        out_shape=tuple(out_shape),
        grid=(s, n_t),
        in_specs=in_specs,
        out_specs=tuple(out_specs),
        compiler_params=pltpu.CompilerParams(
            dimension_semantics=("arbitrary", "arbitrary"),
            vmem_limit_bytes=VMEM_LIMIT_BYTES),
        name="kv_proj",
    )(*args)


BIAS_Q = 2 * CHUNK
BIAS_K = BIAS_Q + KV_ROWS
BIAS_BASE = 1024


BIAS_HEADS_PER_STEP = 4


def _bias_kernel(base_ref, *refs, n_side):
    side_in = refs[:n_side]
    wide_ref, narrow_ref = refs[n_side:n_side + 2]
    for src, dst in zip(side_in, refs[n_side + 2:]):
        dst[...] = src[...].astype(dst.dtype)
    qc = lax.broadcasted_iota(jnp.int32, (BIAS_Q, BIAS_K), 0) // CHUNK
    kc = lax.broadcasted_iota(jnp.int32, (BIAS_Q, BIAS_K), 1) // CHUNK
    valid = (kc >= qc) & (kc <= qc + N_LEFT_CHUNKS)
    for h in range(BIAS_HEADS_PER_STEP):
        base = base_ref[h] * LOG2E
        tiled = jnp.broadcast_to(base, (BIAS_Q, BIAS_BASE))
        toep = pltpu.roll(tiled, 0, 1, stride=1, stride_axis=0)[:, 0:BIAS_K]
        table = jnp.where(valid, toep, NEG_INF)
        wide_ref[h] = table
        narrow_ref[h] = table[0:CHUNK, 0:CHUNK + KV_ROWS]


def _bias_tables(rel_bias, side_casts=()):
    n = rel_bias.shape[0] * N_HEADS
    f = jnp.transpose(rel_bias, (0, 2, 1)).reshape(n, N_REL)
    rev = f[:, ::-1]
    far = f[:, N_REL - 1:N_REL]
    n_left = KV_ROWS - MAX_REL
    base = jnp.concatenate(
        [jnp.broadcast_to(far, (n, n_left)), rev,
         jnp.broadcast_to(far, (n, BIAS_BASE - n_left - N_REL))], axis=1)
    base = base.reshape(n, 1, BIAS_BASE)
    hs = BIAS_HEADS_PER_STEP
    side_in, side_out, side_shapes = _side_cast_specs(side_casts, n // hs, lambda h: h)
    return pl.pallas_call(
        functools.partial(_bias_kernel, n_side=len(side_casts)),
        out_shape=(jax.ShapeDtypeStruct((n, BIAS_Q, BIAS_K), jnp.float32),
                   jax.ShapeDtypeStruct((n, CHUNK, CHUNK + KV_ROWS), jnp.float32), *side_shapes),
        grid=(n // hs,),
        in_specs=[pl.BlockSpec((hs, 1, BIAS_BASE), lambda h: (h, 0, 0)), *side_in],
        out_specs=(pl.BlockSpec((hs, BIAS_Q, BIAS_K), lambda h: (h, 0, 0)),
                   pl.BlockSpec((hs, CHUNK, CHUNK + KV_ROWS), lambda h: (h, 0, 0)), *side_out),
        compiler_params=pltpu.CompilerParams(dimension_semantics=("arbitrary",),
                                             vmem_limit_bytes=VMEM_LIMIT_BYTES),
        name="bias_table",
    )(base, *[arr for arr, _ in side_casts])


def _attn_layer_kernel(*refs, tm, qg):
    tile = functools.partial(_attn_tile, *refs, tm=tm, qg=qg)
    first = pl.program_id(1) == 0
    pl.when(first)(functools.partial(tile, has_a=False))
    pl.when(jnp.logical_not(first))(functools.partial(tile, has_a=True))


def _attn_tile(x_ref, kta_ref, ktb_ref, va_ref, vb_ref, bias_ref,
               g_ref, win_ref, gq_ref, gmat_ref, wout_ref,
               y_ref, q_ref, o_ref, z_ref, *, tm, qg, has_a):
    n_grp = tm // qg
    cb = PIECE_COLS
    n_cb = E_B // cb
    pairs_per_cb = cb // LANES
    x = x_ref[0]
    hb = _rms(x, g_ref[...]).astype(jnp.bfloat16)
    q_gain = gq_ref[...] * (HEAD_DIM ** -0.5 * LOG2E)

    def rows_of(j):
        return slice(j * qg, (j + 1) * qg)

    def q_piece(c):
        cols = slice(c * cb, (c + 1) * cb)
        qn = _head_rms(_bdot(hb, win_ref[:, cols]), gmat_ref, q_gain[:, cols])
        for h in range(pairs_per_cb):
            q_ref[c * pairs_per_cb + h] = qn[:, h * LANES:(h + 1) * LANES].astype(jnp.bfloat16)

    def z_piece(c):
        cols = slice(c * cb, (c + 1) * cb)
        z_ref[:, cols] = _bdot(hb, win_ref[:, E_B + c * cb:E_B + (c + 1) * cb])

    def out_piece(c):
        cols = slice(c * cb, (c + 1) * cb)
        o = jnp.concatenate([o_ref[c * pairs_per_cb + h] for h in range(pairs_per_cb)],
                            axis=1)
        yb = (o * _silu(z_ref[:, cols])).astype(jnp.bfloat16)
        contrib = _bdot(yb, wout_ref[cols, :])
        if c == 0:
            y_ref[0] = x + contrib
        else:
            y_ref[0] += contrib

    lane_lo = lax.broadcasted_iota(jnp.int32, (1, LANES), 1) < HEAD_DIM
    widths = [(KV_ROWS - j * qg, (j + 1) * qg) for j in range(n_grp)]

    def scores(p, j):
        wa, wbj = widths[j]
        qb = q_ref[p, j * qg:(j + 1) * qg, :]
        zero = jnp.zeros_like(qb)
        lhs = jnp.concatenate([jnp.where(lane_lo, qb, zero),
                               jnp.where(lane_lo, zero, qb)], axis=0)
        s = _bdot(lhs, ktb_ref[0, p, :, 0:wbj])
        if has_a:
            sa = _bdot(lhs, kta_ref[0, p, :, KV_ROWS - wa:KV_ROWS])
            return jnp.concatenate([sa, s], axis=1) + bias_ref[p]
        return s + bias_ref[p, :, wa:wa + wbj]

    def attend(p, j, s):
        wa, wbj = widths[j]
        mx = jnp.max(s, axis=1, keepdims=True)
        e = jnp.exp2(s - mx)
        l = jnp.sum(e, axis=1, keepdims=True)
        pb = e.astype(jnp.bfloat16)
        if has_a:
            o2 = (_bdot(pb[:, 0:wa], va_ref[0, p, KV_ROWS - wa:KV_ROWS, :])
                  + _bdot(pb[:, wa:wa + wbj], vb_ref[0, p, 0:wbj, :]))
        else:
            o2 = _bdot(pb, vb_ref[0, p, 0:wbj, :])
        o2 = o2 * (1.0 / l)
        o_ref[p, rows_of(j), :] = jnp.where(lane_lo, o2[0:qg], o2[qg:2 * qg])

    items = [(c * pairs_per_cb + h, j) for c in range(n_cb) for j in range(n_grp)
             for h in range(pairs_per_cb)]
    per_cb = len(items) // n_cb
    pieces = []
    for c in range(n_cb):
        if c > 0:
            pieces.append((per_cb * c - QK_AHEAD - 1, 0, len(pieces), q_piece, c))
        pieces.append((per_cb * c + per_cb // 2, 0, len(pieces), z_piece, c))
        if c + 1 < n_cb:
            pieces.append((per_cb * (c + 1) + per_cb // 2, per_cb * (c + 1), len(pieces),
                           out_piece, c))
    n_pieces = len(pieces)
    q_piece(0)
    pending = {}
    for i in range(min(QK_AHEAD, len(items))):
        pending[i] = scores(*items[i])
    issued = 0
    for i in range(len(items)):
        quota = ((i + 1) * n_pieces) // len(items) - issued
        ready = sorted(pc for pc in pieces if pc[1] <= i)
        take = [pc for pc in ready if pc[0] <= i]
        take += [pc for pc in ready if pc[0] > i][:max(0, quota - len(take))]
        for pc in take:
            pieces.remove(pc)
            pc[3](pc[4])
            issued += 1
        if i + QK_AHEAD < len(items):
            pending[i + QK_AHEAD] = scores(*items[i + QK_AHEAD])
        attend(*items[i], pending.pop(i))
    for pc in sorted(pieces):
        pc[3](pc[4])
    out_piece(n_cb - 1)


def _attn_layer(x, kt, v, bias, g_norm, w_in, gq_t, gmat, w_out, *, layer, tm, qg):
    s, t_len, _ = x.shape
    n_t = t_len // tm
    const2 = lambda b, t: (0, 0)
    assert tm == KV_ROWS and kt.shape[-1] == t_len
    prev = lambda t: jnp.maximum(t - 1, 0)
    in_specs = [
        pl.BlockSpec((1, tm, D_MODEL), lambda b, t: (b, t, 0)),
        pl.BlockSpec((1, N_PAIRS, LANES, KV_ROWS), lambda b, t: (b, 0, 0, prev(t))),
        pl.BlockSpec((1, N_PAIRS, LANES, tm), lambda b, t: (b, 0, 0, t)),
        pl.BlockSpec((1, N_PAIRS, KV_ROWS, LANES), lambda b, t: (b, 0, prev(t), 0)),
        pl.BlockSpec((1, N_PAIRS, tm, LANES), lambda b, t: (b, 0, t, 0)),
        _layer_spec(bias, layer),
        _layer_spec(g_norm, layer),
        _layer_spec(w_in, layer),
        _layer_spec(gq_t, layer),
        pl.BlockSpec((MXU_DIM, MXU_DIM), const2),
        _layer_spec(w_out, layer),
    ]
    return pl.pallas_call(
        functools.partial(_attn_layer_kernel, tm=tm, qg=qg),
        out_shape=jax.ShapeDtypeStruct(x.shape, jnp.float32),
        grid=(s, n_t),
        in_specs=in_specs,
        out_specs=pl.BlockSpec((1, tm, D_MODEL), lambda b, t: (b, t, 0)),
        scratch_shapes=[pltpu.VMEM((N_PAIRS, tm, LANES), jnp.bfloat16),
                        pltpu.VMEM((N_PAIRS, tm, LANES), jnp.float32),
                        pltpu.VMEM((tm, E_B), jnp.float32)],
        compiler_params=pltpu.CompilerParams(
            dimension_semantics=("arbitrary", "arbitrary"),
            vmem_limit_bytes=VMEM_LIMIT_BYTES),
        name="attn_layer",
    )(x, kt, kt, v, v, bias, g_norm, w_in, gq_t, gmat, w_out)


def _attn_cached_kernel(x_ref, ktc_ref, ktn_ref, vtc_ref, vn_ref, bias_ref,
                        g_ref, win_ref, gq_ref, gmat_ref, wout_ref,
                        y_ref, q_ref, o_ref, z_ref, *, n_seq, qg):
    b = pl.program_id(0)
    lane_lo = lax.broadcasted_iota(jnp.int32, (1, LANES), 1) < HEAD_DIM
    contract_last = (((1,), (1,)), ((), ()))

    @pl.when(b == 0)
    def _():
        hb = _rms(x_ref[...], g_ref[...]).astype(jnp.bfloat16)
        q_gain = gq_ref[...] * (HEAD_DIM ** -0.5 * LOG2E)
        qn = _head_rms(_bdot(hb, win_ref[:, 0:E_B]), gmat_ref, q_gain)
        for p in range(N_PAIRS):
            q_ref[p] = qn[:, p * LANES:(p + 1) * LANES].astype(jnp.bfloat16)
        z_ref[...] = _bdot(hb, win_ref[:, E_B:2 * E_B])

    rows = pl.ds(pl.multiple_of(b * qg, qg), qg)

    def scores(p):
        qb = q_ref[p, rows, :]
        zero = jnp.zeros_like(qb)
        lhs = jnp.concatenate([jnp.where(lane_lo, qb, zero),
                               jnp.where(lane_lo, zero, qb)], axis=0)
        sa = _bdot(lhs, ktc_ref[0, p].astype(jnp.bfloat16))
        sb = _bdot(lhs, ktn_ref[0, p, :, 0:qg])
        return jnp.concatenate([sa, sb], axis=1) + bias_ref[p]

    def attend(p, s):
        keys = vtc_ref.shape[-1]
        mx = jnp.max(s, axis=1, keepdims=True)
        e = jnp.exp2(s - mx)
        l = jnp.sum(e, axis=1, keepdims=True)
        pb = e.astype(jnp.bfloat16)
        oa = lax.dot_general(pb[:, 0:keys], vtc_ref[0, p].astype(jnp.bfloat16), contract_last,
                             preferred_element_type=jnp.float32)
        o2 = (oa + _bdot(pb[:, keys:keys + qg], vn_ref[0, p])) * (1.0 / l)
        o_ref[p, rows, :] = jnp.where(lane_lo, o2[0:qg], o2[qg:2 * qg])

    pending = {p: scores(p) for p in range(min(QK_AHEAD, N_PAIRS))}
    for p in range(N_PAIRS):
        if p + QK_AHEAD < N_PAIRS:
            pending[p + QK_AHEAD] = scores(p + QK_AHEAD)
        attend(p, pending.pop(p))

    @pl.when(b == n_seq - 1)
    def _():
        o = jnp.concatenate([o_ref[p] for p in range(N_PAIRS)], axis=1)
        yb = (o * _silu(z_ref[...])).astype(jnp.bfloat16)
        y_ref[...] = x_ref[...] + _bdot(yb, wout_ref[...])


def _attn_layer_cached(x, kt_cache, kt_new, vt_cache, v_new, bias, g_norm, w_in, gq_t, gmat,
                       w_out, *, layer):
    s, qg, _ = x.shape
    m = s * qg
    keys = kt_cache.shape[-1]
    whole = pl.BlockSpec((m, D_MODEL), lambda b: (0, 0))
    per_seq = lambda a: pl.BlockSpec((1,) + a.shape[1:], lambda b: (b, 0, 0, 0))
    in_specs = [whole, per_seq(kt_cache), per_seq(kt_new), per_seq(vt_cache), per_seq(v_new),
                _layer_spec(bias, layer), _layer_spec(g_norm, layer), _layer_spec(w_in, layer),
                _layer_spec(gq_t, layer), pl.BlockSpec((MXU_DIM, MXU_DIM), lambda b: (0, 0)),
                _layer_spec(w_out, layer)]
    assert bias.shape[-1] == keys + qg
    y = pl.pallas_call(
        functools.partial(_attn_cached_kernel, n_seq=s, qg=qg),
        out_shape=jax.ShapeDtypeStruct((m, D_MODEL), jnp.float32),
        grid=(s,),
        in_specs=in_specs,
        out_specs=whole,
        scratch_shapes=[pltpu.VMEM((N_PAIRS, m, LANES), jnp.bfloat16),
                        pltpu.VMEM((N_PAIRS, m, LANES), jnp.float32),
                        pltpu.VMEM((m, E_B), jnp.float32)],
        compiler_params=pltpu.CompilerParams(
            dimension_semantics=("arbitrary",), vmem_limit_bytes=VMEM_LIMIT_BYTES),
        name="attn_cached",
    )(x.reshape(m, D_MODEL), kt_cache, kt_new, vt_cache, v_new, bias, g_norm, w_in, gq_t, gmat,
      w_out)
    return y.reshape(x.shape)


def _trunk(x, hist16, cache_k, cache_v, pos0, w, *, pool_seg, pool_len, tile):
    s, t_len, _ = x.shape
    hists = []
    pool_names = ("w_in_a", "w_grp_a", "w_out_a")
    n_a = len(w["w_in_a"])
    for layer in range(n_a):
        side = []
        cast_next = layer + 1 < n_a and w["w_in_a"][layer + 1] is None
        cast_kv = w["w_kv"] is None
        if cast_next:
            side += [(w["f32"][name], layer + 1) for name in pool_names]
        if cast_kv:
            side.append((w["f32"]["w_kv"], None))
        x, h_out, *cast = _pool_layer(
            x, hist16, layer, (w["norm_a"], layer), (w["w_in_a"][layer], 0),
            (w["w_grp_a"][layer], 0), (w["scale_a"], layer), (w["w_out_a"][layer], 0),
            n_seg=pool_seg, seg_len=pool_len, pos0=pos0, side_casts=side)
        if cast_next:
            for name in pool_names:
                w[name][layer + 1] = cast.pop(0).reshape((1,) + w["shape"][name][1:])
        if cast_kv:
            w["w_kv"] = cast.pop(0)
        hists.append(h_out[:, HIST_PAD - POOL_HIST:, :])
    side_names = []
    if w["w_in_b"] is None:
        side_names = ["w_in_b", "w_out_b"] + [n for n in ("cache_kt", "cache_vt") if n in w["f32"]]
    side = [(w["f32"][name].reshape(-1, w["f32"][name].shape[-1]), None) for name in side_names]
    k_new, v_new, kt, vb, *cast = _kv_proj(x, w["norm_kv"], w["w_kv"], w["gk_t"], w["gmat"],
                                           tm=min(pool_len, t_len), side_casts=side)
    for name, c in zip(side_names, cast):
        w[name] = c.reshape(w["f32"][name].shape)
    weights = (w["norm_b"], w["w_in_b"], w["gq_t"], w["gmat"], w["w_out_b"])
    if cache_k is None:
        for j in range(w["w_in_b"].shape[0]):
            x = _attn_layer(x, kt, vb, w["bias_prompt"], *weights, layer=j, tm=tile, qg=BIAS_Q)
    else:
        for j in range(w["w_in_b"].shape[0]):
            x = _attn_layer_cached(x, w["cache_kt"], kt, w["cache_vt"], vb,
                                   w["bias_sample"], *weights, layer=j)
    return x, jnp.stack(hists, axis=0), k_new, v_new


def kernel(x_prompt, x_sample, state_pool, cache_k, cache_v, norm_a, w_in_a, w_grp_a, scale_a,
           w_out_a, norm_kv, w_kv, g_k, norm_b, w_in_b, g_q, rel_bias_b, w_out_b):
    bf = jnp.bfloat16
    head_of_lane = jnp.arange(MXU_DIM, dtype=jnp.int32) // HEAD_DIM
    gmat = (head_of_lane[:, None] == head_of_lane[None, :]).astype(bf)
    n_b = rel_bias_b.shape[0]
    n_a = w_in_a.shape[0]
    w_grp_rows = w_grp_a.reshape(n_a, -1, G_A)
    pool_f32 = dict(w_in_a=w_in_a, w_grp_a=w_grp_rows, w_out_a=w_out_a)
    pool_shape = dict(w_in_a=w_in_a.shape, w_grp_a=w_grp_a.shape, w_out_a=w_out_a.shape)
    wide, narrow, *first = _bias_tables(rel_bias_b,
                                        side_casts=[(a, 0) for a in pool_f32.values()])
    pool_bf = {name: [c.reshape((1,) + pool_shape[name][1:])] + [None] * (n_a - 1)
               for name, c in zip(pool_f32, first)}
    pair_rows = HEADS_PER_LANE_BLOCK
    as_pairs = lambda c: jnp.transpose(c, (0, 2, 3, 1)).reshape(
        c.shape[0], N_PAIRS, LANES, c.shape[1])
    w = dict(
        f32=dict(w_kv=w_kv, w_in_b=w_in_b, w_out_b=w_out_b, cache_kt=as_pairs(cache_k),
                 cache_vt=as_pairs(cache_v), **pool_f32), shape=pool_shape,
        norm_a=norm_a[:, None, :], scale_a=scale_a[:, None, :], **pool_bf,
        norm_kv=norm_kv[None, :], w_kv=None, gk_t=jnp.tile(g_k, N_HEADS)[None, :],
        norm_b=norm_b[:, None, :], w_in_b=None,
        gq_t=jnp.tile(g_q, (1, N_HEADS))[:, None, :], w_out_b=None, gmat=gmat,
        bias_prompt=wide.reshape(n_b, N_PAIRS, pair_rows * BIAS_Q, BIAS_K),
        bias_sample=narrow.reshape(n_b, N_PAIRS, pair_rows * CHUNK, CHUNK + KV_ROWS),
    )
    bp, sp, _ = x_prompt.shape
    bs, ss, _ = x_sample.shape
    y_p, pool_p, k_p, v_p = _trunk(x_prompt, None, None, None, 0, w,
                                   pool_seg=1, pool_len=min(POOL_TILE, sp), tile=KV_ROWS)
    hist16 = jnp.pad(state_pool, ((0, 0), (0, 0), (HIST_PAD - POOL_HIST, 0), (0, 0)))
    y_s, pool_s, k_s, v_s = _trunk(x_sample, hist16, cache_k, cache_v, PAST_LEN, w,
                                   pool_seg=bs, pool_len=ss, tile=ss)
    lp = min(KV_ROWS, sp)
    return (y_p, y_s, pool_p, pool_s,
            k_p.reshape(bp, lp, N_HEADS, HEAD_DIM), v_p.reshape(bp, lp, N_HEADS, HEAD_DIM),
            k_s.reshape(bs, ss, N_HEADS, HEAD_DIM), v_s.reshape(bs, ss, N_HEADS, HEAD_DIM))
```
